```python
import jax, jax.numpy as jnp
from jax import lax
import numpy as np

D_MODEL = 2048
BATCH = 4
SEQ = 4096
DEPTH = 2

CHUNK = 128
A_GROUPS = 8
A_WIDTH = 1024
A_GROUP_DIM = A_WIDTH // A_GROUPS
B_HEADS = 8
B_HEAD_DIM = 128
B_WIDTH = B_HEADS * B_HEAD_DIM
Q_BLOCK = 128
EVEN_IN = 2 * A_WIDTH + 3 * B_WIDTH + B_HEADS
EVEN_OUT = A_WIDTH + B_WIDTH
C_HEAD_DIM = 64
C_HEADS = D_MODEL // C_HEAD_DIM
LORA_W = 96
LORA_A = 96
LORA_G = 256
N_MIX = 6
N_EXPERTS = 16
N_GROUPS = 4
EXPERTS_PER_GROUP = N_EXPERTS // N_GROUPS
TOP_K = 2
D_EXPERT = 1024
ALPHA = (2 * DEPTH) ** 0.25
BETA = (8 * DEPTH) ** -0.25
LN_EPS = 1e-5
GN_EPS = 64e-5
N_EVEN = (DEPTH + 1) // 2
N_ODD = DEPTH // 2

kernel_name = 'hybrid_sgu_fox_rwkv7_grouped_moe_deepnorm'


def layer_norm(x, g, b, eps=LN_EPS):
    xf = x.astype(jnp.float32)
    mu = jnp.mean(xf, -1, keepdims=True)
    var = jnp.mean(jnp.square(xf - mu), -1, keepdims=True)
    return ((xf - mu) * lax.rsqrt(var + eps)).astype(x.dtype) * g + b


def spatial_gating(z, w_s, b_s, g_v, b_v):
    Bsz, T, _ = z.shape
    u, v = jnp.split(z, 2, axis=-1)
    v = layer_norm(v, g_v, b_v)
    v = v.reshape(Bsz, T // CHUNK, CHUNK, A_GROUPS, A_GROUP_DIM)
    causal = jnp.tril(jnp.ones((CHUNK, CHUNK), dtype=bool))
    w = jnp.where(causal, w_s, 0)
    s = jnp.einsum('gts,bcsgd->bctgd', w, v) + b_s.T[None, None, :, :, None]
    return u * s.reshape(Bsz, T, A_WIDTH)


def forgetting_attention(q, k, v, f_logit):
    Bsz, T, H, Dh = q.shape
    log_f = jax.nn.log_sigmoid(f_logit.astype(jnp.float32))
    c = jnp.cumsum(log_f, axis=1).transpose(0, 2, 1)
    scale = Dh ** -0.5
    n_blocks = T // Q_BLOCK
    qb = q.reshape(Bsz, n_blocks, Q_BLOCK, H, Dh).transpose(1, 0, 3, 2, 4)
    cb = c.reshape(Bsz, H, n_blocks, Q_BLOCK).transpose(2, 0, 1, 3)
    k_pos = jnp.arange(T)

    def block(args):
        i, q_i, c_i = args
        s = jnp.einsum('bhqd,bkhd->bhqk', q_i, k).astype(jnp.float32) * scale
        s = s + c_i[..., :, None] - c[:, :, None, :]
        q_pos = i * Q_BLOCK + jnp.arange(Q_BLOCK)
        s = jnp.where(q_pos[:, None] >= k_pos[None, :], s, -jnp.inf)
        p = jax.nn.softmax(s, axis=-1).astype(v.dtype)
        return jnp.einsum('bhqk,bkhd->bqhd', p, v)

    out = lax.map(block, (jnp.arange(n_blocks), qb, cb))
    return out.transpose(1, 0, 2, 3, 4).reshape(Bsz, T, H * Dh)


def even_mixer(x, w_in, b_a, w_s, b_s, g_v, b_v, b_f, w_out):
    Bsz, T, _ = x.shape
    proj = x @ w_in
    o1 = 2 * A_WIDTH
    z_a, q, k, v, f_logit = jnp.split(proj, [o1, o1 + B_WIDTH, o1 + 2 * B_WIDTH, o1 + 3 * B_WIDTH], axis=-1)
    y_a = spatial_gating(jax.nn.gelu(z_a + b_a), w_s, b_s, g_v, b_v)
    heads = lambda t: t.reshape(Bsz, T, B_HEADS, B_HEAD_DIM)
    y_b = forgetting_attention(heads(q), heads(k), heads(v), f_logit + b_f)
    return jnp.concatenate([y_a, y_b], axis=-1) @ w_out


def rwkv7_step(S, inp):
    r_t, w_t, k_t, v_t, a_t, b_t = inp
    sa = jnp.einsum('bhvk,bhk->bhv', S, a_t)
    S = S * w_t[:, :, None, :] + sa[..., None] * b_t[:, :, None, :] + v_t[..., None] * k_t[:, :, None, :]
    return S, jnp.einsum('bhvk,bhk->bhv', S, r_t)


def rwkv7_time_mix(x, mu, w_rkv, w0, w1, w2, a0, a1, a2, g1, g2, k_k, k_a, r_k, gn_g, gn_b, w_o):
    Bsz, T, D = x.shape
    H, N = C_HEADS, C_HEAD_DIM
    f32 = jnp.float32
    x_prev = jnp.pad(x, ((0, 0), (1, 0), (0, 0)))[:, :-1]
    xm = x[None] + (x_prev - x)[None] * mu[:, None, None, :]
    r, k, v = jnp.einsum('nbtd,nde->nbte', xm[:3], w_rkv)
    w = -jax.nn.softplus(-(w0 + jnp.tanh(xm[3] @ w1) @ w2)) - 0.5
    a = jax.nn.sigmoid(a0 + (xm[4] @ a1) @ a2)
    g = jax.nn.sigmoid(xm[5] @ g1) @ g2
    heads = lambda t: t.reshape(Bsz, T, H, N)
    kk = heads(k * k_k).astype(f32)
    kk = kk / jnp.maximum(jnp.linalg.norm(kk, axis=-1, keepdims=True), 1e-12)
    k = k * (1 + (a - 1) * k_a)
    decay = jnp.exp(-jnp.exp(heads(w).astype(f32)))
    r_h, k_h, v_h, a_h = heads(r), heads(k), heads(v), heads(a)
    seq = tuple(jnp.moveaxis(t.astype(f32), 1, 0) for t in (r_h, decay, k_h, v_h, -kk, kk * a_h))
    S0 = jnp.zeros((Bsz, H, N, N), f32)
    _, y = lax.scan(rwkv7_step, S0, seq)
    y = jnp.moveaxis(y, 0, 1)
    y = layer_norm(y, gn_g.reshape(H, N), gn_b.reshape(H, N), GN_EPS)
    bonus = jnp.sum(r_h.astype(f32) * k_h.astype(f32) * r_k, axis=-1, keepdims=True) * v_h.astype(f32)
    y = (y + bonus).astype(x.dtype).reshape(Bsz, T, D) * g
    return y @ w_o


def grouped_moe(h, w_router, b_router, w_gu, w_down):
    Bsz, T, D = h.shape
    t = h.reshape(-1, D)
    logits = (t @ w_router).astype(jnp.float32) + b_router
    probs = jax.nn.softmax(logits, axis=-1)
    grouped = probs.reshape(-1, N_GROUPS, EXPERTS_PER_GROUP)
    group_score = jnp.sum(lax.top_k(grouped, TOP_K)[0], axis=-1)
    sel_group = jnp.argmax(group_score, axis=-1)
    in_group = (jnp.arange(N_EXPERTS) // EXPERTS_PER_GROUP)[None, :] == sel_group[:, None]
    top_p, top_i = lax.top_k(jnp.where(in_group, probs, -1.0), TOP_K)
    top_p = top_p / jnp.sum(top_p, axis=-1, keepdims=True)
    gates = jnp.sum(jax.nn.one_hot(top_i, N_EXPERTS, dtype=jnp.float32) * top_p[..., None], axis=1).astype(h.dtype)
    y = jnp.zeros_like(t)
    for e in range(N_EXPERTS):
        g_e, u_e = jnp.split(t @ w_gu[e], 2, axis=-1)
        y = y + (gates[:, e:e + 1] * jax.nn.silu(g_e) * u_e) @ w_down[e]
    return y.reshape(Bsz, T, D)


def setup_inputs(seed: int = 0) -> dict:
    key = jax.random.key(seed)
    ks = iter(jax.random.split(key, 64))
    D = D_MODEL
    nrm = lambda shape, scale: jax.random.normal(next(ks), shape, jnp.float32) * scale
    uni = lambda shape, lo, hi: jax.random.uniform(next(ks), shape, jnp.float32, lo, hi)
    return {
        'x': nrm((BATCH, SEQ, D), 1.0),
        'ev_w_in': nrm((N_EVEN, D, EVEN_IN), D ** -0.5),
        'ev_b_a': nrm((N_EVEN, 2 * A_WIDTH), 0.02),
        'ev_w_s': nrm((N_EVEN, A_GROUPS, CHUNK, CHUNK), CHUNK ** -0.5),
        'ev_b_s': 1.0 + nrm((N_EVEN, A_GROUPS, CHUNK), 0.1),
        'ev_g_v': 1.0 + nrm((N_EVEN, A_WIDTH), 0.05),
        'ev_b_v': nrm((N_EVEN, A_WIDTH), 0.02),
        'ev_b_f': 4.0 + nrm((N_EVEN, B_HEADS), 0.5),
        'ev_w_out': nrm((N_EVEN, EVEN_OUT, D), BETA * EVEN_OUT ** -0.5),
        'rw_mu': uni((N_ODD, N_MIX, D), 0.0, 1.0),
        'rw_w_rkv': nrm((N_ODD, 3, D, D), D ** -0.5),
        'rw_w0': uni((N_ODD, D), -6.0, -1.0),
        'rw_w1': nrm((N_ODD, D, LORA_W), D ** -0.5),
        'rw_w2': nrm((N_ODD, LORA_W, D), 0.1 * LORA_W ** -0.5),
        'rw_a0': nrm((N_ODD, D), 0.1),
        'rw_a1': nrm((N_ODD, D, LORA_A), D ** -0.5),
        'rw_a2': nrm((N_ODD, LORA_A, D), 0.1 * LORA_A ** -0.5),
        'rw_g1': nrm((N_ODD, D, LORA_G), D ** -0.5),
        'rw_g2': nrm((N_ODD, LORA_G, D), LORA_G ** -0.5),
        'rw_k_k': 0.85 + nrm((N_ODD, D), 0.02),
        'rw_k_a': 1.0 + nrm((N_ODD, D), 0.02),
        'rw_r_k': nrm((N_ODD, C_HEADS, C_HEAD_DIM), 0.1),
        'rw_gn_g': 1.0 + nrm((N_ODD, D), 0.05),
        'rw_gn_b': nrm((N_ODD, D), 0.02),
        'rw_w_o': nrm((N_ODD, D, D), BETA * D ** -0.5),
        'ln_g': 1.0 + nrm((DEPTH, 2, D), 0.05),
        'ln_b': nrm((DEPTH, 2, D), 0.02),
        'w_router': nrm((D, N_EXPERTS), D ** -0.5),
        'b_router': nrm((N_EXPERTS,), 0.01),
        'w_gu': nrm((DEPTH, N_EXPERTS, D, 2 * D_EXPERT), D ** -0.5),
        'w_down': nrm((DEPTH, N_EXPERTS, D_EXPERT, D), BETA * D_EXPERT ** -0.5),
    }


def reference(x, ev_w_in, ev_b_a, ev_w_s, ev_b_s, ev_g_v, ev_b_v, ev_b_f, ev_w_out,
              rw_mu, rw_w_rkv, rw_w0, rw_w1, rw_w2, rw_a0, rw_a1, rw_a2, rw_g1, rw_g2,
              rw_k_k, rw_k_a, rw_r_k, rw_gn_g, rw_gn_b, rw_w_o,
              ln_g, ln_b, w_router, b_router, w_gu, w_down):
    for layer in range(DEPTH):
        i = layer // 2
        if layer % 2 == 0:
            mixed = even_mixer(x, ev_w_in[i], ev_b_a[i], ev_w_s[i], ev_b_s[i], ev_g_v[i], ev_b_v[i],
                               ev_b_f[i], ev_w_out[i])
        else:
            mixed = rwkv7_time_mix(x, rw_mu[i], rw_w_rkv[i], rw_w0[i], rw_w1[i], rw_w2[i], rw_a0[i],
                                   rw_a1[i], rw_a2[i], rw_g1[i], rw_g2[i], rw_k_k[i], rw_k_a[i],
                                   rw_r_k[i], rw_gn_g[i], rw_gn_b[i], rw_w_o[i])
        x = layer_norm(ALPHA * x + mixed, ln_g[layer, 0], ln_b[layer, 0])
        x = layer_norm(ALPHA * x + grouped_moe(x, w_router, b_router, w_gu[layer], w_down[layer]),
                       ln_g[layer, 1], ln_b[layer, 1])
    return x
```

```python
import functools

import jax
import jax.numpy as jnp
from jax import lax
from jax.experimental import pallas as pl
from jax.experimental.pallas import tpu as pltpu

F32 = jnp.float32
BF16 = jnp.bfloat16
HI = lax.Precision.HIGHEST

LN_EPS = 1e-5
N_GROUPS = 4
LANES = 128
VMEM_LIMIT = 56 * 1024 * 1024


def _cparams(sem):
    return pltpu.CompilerParams(dimension_semantics=sem, vmem_limit_bytes=VMEM_LIMIT)


def _layer_norm(x, g, b, eps):
    mu = jnp.mean(x, -1, keepdims=True)
    xc = x - mu
    var = jnp.mean(xc * xc, -1, keepdims=True)
    return xc * lax.rsqrt(var + eps) * g + b


def _nt(a, b, **kw):
    return lax.dot_general(a, b, (((1,), (1,)), ((), ())), preferred_element_type=F32, **kw)


def _tn(a, b, **kw):
    return lax.dot_general(a, b, (((0,), (0,)), ((), ())), preferred_element_type=F32, **kw)


def _mm_kernel(a_ref, w_ref, o_ref):
    o_ref[...] = jnp.dot(a_ref[...].astype(BF16), w_ref[...], preferred_element_type=F32).astype(o_ref.dtype)


def _matmul(a, w, n_cols, out_dtype, tm=512, tn=512):
    m, k = a.shape
    tn = next(c for c in (tn, 256, LANES) if n_cols % c == 0)
    return pl.pallas_call(
        _mm_kernel,
        grid=(m // tm, n_cols // tn),
        in_specs=[pl.BlockSpec((tm, k), lambda i, j: (i, 0)), pl.BlockSpec((k, tn), lambda i, j: (0, j))],
        out_specs=pl.BlockSpec((tm, tn), lambda i, j: (i, j)),
        out_shape=jax.ShapeDtypeStruct((m, n_cols), out_dtype),
        compiler_params=_cparams(("parallel", "parallel")),
        name="matmul",
    )(a, w)


def _proj_ln_kernel(*refs, n_in, alpha):
    a_refs, w_refs = refs[:n_in], refs[n_in:2 * n_in]
    res_ref, g_ref, b_ref, o_ref = refs[2 * n_in:]
    acc = alpha * res_ref[...]
    for a_ref, w_ref in zip(a_refs, w_refs):
        acc = acc + jnp.dot(a_ref[...], w_ref[...], preferred_element_type=F32)
    o_ref[...] = _layer_norm(acc, g_ref[...], b_ref[...], LN_EPS)


def _proj_ln(a_list, w, res, g, b, alpha, tm=256):
    m, d = res.shape
    n_in = len(a_list)
    kc = a_list[0].shape[1]
    in_specs = [pl.BlockSpec((tm, kc), lambda i: (i, 0)) for _ in a_list]
    in_specs += [pl.BlockSpec((kc, d), functools.partial(lambda i, r: (r, 0), r=r)) for r in range(n_in)]
    in_specs += [pl.BlockSpec((tm, d), lambda i: (i, 0)), pl.BlockSpec((1, d), lambda i: (0, 0)),
                 pl.BlockSpec((1, d), lambda i: (0, 0))]
    return pl.pallas_call(
        functools.partial(_proj_ln_kernel, n_in=n_in, alpha=alpha),
        grid=(m // tm,),
        in_specs=in_specs,
        out_specs=pl.BlockSpec((tm, d), lambda i: (i, 0)),
        out_shape=jax.ShapeDtypeStruct((m, d), F32),
        compiler_params=_cparams(("parallel",)),
        name="proj_ln",
    )(*a_list, *([w] * n_in), res, g.reshape(1, d), b.reshape(1, d))


def _fgate_kernel(x_ref, wf_ref, bf_ref, c_ref, carry):
    @pl.when(pl.program_id(1) == 0)
    def _():
        carry[...] = jnp.zeros_like(carry)

    tm = x_ref.shape[0]
    z = _nt(wf_ref[...], x_ref[...], precision=HI) + bf_ref[...]
    log_f = jnp.minimum(z, 0.0) - jnp.log1p(jnp.exp(-jnp.abs(z)))
    row = lax.broadcasted_iota(jnp.int32, (tm, tm), 0)
    col = lax.broadcasted_iota(jnp.int32, (tm, tm), 1)
    upper = jnp.where(row <= col, 1.0, 0.0).astype(F32)
    c = jnp.dot(log_f, upper, preferred_element_type=F32, precision=HI) + carry[...]
    c_ref[0] = c
    carry[...] = carry[...] + jnp.sum(log_f, axis=-1, keepdims=True)


def _fgate(x2d, wf_t, b_f, batch, tm=512):
    n, d = x2d.shape
    h = wf_t.shape[0]
    t = n // batch
    nt = t // tm
    return pl.pallas_call(
        _fgate_kernel,
        grid=(batch, nt),
        in_specs=[pl.BlockSpec((tm, d), lambda b, i: (b * nt + i, 0)), pl.BlockSpec((h, d), lambda b, i: (0, 0)),
                  pl.BlockSpec((h, 1), lambda b, i: (0, 0))],
        out_specs=pl.BlockSpec((1, h, tm), lambda b, i: (b, 0, i)),
        out_shape=jax.ShapeDtypeStruct((batch, h, t), F32),
        scratch_shapes=[pltpu.VMEM((h, 1), F32)],
        compiler_params=_cparams(("parallel", "arbitrary")),
        name="fgate",
    )(x2d, wf_t, b_f.reshape(h, 1))


def _gelu_tanh(x):
    return 0.5 * x * (1.0 + jnp.tanh(0.7978845608028654 * (x + 0.044715 * (x * x * x))))


def _sgu_kernel(z_ref, ba_ref, ws_ref, bs_ref, gv_ref, bv_ref, o_ref, *, chunk, groups):
    aw = o_ref.shape[1]
    gd = aw // groups
    z = _gelu_tanh(z_ref[...].astype(F32) + ba_ref[...])
    u = z[:, :aw]
    v = _layer_norm(z[:, aw:], gv_ref[...], bv_ref[...], LN_EPS).astype(BF16)
    row = lax.broadcasted_iota(jnp.int32, (chunk, chunk), 0)
    col = lax.broadcasted_iota(jnp.int32, (chunk, chunk), 1)
    causal = row >= col
    bs = bs_ref[...]
    for g in range(groups):
        w_g = jnp.where(causal, ws_ref[g], 0.0).astype(BF16)
        for c in range(z.shape[0] // chunk):
            rs = slice(c * chunk, (c + 1) * chunk)
            cs = slice(g * gd, (g + 1) * gd)
            s = jnp.dot(w_g, v[rs, cs], preferred_element_type=F32) + bs[:, g:g + 1]
            o_ref[rs, cs] = (u[rs, cs] * s).astype(o_ref.dtype)


def _sgu(proj, b_a, w_s, b_s, g_v, b_v, tm=512):
    n = proj.shape[0]
    groups, chunk, _ = w_s.shape
    aw = g_v.shape[0]
    return pl.pallas_call(
        functools.partial(_sgu_kernel, chunk=chunk, groups=groups),
        grid=(n // tm,),
        in_specs=[pl.BlockSpec((tm, 2 * aw), lambda i: (i, 0)), pl.BlockSpec((1, 2 * aw), lambda i: (0, 0)),
                  pl.BlockSpec((groups, chunk, chunk), lambda i: (0, 0, 0)),
                  pl.BlockSpec((chunk, groups), lambda i: (0, 0)), pl.BlockSpec((1, aw), lambda i: (0, 0)),
                  pl.BlockSpec((1, aw), lambda i: (0, 0))],
        out_specs=pl.BlockSpec((tm, aw), lambda i: (i, 0)),
        out_shape=jax.ShapeDtypeStruct((n, aw), BF16),
        compiler_params=_cparams(("parallel",)),
        name="sgu",
    )(proj, b_a.reshape(1, -1), w_s, b_s.T, g_v.reshape(1, aw), b_v.reshape(1, aw))


def _fox_kernel(q_ref, k_ref, v_ref, cq_ref, ck_ref, o_ref, m_scr, l_scr, acc_scr, *, scale, blk):
    qi = pl.program_id(2)
    m_scr[...] = jnp.full_like(m_scr, -jnp.inf)
    l_scr[...] = jnp.zeros_like(l_scr)
    acc_scr[...] = jnp.zeros_like(acc_scr)
    q = q_ref[...]
    cq = cq_ref[0, 0]

    def step(ki, masked):
        ks = pl.multiple_of(ki * blk, blk)
        s = _nt(q, k_ref[pl.ds(ks, blk), :]) * scale
        s = s + (cq - ck_ref[0, 0, pl.ds(ki, 1), :])
        if masked:
            row = lax.broadcasted_iota(jnp.int32, (blk, blk), 0)
            col = lax.broadcasted_iota(jnp.int32, (blk, blk), 1)
            s = jnp.where(row >= col, s, -jnp.inf)
        m_prev = m_scr[...]
        m_new = jnp.maximum(m_prev, jnp.max(s, axis=-1, keepdims=True))
        corr = jnp.exp(m_prev - m_new)
        p = jnp.exp(s - m_new)
        l_scr[...] = corr * l_scr[...] + jnp.sum(p, axis=-1, keepdims=True)
        acc_scr[...] = corr * acc_scr[...] + jnp.dot(p.astype(BF16), v_ref[pl.ds(ks, blk), :],
                                                     preferred_element_type=F32)
        m_scr[...] = m_new

    def body(ki, carry):
        step(ki, False)
        return carry

    lax.fori_loop(0, qi, body, 0)
    step(qi, True)
    o_ref[...] = (acc_scr[...] / l_scr[...]).astype(o_ref.dtype)


def _fox_attention(proj, c, batch, heads, q_col, blk=512):
    n = proj.shape[0]
    t = n // batch
    nb = t // blk
    dh = LANES
    q0, k0, v0 = q_col // dh, q_col // dh + heads, q_col // dh + 2 * heads
    c_col = c.reshape(batch, heads, t, 1)
    c_row = c.reshape(batch, heads, nb, blk)
    return pl.pallas_call(
        functools.partial(_fox_kernel, scale=dh ** -0.5, blk=blk),
        grid=(batch, heads, nb),
        in_specs=[pl.BlockSpec((blk, dh), lambda b, h, i: (b * nb + i, q0 + h)),
                  pl.BlockSpec((t, dh), lambda b, h, i: (b, k0 + h)),
                  pl.BlockSpec((t, dh), lambda b, h, i: (b, v0 + h)),
                  pl.BlockSpec((1, 1, blk, 1), lambda b, h, i: (b, h, i, 0)),
                  pl.BlockSpec((1, 1, nb, blk), lambda b, h, i: (b, h, 0, 0))],
        out_specs=pl.BlockSpec((blk, dh), lambda b, h, i: (b * nb + i, h)),
        out_shape=jax.ShapeDtypeStruct((n, heads * dh), BF16),
        scratch_shapes=[pltpu.VMEM((blk, 1), F32), pltpu.VMEM((blk, 1), F32), pltpu.VMEM((blk, dh), F32)],
        compiler_params=_cparams(("parallel", "parallel", "arbitrary")),
        name="fox_attention",
    )(proj, proj, proj, c_col, c_row)


def _first_max(p, lane, valid):
    pm = jnp.where(valid, p, -2.0)
    m = jnp.max(pm, axis=-1, keepdims=True)
    idx = jnp.min(jnp.where(pm == m, lane, float(LANES)), axis=-1, keepdims=True)
    return m, idx


def _router_gates(x, wr_ref, br_ref, n_experts):
    rows = x.shape[0]
    per = n_experts // N_GROUPS
    lane = lax.broadcasted_iota(jnp.int32, (rows, LANES), 1).astype(F32)
    real = lane < n_experts
    logits = jnp.dot(x, wr_ref[...], preferred_element_type=F32, precision=HI) + br_ref[...]
    logits = jnp.where(real, logits, -jnp.inf)
    e = jnp.exp(logits - jnp.max(logits, axis=-1, keepdims=True))
    probs = e / jnp.sum(e, axis=-1, keepdims=True)
    best_score = jnp.full((rows, 1), -1.0, F32)
    best_group = jnp.zeros((rows, 1), F32)
    for grp in range(N_GROUPS):
        in_g = (lane >= grp * per) & (lane < (grp + 1) * per)
        m1, i1 = _first_max(probs, lane, in_g)
        m2, _ = _first_max(probs, lane, in_g & (lane != i1))
        score = m1 + m2
        take = score > best_score
        best_score = jnp.where(take, score, best_score)
        best_group = jnp.where(take, float(grp), best_group)
    in_sel = (lane >= best_group * per) & (lane < (best_group + 1) * per)
    p1, i1 = _first_max(probs, lane, in_sel)
    p2, i2 = _first_max(probs, lane, in_sel & (lane != i1))
    tot = p1 + p2
    return jnp.where(lane == i1, p1 / tot, 0.0) + jnp.where(lane == i2, p2 / tot, 0.0)


def _moe_dense_kernel(x_ref, wr_ref, br_ref, wgu_ref, wd_ref, g_ref, b_ref, o_ref, acc, gates, xb, *, alpha,
                      n_experts):
    e = pl.program_id(1)

    @pl.when(e == 0)
    def _():
        x = x_ref[...]
        xb[...] = x.astype(BF16)
        gates[...] = _router_gates(x, wr_ref, br_ref, n_experts)
        acc[...] = jnp.zeros_like(acc)

    de = wd_ref.shape[1]
    gu = jnp.dot(xb[...], wgu_ref[0], preferred_element_type=F32)
    gpart, upart = gu[:, :de], gu[:, de:]
    lane = lax.broadcasted_iota(jnp.int32, gates.shape, 1)
    gate_e = jnp.sum(jnp.where(lane == e, gates[...], 0.0), axis=-1, keepdims=True)
    h = gate_e * (gpart / (1.0 + jnp.exp(-gpart))) * upart
    acc[...] += jnp.dot(h.astype(BF16), wd_ref[0], preferred_element_type=F32)

    @pl.when(e == n_experts - 1)
    def _():
        o_ref[...] = _layer_norm(alpha * x_ref[...] + acc[...], g_ref[...], b_ref[...], LN_EPS)


def _moe_ln(x, w_router, b_router, w_gu, w_down, g, b, alpha, tm=256):
    n, d = x.shape
    n_experts, de = w_down.shape[0], w_down.shape[1]
    wr = jnp.zeros((d, LANES), F32).at[:, :n_experts].set(w_router)
    br = jnp.zeros((1, LANES), F32).at[0, :n_experts].set(b_router)
    return pl.pallas_call(
        functools.partial(_moe_dense_kernel, alpha=alpha, n_experts=n_experts),
        grid=(n // tm, n_experts),
        in_specs=[pl.BlockSpec((tm, d), lambda i, e: (i, 0)), pl.BlockSpec((d, LANES), lambda i, e: (0, 0)),
                  pl.BlockSpec((1, LANES), lambda i, e: (0, 0)),
                  pl.BlockSpec((1, d, 2 * de), lambda i, e: (e, 0, 0)),
                  pl.BlockSpec((1, de, d), lambda i, e: (e, 0, 0)), pl.BlockSpec((1, d), lambda i, e: (0, 0)),
                  pl.BlockSpec((1, d), lambda i, e: (0, 0))],
        out_specs=pl.BlockSpec((tm, d), lambda i, e: (i, 0)),
        out_shape=jax.ShapeDtypeStruct((n, d), F32),
        scratch_shapes=[pltpu.VMEM((tm, d), F32), pltpu.VMEM((tm, LANES), F32), pltpu.VMEM((tm, d), BF16)],
        compiler_params=_cparams(("parallel", "arbitrary")),
        name="moe_dense",
    )(x, wr, br, w_gu, w_down, g.reshape(1, d), b.reshape(1, d))


def _token_shift(x_ref, prev_ref, first):
    x = x_ref[...]
    prev_row = jnp.where(first, 0.0, prev_ref[7:8, :])
    row = lax.broadcasted_iota(jnp.int32, x.shape, 0)
    return x, jnp.where(row == 0, prev_row, pltpu.roll(x, 1, 0))


def _rkv_kernel(x_ref, prev_ref, mu_ref, w_ref, o_ref, *, tiles_per_seq):
    first = pl.program_id(0) % tiles_per_seq == 0
    x, xp = _token_shift(x_ref, prev_ref, first)
    xm = x + (xp - x) * mu_ref[0]
    o_ref[0] = jnp.dot(xm.astype(BF16), w_ref[0], preferred_element_type=F32)


def _rkv_proj(x, mu3, w_rkv, t, tm=512):
    n, d = x.shape
    sub = tm // 8
    return pl.pallas_call(
        functools.partial(_rkv_kernel, tiles_per_seq=t // tm),
        grid=(n // tm, 3),
        in_specs=[pl.BlockSpec((tm, d), lambda i, j: (i, 0)),
                  pl.BlockSpec((8, d), lambda i, j: (jnp.maximum(i * sub - 1, 0), 0)),
                  pl.BlockSpec((1, 1, d), lambda i, j: (j, 0, 0)), pl.BlockSpec((1, d, d), lambda i, j: (j, 0, 0))],
        out_specs=pl.BlockSpec((1, tm, d), lambda i, j: (j, i, 0)),
        out_shape=jax.ShapeDtypeStruct((3, n, d), F32),
        compiler_params=_cparams(("parallel", "arbitrary")),
        name="rkv_proj",
    )(x, x, mu3.reshape(3, 1, d), w_rkv)


def _lora_kernel(x_ref, prev_ref, mu_ref, w0_ref, w1_ref, w2_ref, a0_ref, a1_ref, a2_ref, g1_ref, g2_ref, wl_ref,
                 a_ref, g_ref, *, tiles_per_seq):
    first = pl.program_id(0) % tiles_per_seq == 0
    x, xp = _token_shift(x_ref, prev_ref, first)
    dx = xp - x

    def mm(a, w_ref):
        return jnp.dot(a.astype(BF16), w_ref[...], preferred_element_type=F32)

    zw = w0_ref[...] + mm(jnp.tanh(mm(x + dx * mu_ref[0:1, :], w1_ref)), w2_ref)
    wl_ref[...] = jnp.minimum(zw, 0.0) - jnp.log1p(jnp.exp(-jnp.abs(zw))) - 0.5
    za = a0_ref[...] + mm(mm(x + dx * mu_ref[1:2, :], a1_ref), a2_ref)
    a_ref[...] = 1.0 / (1.0 + jnp.exp(-za))
    zg = mm(x + dx * mu_ref[2:3, :], g1_ref)
    g_ref[...] = mm(1.0 / (1.0 + jnp.exp(-zg)), g2_ref)


def _lora(x, mu3, w0, w1, w2, a0, a1, a2, g1, g2, t, tm=512):
    n, d = x.shape
    sub = tm // 8
    const = lambda shape: pl.BlockSpec(shape, lambda i: (0, 0))
    row = pl.BlockSpec((tm, d), lambda i: (i, 0))
    return pl.pallas_call(
        functools.partial(_lora_kernel, tiles_per_seq=t // tm),
        grid=(n // tm,),
        in_specs=[row, pl.BlockSpec((8, d), lambda i: (jnp.maximum(i * sub - 1, 0), 0)), const((3, d)),
                  const((1, d)), const(w1.shape), const(w2.shape), const((1, d)), const(a1.shape), const(a2.shape),
                  const(g1.shape), const(g2.shape)],
        out_specs=[row, row, row],
        out_shape=[jax.ShapeDtypeStruct((n, d), F32)] * 3,
        compiler_params=_cparams(("parallel",)),
        name="lora",
    )(x, x, mu3, w0.reshape(1, d), w1, w2, a0.reshape(1, d), a1, a2, g1, g2)


def _inv_unit_lower(m, size):
    row = lax.broadcasted_iota(jnp.int32, m.shape, 0)
    col = lax.broadcasted_iota(jnp.int32, m.shape, 1)
    p = jnp.where(row == col, 1.0, 0.0).astype(F32) + m
    n = 1
    while 2 * n < size:
        m = jnp.dot(m, m, preferred_element_type=F32, precision=HI)
        p = p + jnp.dot(p, m, preferred_element_type=F32, precision=HI)
        n *= 2
    return p


def _scan_kernel(r_ref, k_ref, v_ref, wl_ref, a_ref, g_ref, kk_ref, ka_ref, rk_ref, gg_ref, gb_ref, o_ref, state, *,
                 hd, gn_eps):
    @pl.when(pl.program_id(2) == 0)
    def _():
        state[...] = jnp.zeros_like(state)

    c = r_ref.shape[0]
    nh = LANES // hd
    hc = nh * c
    dot = functools.partial(jnp.dot, preferred_element_type=F32, precision=HI)
    lane = lax.broadcasted_iota(jnp.int32, (1, LANES), 1)
    head_masks = [(lane >= h * hd) & (lane < (h + 1) * hd) for h in range(nh)]
    lr = lax.broadcasted_iota(jnp.int32, (LANES, LANES), 0) // hd
    lc = lax.broadcasted_iota(jnp.int32, (LANES, LANES), 1) // hd
    same_head = jnp.where(lr == lc, 1.0, 0.0).astype(F32)

    def per_head(x):
        return jnp.concatenate([jnp.where(hm, x, 0.0) for hm in head_masks], axis=0)

    r, k, v, a_gate = r_ref[...], k_ref[...], v_ref[...], a_ref[...]
    kk = k * kk_ref[...]
    kk = kk / jnp.maximum(jnp.sqrt(dot(kk * kk, same_head)), 1e-12)
    k2 = k * (1.0 + (a_gate - 1.0) * ka_ref[...])
    lw = -jnp.exp(wl_ref[...])
    trow = lax.broadcasted_iota(jnp.int32, (c, c), 0)
    tcol = lax.broadcasted_iota(jnp.int32, (c, c), 1)
    cum = dot(jnp.where(trow >= tcol, 1.0, 0.0).astype(F32), lw)
    gam = jnp.exp(cum)
    inv_gam = jnp.exp(-cum)
    gam_end = gam[c - 1:c, :]
    a_t = -kk * jnp.exp(cum - lw)
    b_t = kk * a_gate * inv_gam
    k_t = k2 * inv_gam
    r_t = r * gam

    s0 = state[...]
    a_s0 = _nt(a_t, s0, precision=HI)
    r_s0 = _nt(r_t, s0, precision=HI)
    a2, r2, b2, k2h, v2 = per_head(a_t), per_head(r_t), per_head(b_t), per_head(k_t), per_head(v)
    prow = lax.broadcasted_iota(jnp.int32, (hc, hc), 0)
    pcol = lax.broadcasted_iota(jnp.int32, (hc, hc), 1)
    same_blk = (prow // c) == (pcol // c)
    strict = same_blk & (prow > pcol)
    incl = same_blk & (prow >= pcol)
    a_ab = jnp.where(strict, _nt(a2, b2, precision=HI), 0.0)
    a_ak = jnp.where(strict, _nt(a2, k2h, precision=HI), 0.0)
    a_rb = jnp.where(incl, _nt(r2, b2, precision=HI), 0.0)
    a_rk = jnp.where(incl, _nt(r2, k2h, precision=HI), 0.0)
    u2 = dot(_inv_unit_lower(a_ab, c), per_head(a_s0) + dot(a_ak, v2))
    y2 = dot(a_rb, u2) + dot(a_rk, v2)
    y = r_s0
    for h in range(nh):
        y = y + y2[h * c:(h + 1) * c, :]
    uv = jnp.concatenate([u2, v2], axis=0)
    bkg = jnp.concatenate([b2 * gam_end, k2h * gam_end], axis=0)
    state[...] = s0 * gam_end + _tn(uv, bkg, precision=HI)

    inv_hd = 1.0 / hd
    mean = dot(y, same_head) * inv_hd
    yc = y - mean
    var = dot(yc * yc, same_head) * inv_hd
    yn = yc * lax.rsqrt(var + gn_eps) * gg_ref[...] + gb_ref[...]
    bonus = dot(r * k2 * rk_ref[...], same_head) * v
    o_ref[...] = ((yn + bonus) * g_ref[...]).astype(o_ref.dtype)


def _rwkv_scan(rkv, wl, a, g, k_k, k_a, r_k, gn_g, gn_b, batch, hd, gn_eps, chunk=64):
    _, n, d = rkv.shape
    t = n // batch
    nc = t // chunk
    tok = lambda j: pl.BlockSpec((1, chunk, LANES), functools.partial(lambda b, p, c, j: (j, b * nc + c, p), j=j))
    tok2 = pl.BlockSpec((chunk, LANES), lambda b, p, c: (b * nc + c, p))
    par = pl.BlockSpec((1, LANES), lambda b, p, c: (0, p))

    def kernel(r_ref, k_ref, v_ref, *rest):
        _scan_kernel(r_ref.at[0], k_ref.at[0], v_ref.at[0], *rest, hd=hd, gn_eps=gn_eps)

    return pl.pallas_call(
        kernel,
        grid=(batch, d // LANES, nc),
        in_specs=[tok(0), tok(1), tok(2), tok2, tok2, tok2, par, par, par, par, par],
        out_specs=tok2,
        out_shape=jax.ShapeDtypeStruct((n, d), BF16),
        scratch_shapes=[pltpu.VMEM((LANES, LANES), F32)],
        compiler_params=_cparams(("parallel", "parallel", "arbitrary")),
        name="rwkv_scan",
    )(rkv, rkv, rkv, wl, a, g, k_k.reshape(1, d), k_a.reshape(1, d), r_k.reshape(1, d), gn_g.reshape(1, d),
      gn_b.reshape(1, d))


def _pad_lora(w_in, w_out):
    r = w_in.shape[1]
    rp = -(-r // LANES) * LANES
    return (jnp.pad(w_in, ((0, 0), (0, rp - r))).astype(BF16), jnp.pad(w_out, ((0, rp - r), (0, 0))).astype(BF16))


def kernel(x, ev_w_in, ev_b_a, ev_w_s, ev_b_s, ev_g_v, ev_b_v, ev_b_f, ev_w_out, rw_mu, rw_w_rkv, rw_w0, rw_w1, rw_w2, rw_a0, rw_a1, rw_a2, rw_g1, rw_g2, rw_k_k, rw_k_a, rw_r_k, rw_gn_g, rw_gn_b, rw_w_o, ln_g, ln_b, w_router, b_router, w_gu, w_down):
    batch, t, d = x.shape
    depth = ln_g.shape[0]
    alpha = (2 * depth) ** 0.25
    h = x.reshape(batch * t, d)
    for layer in range(depth):
        i = layer // 2
        if layer % 2 == 0:
            aw = ev_g_v.shape[1]
            heads = ev_b_f.shape[1]
            q_col = 2 * aw
            n_main = q_col + 3 * heads * LANES
            w_in = ev_w_in[i]
            proj = _matmul(h, w_in.astype(BF16), n_main, BF16)
            c = _fgate(h, w_in[:, n_main:].T, ev_b_f[i], batch)
            y_a = _sgu(proj, ev_b_a[i], ev_w_s[i], ev_b_s[i], ev_g_v[i], ev_b_v[i])
            y_b = _fox_attention(proj, c, batch, heads, q_col)
            h = _proj_ln([y_a, y_b], ev_w_out[i].astype(BF16), h, ln_g[layer, 0], ln_b[layer, 0], alpha)
        else:
            hd = rw_r_k.shape[2]
            mu = rw_mu[i]
            rkv = _rkv_proj(h, mu[:3], rw_w_rkv[i].astype(BF16), t)
            w1, w2 = _pad_lora(rw_w1[i], rw_w2[i])
            a1, a2 = _pad_lora(rw_a1[i], rw_a2[i])
            g1, g2 = _pad_lora(rw_g1[i], rw_g2[i])
            wl, a, g = _lora(h, mu[3:], rw_w0[i], w1, w2, rw_a0[i], a1, a2, g1, g2, t)
            y = _rwkv_scan(rkv, wl, a, g, rw_k_k[i], rw_k_a[i], rw_r_k[i].reshape(-1), rw_gn_g[i], rw_gn_b[i],
                           batch, hd, hd * 1e-5)
            h = _proj_ln([y], rw_w_o[i].astype(BF16), h, ln_g[layer, 0], ln_b[layer, 0], alpha)
        h = _moe_ln(h, w_router, b_router, w_gu[layer].astype(BF16), w_down[layer].astype(BF16), ln_g[layer, 1],
                    ln_b[layer, 1], alpha)
    return h.reshape(batch, t, d)
```

```python
import functools

import jax
import jax.numpy as jnp
from jax import lax
from jax.experimental import pallas as pl
from jax.experimental.pallas import tpu as pltpu

F32 = jnp.float32
BF16 = jnp.bfloat16
HI = lax.Precision.HIGHEST

LN_EPS = 1e-5
N_GROUPS = 4
LANES = 128
VMEM_LIMIT = 56 * 1024 * 1024


def _cparams(sem):
    return pltpu.CompilerParams(dimension_semantics=sem, vmem_limit_bytes=VMEM_LIMIT)


def _layer_norm(x, g, b, eps):
    mu = jnp.mean(x, -1, keepdims=True)
    xc = x - mu
    var = jnp.mean(xc * xc, -1, keepdims=True)
    return xc * lax.rsqrt(var + eps) * g + b


def _nt(a, b, **kw):
    return lax.dot_general(a, b, (((1,), (1,)), ((), ())), preferred_element_type=F32, **kw)


def _tn(a, b, **kw):
    return lax.dot_general(a, b, (((0,), (0,)), ((), ())), preferred_element_type=F32, **kw)


def _bdot(a, b):
    return jnp.dot(a.astype(BF16), b.astype(BF16), preferred_element_type=F32)


def _mm_kernel(a_ref, w_ref, o_ref):
    o_ref[...] = jnp.dot(a_ref[...].astype(BF16), w_ref[...], preferred_element_type=F32).astype(o_ref.dtype)


def _matmul(a, w, n_cols, out_dtype, tm=512, tn=512):
    m, k = a.shape
    tn = next(c for c in (tn, 256, LANES) if n_cols % c == 0)
    return pl.pallas_call(
        _mm_kernel,
        grid=(m // tm, n_cols // tn),
        in_specs=[pl.BlockSpec((tm, k), lambda i, j: (i, 0)), pl.BlockSpec((k, tn), lambda i, j: (0, j))],
        out_specs=pl.BlockSpec((tm, tn), lambda i, j: (i, j)),
        out_shape=jax.ShapeDtypeStruct((m, n_cols), out_dtype),
        compiler_params=_cparams(("parallel", "parallel")),
        name="matmul",
    )(a, w)


def _proj_ln_kernel(*refs, n_in, alpha):
    a_refs, w_refs = refs[:n_in], refs[n_in:2 * n_in]
    res_ref, g_ref, b_ref, o_ref = refs[2 * n_in:]
    acc = alpha * res_ref[...]
    for a_ref, w_ref in zip(a_refs, w_refs):
        acc = acc + jnp.dot(a_ref[...], w_ref[...], preferred_element_type=F32)
    o_ref[...] = _layer_norm(acc, g_ref[...], b_ref[...], LN_EPS)


def _proj_ln(a_list, w, res, g, b, alpha, tm=256):
    m, d = res.shape
    n_in = len(a_list)
    kc = a_list[0].shape[1]
    in_specs = [pl.BlockSpec((tm, kc), lambda i: (i, 0)) for _ in a_list]
    in_specs += [pl.BlockSpec((kc, d), functools.partial(lambda i, r: (r, 0), r=r)) for r in range(n_in)]
    in_specs += [pl.BlockSpec((tm, d), lambda i: (i, 0)), pl.BlockSpec((1, d), lambda i: (0, 0)),
                 pl.BlockSpec((1, d), lambda i: (0, 0))]
    return pl.pallas_call(
        functools.partial(_proj_ln_kernel, n_in=n_in, alpha=alpha),
        grid=(m // tm,),
        in_specs=in_specs,
        out_specs=pl.BlockSpec((tm, d), lambda i: (i, 0)),
        out_shape=jax.ShapeDtypeStruct((m, d), F32),
        compiler_params=_cparams(("parallel",)),
        name="proj_ln",
    )(*a_list, *([w] * n_in), res, g.reshape(1, d), b.reshape(1, d))


def _fgate_kernel(x_ref, wf_ref, bf_ref, c_ref, carry):
    @pl.when(pl.program_id(1) == 0)
    def _():
        carry[...] = jnp.zeros_like(carry)

    tm = x_ref.shape[0]
    z = _nt(wf_ref[...], x_ref[...], precision=HI) + bf_ref[...]
    log_f = jnp.minimum(z, 0.0) - jnp.log1p(jnp.exp(-jnp.abs(z)))
    row = lax.broadcasted_iota(jnp.int32, (tm, tm), 0)
    col = lax.broadcasted_iota(jnp.int32, (tm, tm), 1)
    upper = jnp.where(row <= col, 1.0, 0.0).astype(F32)
    c = jnp.dot(log_f, upper, preferred_element_type=F32, precision=HI) + carry[...]
    c_ref[0] = c
    carry[...] = carry[...] + jnp.sum(log_f, axis=-1, keepdims=True)


def _fgate(x2d, wf_t, b_f, batch, tm=512):
    n, d = x2d.shape
    h = wf_t.shape[0]
    t = n // batch
    nt = t // tm
    return pl.pallas_call(
        _fgate_kernel,
        grid=(batch, nt),
        in_specs=[pl.BlockSpec((tm, d), lambda b, i: (b * nt + i, 0)), pl.BlockSpec((h, d), lambda b, i: (0, 0)),
                  pl.BlockSpec((h, 1), lambda b, i: (0, 0))],
        out_specs=pl.BlockSpec((1, h, tm), lambda b, i: (b, 0, i)),
        out_shape=jax.ShapeDtypeStruct((batch, h, t), F32),
        scratch_shapes=[pltpu.VMEM((h, 1), F32)],
        compiler_params=_cparams(("parallel", "arbitrary")),
        name="fgate",
    )(x2d, wf_t, b_f.reshape(h, 1))


def _gelu_tanh(x):
    return 0.5 * x * (1.0 + jnp.tanh(0.7978845608028654 * (x + 0.044715 * (x * x * x))))


def _sgu_kernel(z_ref, ba_ref, ws_ref, bs_ref, gv_ref, bv_ref, o_ref, *, chunk, groups):
    aw = o_ref.shape[1]
    gd = aw // groups
    z = _gelu_tanh(z_ref[...].astype(F32) + ba_ref[...])
    u = z[:, :aw]
    v = _layer_norm(z[:, aw:], gv_ref[...], bv_ref[...], LN_EPS).astype(BF16)
    row = lax.broadcasted_iota(jnp.int32, (chunk, chunk), 0)
    col = lax.broadcasted_iota(jnp.int32, (chunk, chunk), 1)
    causal = row >= col
    bs = bs_ref[...]
    for g in range(groups):
        w_g = jnp.where(causal, ws_ref[g], 0.0).astype(BF16)
        for c in range(z.shape[0] // chunk):
            rs = slice(c * chunk, (c + 1) * chunk)
            cs = slice(g * gd, (g + 1) * gd)
            s = jnp.dot(w_g, v[rs, cs], preferred_element_type=F32) + bs[:, g:g + 1]
            o_ref[rs, cs] = (u[rs, cs] * s).astype(o_ref.dtype)


def _sgu(proj, b_a, w_s, b_s, g_v, b_v, tm=512):
    n = proj.shape[0]
    groups, chunk, _ = w_s.shape
    aw = g_v.shape[0]
    return pl.pallas_call(
        functools.partial(_sgu_kernel, chunk=chunk, groups=groups),
        grid=(n // tm,),
        in_specs=[pl.BlockSpec((tm, 2 * aw), lambda i: (i, 0)), pl.BlockSpec((1, 2 * aw), lambda i: (0, 0)),
                  pl.BlockSpec((groups, chunk, chunk), lambda i: (0, 0, 0)),
                  pl.BlockSpec((chunk, groups), lambda i: (0, 0)), pl.BlockSpec((1, aw), lambda i: (0, 0)),
                  pl.BlockSpec((1, aw), lambda i: (0, 0))],
        out_specs=pl.BlockSpec((tm, aw), lambda i: (i, 0)),
        out_shape=jax.ShapeDtypeStruct((n, aw), BF16),
        compiler_params=_cparams(("parallel",)),
        name="sgu",
    )(proj, b_a.reshape(1, -1), w_s, b_s.T, g_v.reshape(1, aw), b_v.reshape(1, aw))


def _fox_kernel(q_ref, k_ref, v_ref, cq_ref, ck_ref, o_ref, m_scr, l_scr, acc_scr, *, scale, blk):
    qi = pl.program_id(2)
    m_scr[...] = jnp.full_like(m_scr, -jnp.inf)
    l_scr[...] = jnp.zeros_like(l_scr)
    acc_scr[...] = jnp.zeros_like(acc_scr)
    q = q_ref[...]
    cq = cq_ref[0, 0]

    def step(ki, masked):
        ks = pl.multiple_of(ki * blk, blk)
        s = _nt(q, k_ref[pl.ds(ks, blk), :]) * scale
        s = s + (cq - ck_ref[0, 0, pl.ds(ki, 1), :])
        if masked:
            row = lax.broadcasted_iota(jnp.int32, (blk, blk), 0)
            col = lax.broadcasted_iota(jnp.int32, (blk, blk), 1)
            s = jnp.where(row >= col, s, -jnp.inf)
        m_prev = m_scr[...]
        m_new = jnp.maximum(m_prev, jnp.max(s, axis=-1, keepdims=True))
        corr = jnp.exp(m_prev - m_new)
        p = jnp.exp(s - m_new)
        l_scr[...] = corr * l_scr[...] + jnp.sum(p, axis=-1, keepdims=True)
        acc_scr[...] = corr * acc_scr[...] + jnp.dot(p.astype(BF16), v_ref[pl.ds(ks, blk), :],
                                                     preferred_element_type=F32)
        m_scr[...] = m_new

    def body(ki, carry):
        step(ki, False)
        return carry

    lax.fori_loop(0, qi, body, 0)
    step(qi, True)
    o_ref[...] = (acc_scr[...] / l_scr[...]).astype(o_ref.dtype)


def _fox_attention(proj, c, batch, heads, q_col, blk=512):
    n = proj.shape[0]
    t = n // batch
    nb = t // blk
    dh = LANES
    q0, k0, v0 = q_col // dh, q_col // dh + heads, q_col // dh + 2 * heads
    c_col = c.reshape(batch, heads, t, 1)
    c_row = c.reshape(batch, heads, nb, blk)
    return pl.pallas_call(
        functools.partial(_fox_kernel, scale=dh ** -0.5, blk=blk),
        grid=(batch, heads, nb),
        in_specs=[pl.BlockSpec((blk, dh), lambda b, h, i: (b * nb + i, q0 + h)),
                  pl.BlockSpec((t, dh), lambda b, h, i: (b, k0 + h)),
                  pl.BlockSpec((t, dh), lambda b, h, i: (b, v0 + h)),
                  pl.BlockSpec((1, 1, blk, 1), lambda b, h, i: (b, h, i, 0)),
                  pl.BlockSpec((1, 1, nb, blk), lambda b, h, i: (b, h, 0, 0))],
        out_specs=pl.BlockSpec((blk, dh), lambda b, h, i: (b * nb + i, h)),
        out_shape=jax.ShapeDtypeStruct((n, heads * dh), BF16),
        scratch_shapes=[pltpu.VMEM((blk, 1), F32), pltpu.VMEM((blk, 1), F32), pltpu.VMEM((blk, dh), F32)],
        compiler_params=_cparams(("parallel", "parallel", "arbitrary")),
        name="fox_attention",
    )(proj, proj, proj, c_col, c_row)


def _first_max(p, lane, valid):
    pm = jnp.where(valid, p, -2.0)
    m = jnp.max(pm, axis=-1, keepdims=True)
    idx = jnp.min(jnp.where(pm == m, lane, float(LANES)), axis=-1, keepdims=True)
    return m, idx


def _router_top2(x, wr_ref, br_ref, n_experts):
    rows = x.shape[0]
    per = n_experts // N_GROUPS
    lane = lax.broadcasted_iota(jnp.int32, (rows, LANES), 1).astype(F32)
    real = lane < n_experts
    logits = jnp.dot(x, wr_ref[...], preferred_element_type=F32, precision=HI) + br_ref[...]
    logits = jnp.where(real, logits, -jnp.inf)
    e = jnp.exp(logits - jnp.max(logits, axis=-1, keepdims=True))
    probs = e / jnp.sum(e, axis=-1, keepdims=True)
    best_score = jnp.full((rows, 1), -1.0, F32)
    best_group = jnp.zeros((rows, 1), F32)
    for grp in range(N_GROUPS):
        in_g = (lane >= grp * per) & (lane < (grp + 1) * per)
        m1, i1 = _first_max(probs, lane, in_g)
        m2, _ = _first_max(probs, lane, in_g & (lane != i1))
        score = m1 + m2
        take = score > best_score
        best_score = jnp.where(take, score, best_score)
        best_group = jnp.where(take, float(grp), best_group)
    in_sel = (lane >= best_group * per) & (lane < (best_group + 1) * per)
    p1, i1 = _first_max(probs, lane, in_sel)
    p2, i2 = _first_max(probs, lane, in_sel & (lane != i1))
    tot = p1 + p2
    return lane, i1, i2, p1 / tot, p2 / tot


def _route_kernel(x_ref, wr_ref, br_ref, meta_ref, cnt_ref, carry, *, n_experts):
    @pl.when(pl.program_id(0) == 0)
    def _():
        carry[...] = jnp.zeros_like(carry)

    tm = x_ref.shape[0]
    lane, i1, i2, p1, p2 = _router_top2(x_ref[...], wr_ref, br_ref, n_experts)
    row = lax.broadcasted_iota(jnp.int32, (tm, tm), 0)
    col = lax.broadcasted_iota(jnp.int32, (tm, tm), 1)
    before = jnp.where(row > col, 1.0, 0.0).astype(BF16)
    onehot = jnp.where((lane == i1) | (lane == i2), 1.0, 0.0)
    seen = jnp.dot(before, onehot.astype(BF16), preferred_element_type=F32) + carry[...]
    r1 = jnp.sum(jnp.where(lane == i1, seen, 0.0), axis=-1, keepdims=True)
    r2 = jnp.sum(jnp.where(lane == i2, seen, 0.0), axis=-1, keepdims=True)
    meta = jnp.zeros((tm, LANES), F32)
    for j, val in enumerate((i1, i2, p1, p2, r1, r2)):
        meta = jnp.where(lane == j, val, meta)
    meta_ref[...] = meta
    carry[...] = carry[...] + jnp.sum(onehot, axis=0, keepdims=True)
    cnt_ref[...] = carry[...]


def _route(x, w_router, b_router, tm=512):
    n, d = x.shape
    n_experts = w_router.shape[1]
    wr = jnp.zeros((d, LANES), F32).at[:, :n_experts].set(w_router)
    br = jnp.zeros((1, LANES), F32).at[0, :n_experts].set(b_router)
    return pl.pallas_call(
        functools.partial(_route_kernel, n_experts=n_experts),
        grid=(n // tm,),
        in_specs=[pl.BlockSpec((tm, d), lambda i: (i, 0)), pl.BlockSpec((d, LANES), lambda i: (0, 0)),
                  pl.BlockSpec((1, LANES), lambda i: (0, 0))],
        out_specs=[pl.BlockSpec((tm, LANES), lambda i: (i, 0)), pl.BlockSpec((1, LANES), lambda i: (0, 0))],
        out_shape=[jax.ShapeDtypeStruct((n, LANES), F32), jax.ShapeDtypeStruct((1, LANES), F32)],
        scratch_shapes=[pltpu.VMEM((1, LANES), F32)],
        compiler_params=_cparams(("arbitrary",)),
        name="route",
    )(x, wr, br)


def _row_copy(src_ref, src_row, dst_ref, dst_row, sem):
    return pltpu.make_async_copy(src_ref.at[pl.ds(src_row, 1)], dst_ref.at[pl.ds(dst_row, 1)], sem)


def _dispatch_kernel(pos_ref, x_ref, _, xs_ref, sem):
    tm = x_ref.shape[0]

    def start(r, carry):
        for s in range(pos_ref.shape[0]):
            _row_copy(x_ref, r, xs_ref, pos_ref[s, r], sem).start()
        return carry

    def wait(r, carry):
        for s in range(pos_ref.shape[0]):
            _row_copy(x_ref, r, xs_ref, pos_ref[s, r], sem).wait()
        return carry

    lax.fori_loop(0, tm, start, 0)
    lax.fori_loop(0, tm, wait, 0)


def _dispatch(x, pos, n_rows, tm=256):
    n, d = x.shape
    return pl.pallas_call(
        _dispatch_kernel,
        grid=(n // tm,),
        in_specs=[pl.BlockSpec((pos.shape[0], tm), lambda i: (0, i), memory_space=pltpu.SMEM),
                  pl.BlockSpec((tm, d), lambda i: (i, 0)), pl.BlockSpec(memory_space=pl.ANY)],
        out_specs=pl.BlockSpec(memory_space=pl.ANY),
        out_shape=jax.ShapeDtypeStruct((n_rows, d), x.dtype),
        scratch_shapes=[pltpu.SemaphoreType.DMA],
        input_output_aliases={2: 0},
        compiler_params=_cparams(("arbitrary",)),
        name="moe_dispatch",
    )(pos, x, jnp.zeros((n_rows, d), x.dtype))


def _experts_kernel(te_ref, na_ref, xs_ref, wgu_ref, wd_ref, ys_ref):
    live = pl.program_id(0) < na_ref[0]

    @pl.when(live)
    def _():
        de = wd_ref.shape[1]
        gu = jnp.dot(xs_ref[...].astype(BF16), wgu_ref[0], preferred_element_type=F32)
        gpart, upart = gu[:, :de], gu[:, de:]
        h = (gpart / (1.0 + jnp.exp(-gpart))) * upart
        ys_ref[...] = jnp.dot(h.astype(BF16), wd_ref[0], preferred_element_type=F32)

    @pl.when(jnp.logical_not(live))
    def _():
        ys_ref[...] = jnp.zeros_like(ys_ref)


def _experts(xs, w_gu, w_down, tile_expert, n_active, tm):
    p, d = xs.shape
    de = w_down.shape[1]
    live = lambda j, na: jnp.minimum(j, na[0] - 1)
    grid_spec = pltpu.PrefetchScalarGridSpec(
        num_scalar_prefetch=2,
        grid=(p // tm,),
        in_specs=[pl.BlockSpec((tm, d), lambda j, te, na: (live(j, na), 0)),
                  pl.BlockSpec((1, d, 2 * de), lambda j, te, na: (te[live(j, na)], 0, 0)),
                  pl.BlockSpec((1, de, d), lambda j, te, na: (te[live(j, na)], 0, 0))],
        out_specs=pl.BlockSpec((tm, d), lambda j, te, na: (j, 0)),
    )
    return pl.pallas_call(
        _experts_kernel,
        grid_spec=grid_spec,
        out_shape=jax.ShapeDtypeStruct((p, d), F32),
        compiler_params=_cparams(("arbitrary",)),
        name="moe_experts",
    )(tile_expert, n_active, xs, w_gu, w_down)


def _combine_kernel(pos_ref, x_ref, gw_ref, ys_ref, g_ref, b_ref, o_ref, buf, sem, *, alpha):
    tm = x_ref.shape[0]
    n_slots = pos_ref.shape[0]

    def start(r, carry):
        for s in range(n_slots):
            _row_copy(ys_ref, pos_ref[s, r], buf.at[s], r, sem).start()
        return carry

    def wait(r, carry):
        for s in range(n_slots):
            _row_copy(ys_ref, pos_ref[s, r], buf.at[s], r, sem).wait()
        return carry

    lax.fori_loop(0, tm, start, 0)
    lax.fori_loop(0, tm, wait, 0)
    gw = gw_ref[...]
    y = alpha * x_ref[...]
    for s in range(n_slots):
        y = y + gw[:, s:s + 1] * buf[s]
    o_ref[...] = _layer_norm(y, g_ref[...], b_ref[...], LN_EPS)


def _combine_ln(x, ys, pos, gw, g, b, alpha, tm=256):
    n, d = x.shape
    n_slots = pos.shape[0]
    return pl.pallas_call(
        functools.partial(_combine_kernel, alpha=alpha),
        grid=(n // tm,),
        in_specs=[pl.BlockSpec((n_slots, tm), lambda i: (0, i), memory_space=pltpu.SMEM),
                  pl.BlockSpec((tm, d), lambda i: (i, 0)), pl.BlockSpec((tm, n_slots), lambda i: (i, 0)),
                  pl.BlockSpec(memory_space=pl.ANY), pl.BlockSpec((1, d), lambda i: (0, 0)),
                  pl.BlockSpec((1, d), lambda i: (0, 0))],
        out_specs=pl.BlockSpec((tm, d), lambda i: (i, 0)),
        out_shape=jax.ShapeDtypeStruct((n, d), F32),
        scratch_shapes=[pltpu.VMEM((n_slots, tm, d), F32), pltpu.SemaphoreType.DMA],
        compiler_params=_cparams(("arbitrary",)),
        name="moe_combine",
    )(pos, x, gw, ys, g.reshape(1, d), b.reshape(1, d))


def _moe_ln(x, w_router, b_router, w_gu, w_down, g, b, alpha, tm=256):
    n, d = x.shape
    n_experts = w_down.shape[0]
    meta, counts = _route(x, w_router, b_router)
    eid = meta[:, 0:2].astype(jnp.int32)
    gw = meta[:, 2:4]
    rank = meta[:, 4:6].astype(jnp.int32)
    counts = counts[0, :n_experts].astype(jnp.int32)
    tiles = (counts + tm - 1) // tm
    tile_end = jnp.cumsum(tiles)
    row_off = (tile_end - tiles) * tm
    n_tiles = (eid.size + tm - 1) // tm + n_experts
    tile_expert = jnp.sum(jnp.arange(n_tiles)[:, None] >= tile_end[None, :], axis=1).astype(jnp.int32)
    tile_expert = jnp.minimum(tile_expert, n_experts - 1)
    n_active = tile_end[-1:].astype(jnp.int32)
    sel = eid[:, :, None] == jnp.arange(n_experts)[None, None, :]
    pos = (jnp.sum(jnp.where(sel, row_off[None, None, :], 0), axis=-1) + rank).T
    xs = _dispatch(x, pos, n_tiles * tm)
    ys = _experts(xs, w_gu, w_down, tile_expert, n_active, tm)
    return _combine_ln(x, ys, pos, gw, g, b, alpha)


def _token_shift(x_ref, prev_ref, first):
    x = x_ref[...]
    prev_row = jnp.where(first, 0.0, prev_ref[7:8, :])
    row = lax.broadcasted_iota(jnp.int32, x.shape, 0)
    return x, jnp.where(row == 0, prev_row, pltpu.roll(x, 1, 0))


def _rkv_kernel(x_ref, prev_ref, mu_ref, w_ref, o_ref, *, tiles_per_seq):
    first = pl.program_id(0) % tiles_per_seq == 0
    x, xp = _token_shift(x_ref, prev_ref, first)
    xm = x + (xp - x) * mu_ref[0]
    o_ref[0] = jnp.dot(xm.astype(BF16), w_ref[0], preferred_element_type=F32)


def _rkv_proj(x, mu3, w_rkv, t, tm=512):
    n, d = x.shape
    sub = tm // 8
    return pl.pallas_call(
        functools.partial(_rkv_kernel, tiles_per_seq=t // tm),
        grid=(n // tm, 3),
        in_specs=[pl.BlockSpec((tm, d), lambda i, j: (i, 0)),
                  pl.BlockSpec((8, d), lambda i, j: (jnp.maximum(i * sub - 1, 0), 0)),
                  pl.BlockSpec((1, 1, d), lambda i, j: (j, 0, 0)), pl.BlockSpec((1, d, d), lambda i, j: (j, 0, 0))],
        out_specs=pl.BlockSpec((1, tm, d), lambda i, j: (j, i, 0)),
        out_shape=jax.ShapeDtypeStruct((3, n, d), F32),
        compiler_params=_cparams(("parallel", "arbitrary")),
        name="rkv_proj",
    )(x, x, mu3.reshape(3, 1, d), w_rkv)


def _lora_kernel(x_ref, prev_ref, mu_ref, w0_ref, w1_ref, w2_ref, a0_ref, a1_ref, a2_ref, g1_ref, g2_ref, wl_ref,
                 a_ref, g_ref, *, tiles_per_seq):
    first = pl.program_id(0) % tiles_per_seq == 0
    x, xp = _token_shift(x_ref, prev_ref, first)
    dx = xp - x

    def mm(a, w_ref):
        return jnp.dot(a.astype(BF16), w_ref[...], preferred_element_type=F32)

    zw = w0_ref[...] + mm(jnp.tanh(mm(x + dx * mu_ref[0:1, :], w1_ref)), w2_ref)
    wl_ref[...] = jnp.minimum(zw, 0.0) - jnp.log1p(jnp.exp(-jnp.abs(zw))) - 0.5
    za = a0_ref[...] + mm(mm(x + dx * mu_ref[1:2, :], a1_ref), a2_ref)
    a_ref[...] = 1.0 / (1.0 + jnp.exp(-za))
    zg = mm(x + dx * mu_ref[2:3, :], g1_ref)
    g_ref[...] = mm(1.0 / (1.0 + jnp.exp(-zg)), g2_ref)


def _lora(x, mu3, w0, w1, w2, a0, a1, a2, g1, g2, t, tm=512):
    n, d = x.shape
    sub = tm // 8
    const = lambda shape: pl.BlockSpec(shape, lambda i: (0, 0))
    row = pl.BlockSpec((tm, d), lambda i: (i, 0))
    return pl.pallas_call(
        functools.partial(_lora_kernel, tiles_per_seq=t // tm),
        grid=(n // tm,),
        in_specs=[row, pl.BlockSpec((8, d), lambda i: (jnp.maximum(i * sub - 1, 0), 0)), const((3, d)),
                  const((1, d)), const(w1.shape), const(w2.shape), const((1, d)), const(a1.shape), const(a2.shape),
                  const(g1.shape), const(g2.shape)],
        out_specs=[row, row, row],
        out_shape=[jax.ShapeDtypeStruct((n, d), F32)] * 3,
        compiler_params=_cparams(("parallel",)),
        name="lora",
    )(x, x, mu3, w0.reshape(1, d), w1, w2, a0.reshape(1, d), a1, a2, g1, g2)


def _scan_groups(r, k, v, wl, a_gate, g, kk_w, ka_w, rk_w, gg, gb, s0, *, hd, gn_eps):
    c = r[0].shape[0]
    nh = LANES // hd
    hc = nh * c
    each = lambda f, *xs: [f(*x) for x in zip(*xs)]
    lane = lax.broadcasted_iota(jnp.int32, (1, LANES), 1)
    head_masks = [(lane >= h * hd) & (lane < (h + 1) * hd) for h in range(nh)]
    lr = lax.broadcasted_iota(jnp.int32, (LANES, LANES), 0) // hd
    lc = lax.broadcasted_iota(jnp.int32, (LANES, LANES), 1) // hd
    same_head = jnp.where(lr == lc, 1.0, 0.0).astype(BF16)
    trow = lax.broadcasted_iota(jnp.int32, (c, c), 0)
    tcol = lax.broadcasted_iota(jnp.int32, (c, c), 1)
    lower = jnp.where(trow >= tcol, 1.0, 0.0).astype(BF16)
    prow = lax.broadcasted_iota(jnp.int32, (hc, hc), 0)
    pcol = lax.broadcasted_iota(jnp.int32, (hc, hc), 1)
    same_blk = (prow // c) == (pcol // c)
    strict = same_blk & (prow > pcol)
    incl = same_blk & (prow >= pcol)
    eye = jnp.where(prow == pcol, 1.0, 0.0).astype(F32)

    def split(x):
        hi = x.astype(BF16)
        return hi, (x - hi.astype(F32)).astype(BF16)

    def head_sum(xs):
        parts = each(split, xs)
        return [jnp.dot(hi, same_head, preferred_element_type=F32)
                + jnp.dot(lo, same_head, preferred_element_type=F32) for hi, lo in parts]

    def per_head(x):
        return jnp.concatenate([jnp.where(hm, x, 0.0) for hm in head_masks], axis=0).astype(BF16)

    kk = each(lambda k_, w_: k_ * w_, k, kk_w)
    kk_n = head_sum(each(lambda x: x * x, kk))
    kk = each(lambda x, n_: x / jnp.maximum(jnp.sqrt(n_), 1e-12), kk, kk_n)
    k2 = each(lambda k_, a_, w_: k_ * (1.0 + (a_ - 1.0) * w_), k, a_gate, ka_w)
    lw = each(lambda w_: -jnp.exp(w_), wl)
    lw_parts = each(split, lw)
    cum = [jnp.dot(lower, hi, preferred_element_type=F32) + jnp.dot(lower, lo, preferred_element_type=F32)
           for hi, lo in lw_parts]
    gam = each(jnp.exp, cum)
    inv_gam = each(lambda x: jnp.exp(-x), cum)
    gam_end = each(lambda x: x[c - 1:c, :], gam)
    a_t = each(lambda kk_, cum_, lw_: -kk_ * jnp.exp(cum_ - lw_), kk, cum, lw)
    b_t = each(lambda kk_, a_, ig: kk_ * a_ * ig, kk, a_gate, inv_gam)
    k_t = each(lambda k2_, ig: k2_ * ig, k2, inv_gam)
    r_t = each(lambda r_, gm: r_ * gm, r, gam)

    s0b = each(lambda x: x.astype(BF16), s0)
    a_s0 = each(lambda x, s_: _nt(x.astype(BF16), s_), a_t, s0b)
    r_s0 = each(lambda x, s_: _nt(x.astype(BF16), s_), r_t, s0b)
    a2, r2, b2, k2h, v2 = (each(per_head, x) for x in (a_t, r_t, b_t, k_t, v))
    a_ab = each(lambda x, y_: jnp.where(strict, _nt(x, y_), 0.0), a2, b2)
    a_ak = each(lambda x, y_: jnp.where(strict, _nt(x, y_), 0.0), a2, k2h)
    a_rb = each(lambda x, y_: jnp.where(incl, _nt(x, y_), 0.0), r2, b2)
    a_rk = each(lambda x, y_: jnp.where(incl, _nt(x, y_), 0.0), r2, k2h)
    rhs = each(lambda as0, ak, v_: jnp.concatenate([jnp.where(hm, as0, 0.0) for hm in head_masks], axis=0)
               + _bdot(ak, v_), a_s0, a_ak, v2)
    m = a_ab
    inv = each(lambda x: eye + x, m)
    n_pow = 1
    while 2 * n_pow < c:
        m = each(lambda x: _bdot(x, x), m)
        inv = each(lambda p_, x: p_ + _bdot(p_, x), inv, m)
        n_pow *= 2
    u2 = each(_bdot, inv, rhs)
    y2 = each(lambda rb, u_, rk, v_: _bdot(rb, u_) + _bdot(rk, v_), a_rb, u2, a_rk, v2)
    y = each(lambda rs, y2_: rs + sum(y2_[h * c:(h + 1) * c, :] for h in range(nh)), r_s0, y2)
    uv = each(lambda u_, v_: jnp.concatenate([u_.astype(BF16), v_], axis=0), u2, v2)
    bkg = each(lambda b_, k_, ge: jnp.concatenate([per_head(b_ * ge), per_head(k_ * ge)], axis=0), b_t, k_t, gam_end)
    s_new = each(lambda s_, ge, uv_, bkg_: s_ * ge + _tn(uv_, bkg_), s0, gam_end, uv, bkg)

    inv_hd = 1.0 / hd
    mean = head_sum(y)
    yc = each(lambda y_, m_: y_ - m_ * inv_hd, y, mean)
    var = head_sum(each(lambda x: x * x, yc))
    yn = each(lambda yc_, var_, gg_, gb_: yc_ * lax.rsqrt(var_ * inv_hd + gn_eps) * gg_ + gb_, yc, var, gg, gb)
    rk_sum = head_sum(each(lambda r_, k2_, w_: r_ * k2_ * w_, r, k2, rk_w))
    out = each(lambda yn_, rk_, v_, g_: (yn_ + rk_ * v_) * g_, yn, rk_sum, v, g)
    return out, s_new


def _scan_kernel(r_ref, k_ref, v_ref, wl_ref, a_ref, g_ref, kk_ref, ka_ref, rk_ref, gg_ref, gb_ref, o_ref, state, *,
                 hd, gn_eps):
    @pl.when(pl.program_id(2) == 0)
    def _():
        state[...] = jnp.zeros_like(state)

    ng = state.shape[0]
    sls = [slice(p * LANES, (p + 1) * LANES) for p in range(ng)]
    tok3 = lambda ref: [ref[0, :, sl] for sl in sls]
    tok2 = lambda ref: [ref[:, sl] for sl in sls]
    out, s_new = _scan_groups(tok3(r_ref), tok3(k_ref), tok3(v_ref), tok2(wl_ref), tok2(a_ref), tok2(g_ref),
                              tok2(kk_ref), tok2(ka_ref), tok2(rk_ref), tok2(gg_ref), tok2(gb_ref),
                              [state[p] for p in range(ng)], hd=hd, gn_eps=gn_eps)
    for p in range(ng):
        state[p] = s_new[p]
        o_ref[:, sls[p]] = out[p].astype(o_ref.dtype)


def _rwkv_scan(rkv, wl, a, g, k_k, k_a, r_k, gn_g, gn_b, batch, hd, gn_eps, chunk=64, groups=16):
    _, n, d = rkv.shape
    t = n // batch
    nc = t // chunk
    groups = min(groups, d // LANES)
    w = groups * LANES
    tok = lambda j: pl.BlockSpec((1, chunk, w), functools.partial(lambda b, p, c, j: (j, b * nc + c, p), j=j))
    tok2 = pl.BlockSpec((chunk, w), lambda b, p, c: (b * nc + c, p))
    par = pl.BlockSpec((1, w), lambda b, p, c: (0, p))
    return pl.pallas_call(
        functools.partial(_scan_kernel, hd=hd, gn_eps=gn_eps),
        grid=(batch, d // w, nc),
        in_specs=[tok(0), tok(1), tok(2), tok2, tok2, tok2, par, par, par, par, par],
        out_specs=tok2,
        out_shape=jax.ShapeDtypeStruct((n, d), BF16),
        scratch_shapes=[pltpu.VMEM((groups, LANES, LANES), F32)],
        compiler_params=_cparams(("parallel", "parallel", "arbitrary")),
        name="rwkv_scan",
    )(rkv, rkv, rkv, wl, a, g, k_k.reshape(1, d), k_a.reshape(1, d), r_k.reshape(1, d), gn_g.reshape(1, d),
      gn_b.reshape(1, d))


def _pad_lora(w_in, w_out):
    r = w_in.shape[1]
    rp = -(-r // LANES) * LANES
    return (jnp.pad(w_in, ((0, 0), (0, rp - r))).astype(BF16), jnp.pad(w_out, ((0, rp - r), (0, 0))).astype(BF16))


def kernel(x, ev_w_in, ev_b_a, ev_w_s, ev_b_s, ev_g_v, ev_b_v, ev_b_f, ev_w_out, rw_mu, rw_w_rkv, rw_w0, rw_w1, rw_w2, rw_a0, rw_a1, rw_a2, rw_g1, rw_g2, rw_k_k, rw_k_a, rw_r_k, rw_gn_g, rw_gn_b, rw_w_o, ln_g, ln_b, w_router, b_router, w_gu, w_down):
    batch, t, d = x.shape
    depth = ln_g.shape[0]
    alpha = (2 * depth) ** 0.25
    h = x.reshape(batch * t, d)
    for layer in range(depth):
        i = layer // 2
        if layer % 2 == 0:
            aw = ev_g_v.shape[1]
            heads = ev_b_f.shape[1]
            q_col = 2 * aw
            n_main = q_col + 3 * heads * LANES
            w_in = ev_w_in[i]
            proj = _matmul(h, w_in.astype(BF16), n_main, BF16)
            c = _fgate(h, w_in[:, n_main:].T, ev_b_f[i], batch)
            y_a = _sgu(proj, ev_b_a[i], ev_w_s[i], ev_b_s[i], ev_g_v[i], ev_b_v[i])
            y_b = _fox_attention(proj, c, batch, heads, q_col)
            h = _proj_ln([y_a, y_b], ev_w_out[i].astype(BF16), h, ln_g[layer, 0], ln_b[layer, 0], alpha)
        else:
            hd = rw_r_k.shape[2]
            mu = rw_mu[i]
            rkv = _rkv_proj(h, mu[:3], rw_w_rkv[i].astype(BF16), t)
            w1, w2 = _pad_lora(rw_w1[i], rw_w2[i])
            a1, a2 = _pad_lora(rw_a1[i], rw_a2[i])
            g1, g2 = _pad_lora(rw_g1[i], rw_g2[i])
            wl, a, g = _lora(h, mu[3:], rw_w0[i], w1, w2, rw_a0[i], a1, a2, g1, g2, t)
            y = _rwkv_scan(rkv, wl, a, g, rw_k_k[i], rw_k_a[i], rw_r_k[i].reshape(-1), rw_gn_g[i], rw_gn_b[i],
                           batch, hd, hd * 1e-5)
            h = _proj_ln([y], rw_w_o[i].astype(BF16), h, ln_g[layer, 0], ln_b[layer, 0], alpha)
        h = _moe_ln(h, w_router, b_router, w_gu[layer].astype(BF16), w_down[layer].astype(BF16), ln_g[layer, 1],
                    ln_b[layer, 1], alpha)
    return h.reshape(batch, t, d)
```

```python
import functools

import jax
import jax.numpy as jnp
from jax import lax
from jax.experimental import pallas as pl
from jax.experimental.pallas import tpu as pltpu

F32 = jnp.float32
BF16 = jnp.bfloat16
HI = lax.Precision.HIGHEST

LN_EPS = 1e-5
N_GROUPS = 4
LANES = 128
FOX_BLOCK = 512
LOG2E = 1.4426950408889634
VMEM_LIMIT = 56 * 1024 * 1024


def _cparams(sem):
    return pltpu.CompilerParams(dimension_semantics=sem, vmem_limit_bytes=VMEM_LIMIT)


def _layer_norm(x, g, b, eps):
    mu = jnp.mean(x, -1, keepdims=True)
    xc = x - mu
    var = jnp.mean(xc * xc, -1, keepdims=True)
    return xc * lax.rsqrt(var + eps) * g + b


def _nt(a, b, **kw):
    return lax.dot_general(a, b, (((1,), (1,)), ((), ())), preferred_element_type=F32, **kw)


def _tn(a, b, **kw):
    return lax.dot_general(a, b, (((0,), (0,)), ((), ())), preferred_element_type=F32, **kw)


def _bdot(a, b):
    return jnp.dot(a.astype(BF16), b.astype(BF16), preferred_element_type=F32)


def _mm_kernel(a_ref, w_ref, o_ref, a_bf16):
    @pl.when(pl.program_id(1) == 0)
    def _():
        a_bf16[...] = a_ref[...].astype(BF16)

    o_ref[...] = jnp.dot(a_bf16[...], w_ref[...], preferred_element_type=F32).astype(o_ref.dtype)


def _matmul(a, w, n_cols, out_dtype, tm=1024, tn=512):
    m, k = a.shape
    tm = min(tm, m)
    tn = next(c for c in (tn, 256, LANES) if n_cols % c == 0)
    return pl.pallas_call(
        _mm_kernel,
        grid=(m // tm, n_cols // tn),
        in_specs=[pl.BlockSpec((tm, k), lambda i, j: (i, 0)), pl.BlockSpec((k, tn), lambda i, j: (0, j))],
        out_specs=pl.BlockSpec((tm, tn), lambda i, j: (i, j)),
        out_shape=jax.ShapeDtypeStruct((m, n_cols), out_dtype),
        scratch_shapes=[pltpu.VMEM((tm, k), BF16)],
        compiler_params=_cparams(("parallel", "arbitrary")),
        name="matmul",
    )(a, w)


def _proj_ln_kernel(*refs, n_in, alpha):
    a_refs, w_refs = refs[:n_in], refs[n_in:2 * n_in]
    res_ref, g_ref, b_ref, o_ref = refs[2 * n_in:]
    acc = alpha * res_ref[...]
    for a_ref, w_ref in zip(a_refs, w_refs):
        acc = acc + jnp.dot(a_ref[...], w_ref[...], preferred_element_type=F32)
    o_ref[...] = _layer_norm(acc, g_ref[...], b_ref[...], LN_EPS)


def _proj_ln(a_list, w, res, g, b, alpha, tm=256):
    m, d = res.shape
    n_in = len(a_list)
    kc = a_list[0].shape[1]
    in_specs = [pl.BlockSpec((tm, kc), lambda i: (i, 0)) for _ in a_list]
    in_specs += [pl.BlockSpec((kc, d), functools.partial(lambda i, r: (r, 0), r=r)) for r in range(n_in)]
    in_specs += [pl.BlockSpec((tm, d), lambda i: (i, 0)), pl.BlockSpec((1, d), lambda i: (0, 0)),
                 pl.BlockSpec((1, d), lambda i: (0, 0))]
    return pl.pallas_call(
        functools.partial(_proj_ln_kernel, n_in=n_in, alpha=alpha),
        grid=(m // tm,),
        in_specs=in_specs,
        out_specs=pl.BlockSpec((tm, d), lambda i: (i, 0)),
        out_shape=jax.ShapeDtypeStruct((m, d), F32),
        compiler_params=_cparams(("parallel",)),
        name="proj_ln",
    )(*a_list, *([w] * n_in), res, g.reshape(1, d), b.reshape(1, d))


def _fgate_kernel(x_ref, wf_ref, bf_ref, c_ref, carry):
    @pl.when(pl.program_id(1) == 0)
    def _():
        carry[...] = jnp.zeros_like(carry)

    tm = x_ref.shape[0]
    z = _nt(wf_ref[...], x_ref[...], precision=HI) + bf_ref[...]
    log_f = jnp.minimum(z, 0.0) - jnp.log1p(jnp.exp(-jnp.abs(z)))
    row = lax.broadcasted_iota(jnp.int32, (tm, tm), 0)
    col = lax.broadcasted_iota(jnp.int32, (tm, tm), 1)
    upper = jnp.where(row <= col, 1.0, 0.0).astype(F32)
    c = jnp.dot(log_f, upper, preferred_element_type=F32, precision=HI) + carry[...]
    c_ref[0] = c
    carry[...] = carry[...] + jnp.sum(log_f, axis=-1, keepdims=True)


def _fgate(x2d, wf_t, b_f, batch, tm=512):
    n, d = x2d.shape
    h = wf_t.shape[0]
    t = n // batch
    nt = t // tm
    return pl.pallas_call(
        _fgate_kernel,
        grid=(batch, nt),
        in_specs=[pl.BlockSpec((tm, d), lambda b, i: (b * nt + i, 0)), pl.BlockSpec((h, d), lambda b, i: (0, 0)),
                  pl.BlockSpec((h, 1), lambda b, i: (0, 0))],
        out_specs=pl.BlockSpec((1, h, tm), lambda b, i: (b, 0, i)),
        out_shape=jax.ShapeDtypeStruct((batch, h, t), F32),
        scratch_shapes=[pltpu.VMEM((h, 1), F32)],
        compiler_params=_cparams(("parallel", "arbitrary")),
        name="fgate",
    )(x2d, wf_t, b_f.reshape(h, 1))


def _gelu_tanh(x):
    return 0.5 * x * (1.0 + jnp.tanh(0.7978845608028654 * (x + 0.044715 * (x * x * x))))


def _sgu_kernel(z_ref, ba_ref, ws_ref, bs_ref, gv_ref, bv_ref, o_ref, *, chunk, groups):
    aw = o_ref.shape[1]
    gd = aw // groups
    z = _gelu_tanh(z_ref[...].astype(F32) + ba_ref[...])
    u = z[:, :aw]
    v = _layer_norm(z[:, aw:], gv_ref[...], bv_ref[...], LN_EPS).astype(BF16)
    row = lax.broadcasted_iota(jnp.int32, (chunk, chunk), 0)
    col = lax.broadcasted_iota(jnp.int32, (chunk, chunk), 1)
    causal = row >= col
    bs = bs_ref[...]
    for g in range(groups):
        w_g = jnp.where(causal, ws_ref[g], 0.0).astype(BF16)
        for c in range(z.shape[0] // chunk):
            rs = slice(c * chunk, (c + 1) * chunk)
            cs = slice(g * gd, (g + 1) * gd)
            s = jnp.dot(w_g, v[rs, cs], preferred_element_type=F32) + bs[:, g:g + 1]
            o_ref[rs, cs] = (u[rs, cs] * s).astype(o_ref.dtype)


def _sgu(proj, b_a, w_s, b_s, g_v, b_v, tm=512):
    n = proj.shape[0]
    groups, chunk, _ = w_s.shape
    aw = g_v.shape[0]
    return pl.pallas_call(
        functools.partial(_sgu_kernel, chunk=chunk, groups=groups),
        grid=(n // tm,),
        in_specs=[pl.BlockSpec((tm, 2 * aw), lambda i: (i, 0)), pl.BlockSpec((1, 2 * aw), lambda i: (0, 0)),
                  pl.BlockSpec((groups, chunk, chunk), lambda i: (0, 0, 0)),
                  pl.BlockSpec((chunk, groups), lambda i: (0, 0)), pl.BlockSpec((1, aw), lambda i: (0, 0)),
                  pl.BlockSpec((1, aw), lambda i: (0, 0))],
        out_specs=pl.BlockSpec((tm, aw), lambda i: (i, 0)),
        out_shape=jax.ShapeDtypeStruct((n, aw), BF16),
        compiler_params=_cparams(("parallel",)),
        name="sgu",
    )(proj, b_a.reshape(1, -1), w_s, b_s.T, g_v.reshape(1, aw), b_v.reshape(1, aw))


def _vt_kernel(x_ref, w_ref, o_ref):
    o_ref[0] = _nt(w_ref[...], x_ref[...].astype(BF16)).astype(o_ref.dtype)


def _proj_transposed(x, w_t, blk):
    n, d = x.shape
    rows = w_t.shape[0]
    return pl.pallas_call(
        _vt_kernel,
        grid=(n // blk,),
        in_specs=[pl.BlockSpec((blk, d), lambda i: (i, 0)), pl.BlockSpec((rows, d), lambda i: (0, 0))],
        out_specs=pl.BlockSpec((1, rows, blk), lambda i: (i, 0, 0)),
        out_shape=jax.ShapeDtypeStruct((n // blk, rows, blk), BF16),
        compiler_params=_cparams(("parallel",)),
        name="proj_transposed",
    )(x, w_t)


def _fox_kernel(q_ref, k_ref, vt_ref, cq_ref, ck_ref, o_ref, m_scr, l_scr, acc_scr, *, blk, nh):
    qi = pl.program_id(2)
    heads = range(nh)
    hs = [slice(h * LANES, (h + 1) * LANES) for h in heads]
    m_scr[...] = jnp.full_like(m_scr, -jnp.inf)
    l_scr[...] = jnp.zeros_like(l_scr)
    acc_scr[...] = jnp.zeros_like(acc_scr)
    q = [q_ref[:, hs[h]] for h in heads]
    cq = [cq_ref[0, h, pl.ds(qi, 1), :] * LOG2E for h in heads]

    def step(ki, masked):
        ks = pl.multiple_of(ki * blk, blk)
        t = [_nt(k_ref[pl.ds(ks, blk), hs[h]], q[h]) - ck_ref[0, h, pl.ds(ks, blk), :] * LOG2E for h in heads]
        if masked:
            row = lax.broadcasted_iota(jnp.int32, (blk, blk), 0)
            col = lax.broadcasted_iota(jnp.int32, (blk, blk), 1)
            t = [jnp.where(row <= col, x, -jnp.inf) for x in t]
        m_prev = [m_scr[h] for h in heads]
        m_new = [jnp.maximum(m_prev[h], cq[h] + jnp.max(t[h], axis=0, keepdims=True)) for h in heads]
        p = [jnp.exp2(t[h] - (m_new[h] - cq[h])) for h in heads]
        corr = [jnp.exp2(m_prev[h] - m_new[h]) for h in heads]
        for h in heads:
            l_scr[h] = corr[h] * l_scr[h] + jnp.sum(p[h], axis=0, keepdims=True)
            acc_scr[h] = corr[h] * acc_scr[h] + jnp.dot(vt_ref[ki, hs[h], :], p[h].astype(BF16),
                                                         preferred_element_type=F32)
            m_scr[h] = m_new[h]

    def body(ki, carry):
        step(ki, False)
        return carry

    lax.fori_loop(0, qi, body, 0)
    step(qi, True)
    for h in heads:
        o_ref[:, hs[h]] = jnp.transpose(acc_scr[h] / l_scr[h]).astype(o_ref.dtype)


def _fox_attention(proj, vt, c, batch, heads, q_col, blk, nh=2):
    n = proj.shape[0]
    t = n // batch
    nb = t // blk
    dh = LANES
    nh = min(nh, heads)
    w = nh * dh
    q0, k0 = q_col // w, (q_col + heads * dh) // w
    c_col = c.reshape(batch, heads, t, 1)
    c_row = c.reshape(batch, heads, nb, blk)
    return pl.pallas_call(
        functools.partial(_fox_kernel, blk=blk, nh=nh),
        grid=(batch, heads // nh, nb),
        in_specs=[pl.BlockSpec((blk, w), lambda b, h, i: (b * nb + i, q0 + h)),
                  pl.BlockSpec((t, w), lambda b, h, i: (b, k0 + h)),
                  pl.BlockSpec((nb, w, blk), lambda b, h, i: (b, h, 0)),
                  pl.BlockSpec((1, nh, nb, blk), lambda b, h, i: (b, h, 0, 0)),
                  pl.BlockSpec((1, nh, t, 1), lambda b, h, i: (b, h, 0, 0))],
        out_specs=pl.BlockSpec((blk, w), lambda b, h, i: (b * nb + i, h)),
        out_shape=jax.ShapeDtypeStruct((n, heads * dh), BF16),
        scratch_shapes=[pltpu.VMEM((nh, 1, blk), F32), pltpu.VMEM((nh, 1, blk), F32),
                        pltpu.VMEM((nh, dh, blk), F32)],
        compiler_params=_cparams(("parallel", "parallel", "arbitrary")),
        name="fox_attention",
    )(proj, proj, vt, c_row, c_col)


def _first_max(p, lane, valid):
    pm = jnp.where(valid, p, -2.0)
    m = jnp.max(pm, axis=-1, keepdims=True)
    idx = jnp.min(jnp.where(pm == m, lane, float(LANES)), axis=-1, keepdims=True)
    return m, idx


def _router_top2(x, wr_ref, br_ref, n_experts):
    rows = x.shape[0]
    per = n_experts // N_GROUPS
    lane = lax.broadcasted_iota(jnp.int32, (rows, LANES), 1).astype(F32)
    real = lane < n_experts
    x_hi = x.astype(BF16)
    x_lo = (x - x_hi.astype(F32)).astype(BF16)
    logits = (jnp.dot(x_hi, wr_ref[0], preferred_element_type=F32) + jnp.dot(x_lo, wr_ref[0], preferred_element_type=F32)
              + jnp.dot(x_hi, wr_ref[1], preferred_element_type=F32) + br_ref[...])
    logits = jnp.where(real, logits, -jnp.inf)
    e = jnp.exp(logits - jnp.max(logits, axis=-1, keepdims=True))
    probs = e / jnp.sum(e, axis=-1, keepdims=True)
    best_score = jnp.full((rows, 1), -1.0, F32)
    best_group = jnp.zeros((rows, 1), F32)
    for grp in range(N_GROUPS):
        in_g = (lane >= grp * per) & (lane < (grp + 1) * per)
        m1, i1 = _first_max(probs, lane, in_g)
        m2, _ = _first_max(probs, lane, in_g & (lane != i1))
        score = m1 + m2
        take = score > best_score
        best_score = jnp.where(take, score, best_score)
        best_group = jnp.where(take, float(grp), best_group)
    in_sel = (lane >= best_group * per) & (lane < (best_group + 1) * per)
    p1, i1 = _first_max(probs, lane, in_sel)
    p2, i2 = _first_max(probs, lane, in_sel & (lane != i1))
    tot = p1 + p2
    return lane, i1, i2, p1 / tot, p2 / tot


def _route_kernel(x_ref, wr_ref, br_ref, meta_ref, cnt_ref, carry, *, n_experts):
    @pl.when(pl.program_id(0) == 0)
    def _():
        carry[...] = jnp.zeros_like(carry)

    tm = x_ref.shape[0]
    lane, i1, i2, p1, p2 = _router_top2(x_ref[...], wr_ref, br_ref, n_experts)
    row = lax.broadcasted_iota(jnp.int32, (tm, tm), 0)
    col = lax.broadcasted_iota(jnp.int32, (tm, tm), 1)
    before = jnp.where(row > col, 1.0, 0.0).astype(BF16)
    onehot = jnp.where((lane == i1) | (lane == i2), 1.0, 0.0)
    seen = jnp.dot(before, onehot.astype(BF16), preferred_element_type=F32) + carry[...]
    r1 = jnp.sum(jnp.where(lane == i1, seen, 0.0), axis=-1, keepdims=True)
    r2 = jnp.sum(jnp.where(lane == i2, seen, 0.0), axis=-1, keepdims=True)
    meta = jnp.zeros((tm, LANES), F32)
    for j, val in enumerate((i1, i2, p1, p2, r1, r2)):
        meta = jnp.where(lane == j, val, meta)
    meta_ref[...] = meta
    carry[...] = carry[...] + jnp.sum(onehot, axis=0, keepdims=True)
    cnt_ref[...] = carry[...]


def _route(x, w_router, b_router, tm=512):
    n, d = x.shape
    n_experts = w_router.shape[1]
    wr = jnp.zeros((d, LANES), F32).at[:, :n_experts].set(w_router)
    wr_hi = wr.astype(BF16)
    wr = jnp.stack([wr_hi, (wr - wr_hi.astype(F32)).astype(BF16)])
    br = jnp.zeros((1, LANES), F32).at[0, :n_experts].set(b_router)
    return pl.pallas_call(
        functools.partial(_route_kernel, n_experts=n_experts),
        grid=(n // tm,),
        in_specs=[pl.BlockSpec((tm, d), lambda i: (i, 0)), pl.BlockSpec((2, d, LANES), lambda i: (0, 0, 0)),
                  pl.BlockSpec((1, LANES), lambda i: (0, 0))],
        out_specs=[pl.BlockSpec((tm, LANES), lambda i: (i, 0)), pl.BlockSpec((1, LANES), lambda i: (0, 0))],
        out_shape=[jax.ShapeDtypeStruct((n, LANES), F32), jax.ShapeDtypeStruct((1, LANES), F32)],
        scratch_shapes=[pltpu.VMEM((1, LANES), F32)],
        compiler_params=_cparams(("arbitrary",)),
        name="route",
    )(x, wr, br)


def _row_copy(src_ref, src_row, dst_ref, dst_row, sem):
    return pltpu.make_async_copy(src_ref.at[pl.ds(src_row, 1)], dst_ref.at[pl.ds(dst_row, 1)], sem)


def _dispatch_kernel(pos_ref, x_ref, _, xs_ref, sem):
    tm = x_ref.shape[0]

    def start(r, carry):
        for s in range(pos_ref.shape[0]):
            _row_copy(x_ref, r, xs_ref, pos_ref[s, r], sem).start()
        return carry

    def wait(r, carry):
        for s in range(pos_ref.shape[0]):
            _row_copy(x_ref, r, xs_ref, pos_ref[s, r], sem).wait()
        return carry

    lax.fori_loop(0, tm, start, 0, unroll=8)
    lax.fori_loop(0, tm, wait, 0, unroll=8)


def _dispatch(x, pos, n_rows, tm=256):
    n, d = x.shape
    return pl.pallas_call(
        _dispatch_kernel,
        grid=(n // tm,),
        in_specs=[pl.BlockSpec((pos.shape[0], tm), lambda i: (0, i), memory_space=pltpu.SMEM),
                  pl.BlockSpec((tm, d), lambda i: (i, 0)), pl.BlockSpec(memory_space=pl.ANY)],
        out_specs=pl.BlockSpec(memory_space=pl.ANY),
        out_shape=jax.ShapeDtypeStruct((n_rows, d), x.dtype),
        scratch_shapes=[pltpu.SemaphoreType.DMA],
        input_output_aliases={2: 0},
        compiler_params=_cparams(("arbitrary",)),
        name="moe_dispatch",
    )(pos, x, jnp.zeros((n_rows, d), x.dtype))


def _experts_kernel(te_ref, na_ref, xs_ref, wgu_ref, wd_ref, ys_ref, wgu_b, wd_b):
    j = pl.program_id(0)
    live = j < na_ref[0]
    fresh = (j == 0) | (te_ref[j] != te_ref[jnp.maximum(j - 1, 0)])

    @pl.when(live & fresh)
    def _():
        wgu_b[...] = wgu_ref[0].astype(BF16)
        wd_b[...] = wd_ref[0].astype(BF16)

    @pl.when(live)
    def _():
        de = wd_b.shape[0]
        gu = jnp.dot(xs_ref[...].astype(BF16), wgu_b[...], preferred_element_type=F32)
        gpart, upart = gu[:, :de], gu[:, de:]
        h = (gpart / (1.0 + jnp.exp(-gpart))) * upart
        ys_ref[...] = jnp.dot(h.astype(BF16), wd_b[...], preferred_element_type=F32)

    @pl.when(jnp.logical_not(live))
    def _():
        ys_ref[...] = jnp.zeros_like(ys_ref)


def _experts(xs, w_gu, w_down, tile_expert, n_active, tm):
    p, d = xs.shape
    de = w_down.shape[1]
    live = lambda j, na: jnp.maximum(jnp.minimum(j, na[0] - 1), 0)
    grid_spec = pltpu.PrefetchScalarGridSpec(
        num_scalar_prefetch=2,
        grid=(p // tm,),
        in_specs=[pl.BlockSpec((tm, d), lambda j, te, na: (live(j, na), 0)),
                  pl.BlockSpec((1, d, 2 * de), lambda j, te, na: (te[live(j, na)], 0, 0),
                               pipeline_mode=pl.Buffered(1)),
                  pl.BlockSpec((1, de, d), lambda j, te, na: (te[live(j, na)], 0, 0),
                               pipeline_mode=pl.Buffered(1))],
        out_specs=pl.BlockSpec((tm, d), lambda j, te, na: (j, 0)),
        scratch_shapes=[pltpu.VMEM((d, 2 * de), BF16), pltpu.VMEM((de, d), BF16)],
    )
    return pl.pallas_call(
        _experts_kernel,
        grid_spec=grid_spec,
        out_shape=jax.ShapeDtypeStruct((p, d), F32),
        compiler_params=_cparams(("arbitrary",)),
        name="moe_experts",
    )(tile_expert, n_active, xs, w_gu, w_down)


def _combine_kernel(pos_ref, x_ref, gw_ref, ys_ref, g_ref, b_ref, o_ref, buf, sem, *, alpha):
    tm = x_ref.shape[0]
    n_slots = pos_ref.shape[0]

    def start(r, carry):
        for s in range(n_slots):
            _row_copy(ys_ref, pos_ref[s, r], buf.at[s], r, sem).start()
        return carry

    def wait(r, carry):
        for s in range(n_slots):
            _row_copy(ys_ref, pos_ref[s, r], buf.at[s], r, sem).wait()
        return carry

    lax.fori_loop(0, tm, start, 0, unroll=8)
    lax.fori_loop(0, tm, wait, 0, unroll=8)
    gw = gw_ref[...]
    y = alpha * x_ref[...]
    for s in range(n_slots):
        y = y + gw[:, s:s + 1] * buf[s]
    o_ref[...] = _layer_norm(y, g_ref[...], b_ref[...], LN_EPS)


def _combine_ln(x, ys, pos, gw, g, b, alpha, tm=256):
    n, d = x.shape
    n_slots = pos.shape[0]
    return pl.pallas_call(
        functools.partial(_combine_kernel, alpha=alpha),
        grid=(n // tm,),
        in_specs=[pl.BlockSpec((n_slots, tm), lambda i: (0, i), memory_space=pltpu.SMEM),
                  pl.BlockSpec((tm, d), lambda i: (i, 0)), pl.BlockSpec((tm, n_slots), lambda i: (i, 0)),
                  pl.BlockSpec(memory_space=pl.ANY), pl.BlockSpec((1, d), lambda i: (0, 0)),
                  pl.BlockSpec((1, d), lambda i: (0, 0))],
        out_specs=pl.BlockSpec((tm, d), lambda i: (i, 0)),
        out_shape=jax.ShapeDtypeStruct((n, d), F32),
        scratch_shapes=[pltpu.VMEM((n_slots, tm, d), F32), pltpu.SemaphoreType.DMA],
        compiler_params=_cparams(("arbitrary",)),
        name="moe_combine",
    )(pos, x, gw, ys, g.reshape(1, d), b.reshape(1, d))


def _moe_ln(x, w_router, b_router, w_gu, w_down, g, b, alpha, tm=256):
    n, d = x.shape
    n_experts = w_down.shape[0]
    meta, counts = _route(x, w_router, b_router)
    eid = meta[:, 0:2].astype(jnp.int32)
    gw = meta[:, 2:4]
    rank = meta[:, 4:6].astype(jnp.int32)
    counts = counts[0, :n_experts].astype(jnp.int32)
    tiles = (counts + tm - 1) // tm
    tile_end = jnp.cumsum(tiles)
    row_off = (tile_end - tiles) * tm
    n_tiles = (eid.size + tm - 1) // tm + n_experts
    tile_expert = jnp.sum(jnp.arange(n_tiles)[:, None] >= tile_end[None, :], axis=1).astype(jnp.int32)
    tile_expert = jnp.minimum(tile_expert, n_experts - 1)
    n_active = tile_end[-1:].astype(jnp.int32)
    sel = eid[:, :, None] == jnp.arange(n_experts)[None, None, :]
    pos = (jnp.sum(jnp.where(sel, row_off[None, None, :], 0), axis=-1) + rank).T
    xs = _dispatch(x, pos, n_tiles * tm)
    ys = _experts(xs, w_gu, w_down, tile_expert, n_active, tm)
    return _combine_ln(x, ys, pos, gw, g, b, alpha)


def _token_shift(x_ref, prev_ref, first):
    x = x_ref[...]
    prev_row = jnp.where(first, 0.0, prev_ref[7:8, :])
    row = lax.broadcasted_iota(jnp.int32, x.shape, 0)
    return x, jnp.where(row == 0, prev_row, pltpu.roll(x, 1, 0))


def _rkv_kernel(x_ref, prev_ref, mu_ref, w_ref, o_ref, *, tiles_per_seq):
    first = pl.program_id(0) % tiles_per_seq == 0
    x, xp = _token_shift(x_ref, prev_ref, first)
    xm = x + (xp - x) * mu_ref[0]
    o_ref[0] = jnp.dot(xm.astype(BF16), w_ref[0], preferred_element_type=F32)


def _rkv_proj(x, mu3, w_rkv, t, tm=512):
    n, d = x.shape
    sub = tm // 8
    return pl.pallas_call(
        functools.partial(_rkv_kernel, tiles_per_seq=t // tm),
        grid=(n // tm, 3),
        in_specs=[pl.BlockSpec((tm, d), lambda i, j: (i, 0)),
                  pl.BlockSpec((8, d), lambda i, j: (jnp.maximum(i * sub - 1, 0), 0)),
                  pl.BlockSpec((1, 1, d), lambda i, j: (j, 0, 0)), pl.BlockSpec((1, d, d), lambda i, j: (j, 0, 0))],
        out_specs=pl.BlockSpec((1, tm, d), lambda i, j: (j, i, 0)),
        out_shape=jax.ShapeDtypeStruct((3, n, d), F32),
        compiler_params=_cparams(("parallel", "arbitrary")),
        name="rkv_proj",
    )(x, x, mu3.reshape(3, 1, d), w_rkv)


def _lora_kernel(x_ref, prev_ref, mu_ref, w0_ref, w1_ref, w2_ref, a0_ref, a1_ref, a2_ref, g1_ref, g2_ref, wl_ref,
                 a_ref, g_ref, *, tiles_per_seq):
    first = pl.program_id(0) % tiles_per_seq == 0
    x, xp = _token_shift(x_ref, prev_ref, first)
    dx = xp - x

    def mm(a, w_ref):
        return jnp.dot(a.astype(BF16), w_ref[...], preferred_element_type=F32)

    zw = w0_ref[...] + mm(jnp.tanh(mm(x + dx * mu_ref[0:1, :], w1_ref)), w2_ref)
    wl_ref[...] = jnp.minimum(zw, 0.0) - jnp.log1p(jnp.exp(-jnp.abs(zw))) - 0.5
    za = a0_ref[...] + mm(mm(x + dx * mu_ref[1:2, :], a1_ref), a2_ref)
    a_ref[...] = 1.0 / (1.0 + jnp.exp(-za))
    zg = mm(x + dx * mu_ref[2:3, :], g1_ref)
    g_ref[...] = mm(1.0 / (1.0 + jnp.exp(-zg)), g2_ref)


def _lora(x, mu3, w0, w1, w2, a0, a1, a2, g1, g2, t, tm=512):
    n, d = x.shape
    sub = tm // 8
    const = lambda shape: pl.BlockSpec(shape, lambda i: (0, 0))
    row = pl.BlockSpec((tm, d), lambda i: (i, 0))
    return pl.pallas_call(
        functools.partial(_lora_kernel, tiles_per_seq=t // tm),
        grid=(n // tm,),
        in_specs=[row, pl.BlockSpec((8, d), lambda i: (jnp.maximum(i * sub - 1, 0), 0)), const((3, d)),
                  const((1, d)), const(w1.shape), const(w2.shape), const((1, d)), const(a1.shape), const(a2.shape),
                  const(g1.shape), const(g2.shape)],
        out_specs=[row, row, row],
        out_shape=[jax.ShapeDtypeStruct((n, d), F32)] * 3,
        compiler_params=_cparams(("parallel",)),
        name="lora",
    )(x, x, mu3, w0.reshape(1, d), w1, w2, a0.reshape(1, d), a1, a2, g1, g2)


def _scan_groups(r, k, v, wl, a_gate, g, kk_w, ka_w, rk_w, gg, gb, s0, *, hd, gn_eps):
    c = r[0].shape[0]
    nh = LANES // hd
    hc = nh * c
    each = lambda f, *xs: [f(*x) for x in zip(*xs)]
    lane = lax.broadcasted_iota(jnp.int32, (1, LANES), 1)
    head_masks = [(lane >= h * hd) & (lane < (h + 1) * hd) for h in range(nh)]
    lr = lax.broadcasted_iota(jnp.int32, (LANES, LANES), 0) // hd
    lc = lax.broadcasted_iota(jnp.int32, (LANES, LANES), 1) // hd
    same_head = jnp.where(lr == lc, 1.0, 0.0).astype(BF16)
    trow = lax.broadcasted_iota(jnp.int32, (c, c), 0)
    tcol = lax.broadcasted_iota(jnp.int32, (c, c), 1)
    lower = jnp.where(trow >= tcol, 1.0, 0.0).astype(BF16)
    prow = lax.broadcasted_iota(jnp.int32, (hc, hc), 0)
    pcol = lax.broadcasted_iota(jnp.int32, (hc, hc), 1)
    same_blk = (prow // c) == (pcol // c)
    strict = same_blk & (prow > pcol)
    incl = same_blk & (prow >= pcol)
    eye = jnp.where(prow == pcol, 1.0, 0.0).astype(F32)

    def split(x):
        hi = x.astype(BF16)
        return hi, (x - hi.astype(F32)).astype(BF16)

    def head_sum(xs):
        parts = each(split, xs)
        return [jnp.dot(hi, same_head, preferred_element_type=F32)
                + jnp.dot(lo, same_head, preferred_element_type=F32) for hi, lo in parts]

    def per_head(x):
        return jnp.concatenate([jnp.where(hm, x, 0.0) for hm in head_masks], axis=0).astype(BF16)

    kk = each(lambda k_, w_: k_ * w_, k, kk_w)
    kk_n = head_sum(each(lambda x: x * x, kk))
    kk = each(lambda x, n_: x / jnp.maximum(jnp.sqrt(n_), 1e-12), kk, kk_n)
    k2 = each(lambda k_, a_, w_: k_ * (1.0 + (a_ - 1.0) * w_), k, a_gate, ka_w)
    lw = each(lambda w_: -jnp.exp(w_), wl)
    lw_parts = each(split, lw)
    cum = [jnp.dot(lower, hi, preferred_element_type=F32) + jnp.dot(lower, lo, preferred_element_type=F32)
           for hi, lo in lw_parts]
    gam = each(jnp.exp, cum)
    inv_gam = each(lambda x: jnp.exp(-x), cum)
    gam_end = each(lambda x: x[c - 1:c, :], gam)
    a_t = each(lambda kk_, cum_, lw_: -kk_ * jnp.exp(cum_ - lw_), kk, cum, lw)
    b_t = each(lambda kk_, a_, ig: kk_ * a_ * ig, kk, a_gate, inv_gam)
    k_t = each(lambda k2_, ig: k2_ * ig, k2, inv_gam)
    r_t = each(lambda r_, gm: r_ * gm, r, gam)

    s0b = each(lambda x: x.astype(BF16), s0)
    a_s0 = each(lambda x, s_: _nt(x.astype(BF16), s_), a_t, s0b)
    r_s0 = each(lambda x, s_: _nt(x.astype(BF16), s_), r_t, s0b)
    a2, r2, b2, k2h, v2 = (each(per_head, x) for x in (a_t, r_t, b_t, k_t, v))
    a_ab = each(lambda x, y_: jnp.where(strict, _nt(x, y_), 0.0), a2, b2)
    a_ak = each(lambda x, y_: jnp.where(strict, _nt(x, y_), 0.0), a2, k2h)
    a_rb = each(lambda x, y_: jnp.where(incl, _nt(x, y_), 0.0), r2, b2)
    a_rk = each(lambda x, y_: jnp.where(incl, _nt(x, y_), 0.0), r2, k2h)
    rhs = each(lambda as0, ak, v_: jnp.concatenate([jnp.where(hm, as0, 0.0) for hm in head_masks], axis=0)
               + _bdot(ak, v_), a_s0, a_ak, v2)
    m = a_ab
    inv = each(lambda x: eye + x, m)
    n_pow = 1
    while 2 * n_pow < c:
        m = each(lambda x: _bdot(x, x), m)
        inv = each(lambda p_, x: p_ + _bdot(p_, x), inv, m)
        n_pow *= 2
    u2 = each(_bdot, inv, rhs)
    y2 = each(lambda rb, u_, rk, v_: _bdot(rb, u_) + _bdot(rk, v_), a_rb, u2, a_rk, v2)
    y = each(lambda rs, y2_: rs + sum(y2_[h * c:(h + 1) * c, :] for h in range(nh)), r_s0, y2)
    uv = each(lambda u_, v_: jnp.concatenate([u_.astype(BF16), v_], axis=0), u2, v2)
    bkg = each(lambda b_, k_, ge: jnp.concatenate([per_head(b_ * ge), per_head(k_ * ge)], axis=0), b_t, k_t, gam_end)
    s_new = each(lambda s_, ge, uv_, bkg_: s_ * ge + _tn(uv_, bkg_), s0, gam_end, uv, bkg)

    inv_hd = 1.0 / hd
    mean = head_sum(y)
    yc = each(lambda y_, m_: y_ - m_ * inv_hd, y, mean)
    var = head_sum(each(lambda x: x * x, yc))
    yn = each(lambda yc_, var_, gg_, gb_: yc_ * lax.rsqrt(var_ * inv_hd + gn_eps) * gg_ + gb_, yc, var, gg, gb)
    rk_sum = head_sum(each(lambda r_, k2_, w_: r_ * k2_ * w_, r, k2, rk_w))
    out = each(lambda yn_, rk_, v_, g_: (yn_ + rk_ * v_) * g_, yn, rk_sum, v, g)
    return out, s_new


def _scan_kernel(r_ref, k_ref, v_ref, wl_ref, a_ref, g_ref, kk_ref, ka_ref, rk_ref, gg_ref, gb_ref, o_ref, state, *,
                 hd, gn_eps):
    @pl.when(pl.program_id(2) == 0)
    def _():
        state[...] = jnp.zeros_like(state)

    ng = state.shape[0]
    sls = [slice(p * LANES, (p + 1) * LANES) for p in range(ng)]
    tok3 = lambda ref: [ref[0, :, sl] for sl in sls]
    tok2 = lambda ref: [ref[:, sl] for sl in sls]
    out, s_new = _scan_groups(tok3(r_ref), tok3(k_ref), tok3(v_ref), tok2(wl_ref), tok2(a_ref), tok2(g_ref),
                              tok2(kk_ref), tok2(ka_ref), tok2(rk_ref), tok2(gg_ref), tok2(gb_ref),
                              [state[p] for p in range(ng)], hd=hd, gn_eps=gn_eps)
    for p in range(ng):
        state[p] = s_new[p]
        o_ref[:, sls[p]] = out[p].astype(o_ref.dtype)


def _rwkv_scan(rkv, wl, a, g, k_k, k_a, r_k, gn_g, gn_b, batch, hd, gn_eps, chunk=64, groups=16):
    _, n, d = rkv.shape
    t = n // batch
    nc = t // chunk
    groups = min(groups, d // LANES)
    w = groups * LANES
    tok = lambda j: pl.BlockSpec((1, chunk, w), functools.partial(lambda b, p, c, j: (j, b * nc + c, p), j=j))
    tok2 = pl.BlockSpec((chunk, w), lambda b, p, c: (b * nc + c, p))
    par = pl.BlockSpec((1, w), lambda b, p, c: (0, p))
    return pl.pallas_call(
        functools.partial(_scan_kernel, hd=hd, gn_eps=gn_eps),
        grid=(batch, d // w, nc),
        in_specs=[tok(0), tok(1), tok(2), tok2, tok2, tok2, par, par, par, par, par],
        out_specs=tok2,
        out_shape=jax.ShapeDtypeStruct((n, d), BF16),
        scratch_shapes=[pltpu.VMEM((groups, LANES, LANES), F32)],
        compiler_params=_cparams(("parallel", "parallel", "arbitrary")),
        name="rwkv_scan",
    )(rkv, rkv, rkv, wl, a, g, k_k.reshape(1, d), k_a.reshape(1, d), r_k.reshape(1, d), gn_g.reshape(1, d),
      gn_b.reshape(1, d))


def _pad_lora(w_in, w_out):
    r = w_in.shape[1]
    rp = -(-r // LANES) * LANES
    return (jnp.pad(w_in, ((0, 0), (0, rp - r))).astype(BF16), jnp.pad(w_out, ((0, rp - r), (0, 0))).astype(BF16))


def kernel(x, ev_w_in, ev_b_a, ev_w_s, ev_b_s, ev_g_v, ev_b_v, ev_b_f, ev_w_out, rw_mu, rw_w_rkv, rw_w0, rw_w1, rw_w2, rw_a0, rw_a1, rw_a2, rw_g1, rw_g2, rw_k_k, rw_k_a, rw_r_k, rw_gn_g, rw_gn_b, rw_w_o, ln_g, ln_b, w_router, b_router, w_gu, w_down):
    batch, t, d = x.shape
    depth = ln_g.shape[0]
    alpha = (2 * depth) ** 0.25
    h = x.reshape(batch * t, d)
    for layer in range(depth):
        i = layer // 2
        if layer % 2 == 0:
            aw = ev_g_v.shape[1]
            heads = ev_b_f.shape[1]
            q_col = 2 * aw
            bw = heads * LANES
            k_col, v_col, f_col = q_col + bw, q_col + 2 * bw, q_col + 3 * bw
            w_in = ev_w_in[i]
            col = jnp.arange(v_col)
            q_scale = jnp.where((col >= q_col) & (col < k_col), LANES ** -0.5 * LOG2E, 1.0)
            proj = _matmul(h, (w_in[:, :v_col] * q_scale).astype(BF16), v_col, BF16)
            vt = _proj_transposed(h, w_in[:, v_col:f_col].T.astype(BF16), FOX_BLOCK)
            c = _fgate(h, w_in[:, f_col:].T, ev_b_f[i], batch)
            y_a = _sgu(proj, ev_b_a[i], ev_w_s[i], ev_b_s[i], ev_g_v[i], ev_b_v[i])
            y_b = _fox_attention(proj, vt, c, batch, heads, q_col, FOX_BLOCK)
            h = _proj_ln([y_a, y_b], ev_w_out[i].astype(BF16), h, ln_g[layer, 0], ln_b[layer, 0], alpha)
        else:
            hd = rw_r_k.shape[2]
            mu = rw_mu[i]
            rkv = _rkv_proj(h, mu[:3], rw_w_rkv[i].astype(BF16), t)
            w1, w2 = _pad_lora(rw_w1[i], rw_w2[i])
            a1, a2 = _pad_lora(rw_a1[i], rw_a2[i])
            g1, g2 = _pad_lora(rw_g1[i], rw_g2[i])
            wl, a, g = _lora(h, mu[3:], rw_w0[i], w1, w2, rw_a0[i], a1, a2, g1, g2, t)
            y = _rwkv_scan(rkv, wl, a, g, rw_k_k[i], rw_k_a[i], rw_r_k[i].reshape(-1), rw_gn_g[i], rw_gn_b[i],
                           batch, hd, hd * 1e-5)
            h = _proj_ln([y], rw_w_o[i].astype(BF16), h, ln_g[layer, 0], ln_b[layer, 0], alpha)
        h = _moe_ln(h, w_router, b_router, w_gu[layer], w_down[layer], ln_g[layer, 1], ln_b[layer, 1], alpha)
    return h.reshape(batch, t, d)
```

```python
import functools

import jax
import jax.numpy as jnp
from jax import lax
from jax.experimental import pallas as pl
from jax.experimental.pallas import tpu as pltpu

F32 = jnp.float32
BF16 = jnp.bfloat16
HI = lax.Precision.HIGHEST

LN_EPS = 1e-5
N_GROUPS = 4
LANES = 128
FOX_BLOCK = 512
LOG2E = 1.4426950408889634
VMEM_LIMIT = 56 * 1024 * 1024


def _cparams(sem):
    return pltpu.CompilerParams(dimension_semantics=sem, vmem_limit_bytes=VMEM_LIMIT)


def _layer_norm(x, g, b, eps):
    mu = jnp.mean(x, -1, keepdims=True)
    xc = x - mu
    var = jnp.mean(xc * xc, -1, keepdims=True)
    return xc * lax.rsqrt(var + eps) * g + b


def _nt(a, b, **kw):
    return lax.dot_general(a, b, (((1,), (1,)), ((), ())), preferred_element_type=F32, **kw)


def _tn(a, b, **kw):
    return lax.dot_general(a, b, (((0,), (0,)), ((), ())), preferred_element_type=F32, **kw)


def _bdot(a, b):
    return jnp.dot(a.astype(BF16), b.astype(BF16), preferred_element_type=F32)


def _mm_kernel(a_ref, w_ref, o_ref, a_bf16):
    @pl.when(pl.program_id(1) == 0)
    def _():
        a_bf16[...] = a_ref[...].astype(BF16)

    o_ref[...] = jnp.dot(a_bf16[...], w_ref[...], preferred_element_type=F32).astype(o_ref.dtype)


def _matmul(a, w, n_cols, out_dtype, tm=1024, tn=512):
    m, k = a.shape
    tm = min(tm, m)
    tn = next(c for c in (tn, 256, LANES) if n_cols % c == 0)
    return pl.pallas_call(
        _mm_kernel,
        grid=(m // tm, n_cols // tn),
        in_specs=[pl.BlockSpec((tm, k), lambda i, j: (i, 0)), pl.BlockSpec((k, tn), lambda i, j: (0, j))],
        out_specs=pl.BlockSpec((tm, tn), lambda i, j: (i, j)),
        out_shape=jax.ShapeDtypeStruct((m, n_cols), out_dtype),
        scratch_shapes=[pltpu.VMEM((tm, k), BF16)],
        compiler_params=_cparams(("parallel", "arbitrary")),
        name="matmul",
    )(a, w)


def _proj_ln_kernel(*refs, n_in, alpha):
    a_refs, w_refs = refs[:n_in], refs[n_in:2 * n_in]
    res_ref, g_ref, b_ref, o_ref = refs[2 * n_in:]
    acc = alpha * res_ref[...]
    for a_ref, w_ref in zip(a_refs, w_refs):
        acc = acc + jnp.dot(a_ref[...], w_ref[...], preferred_element_type=F32)
    o_ref[...] = _layer_norm(acc, g_ref[...], b_ref[...], LN_EPS)


def _proj_ln(a_list, w, res, g, b, alpha, tm=256):
    m, d = res.shape
    n_in = len(a_list)
    kc = a_list[0].shape[1]
    in_specs = [pl.BlockSpec((tm, kc), lambda i: (i, 0)) for _ in a_list]
    in_specs += [pl.BlockSpec((kc, d), functools.partial(lambda i, r: (r, 0), r=r)) for r in range(n_in)]
    in_specs += [pl.BlockSpec((tm, d), lambda i: (i, 0)), pl.BlockSpec((1, d), lambda i: (0, 0)),
                 pl.BlockSpec((1, d), lambda i: (0, 0))]
    return pl.pallas_call(
        functools.partial(_proj_ln_kernel, n_in=n_in, alpha=alpha),
        grid=(m // tm,),
        in_specs=in_specs,
        out_specs=pl.BlockSpec((tm, d), lambda i: (i, 0)),
        out_shape=jax.ShapeDtypeStruct((m, d), F32),
        compiler_params=_cparams(("parallel",)),
        name="proj_ln",
    )(*a_list, *([w] * n_in), res, g.reshape(1, d), b.reshape(1, d))


def _fgate_kernel(x_ref, wf_ref, bf_ref, c_ref, carry):
    @pl.when(pl.program_id(1) == 0)
    def _():
        carry[...] = jnp.zeros_like(carry)

    tm = x_ref.shape[0]
    z = _nt(wf_ref[...], x_ref[...], precision=HI) + bf_ref[...]
    log_f = jnp.minimum(z, 0.0) - jnp.log1p(jnp.exp(-jnp.abs(z)))
    row = lax.broadcasted_iota(jnp.int32, (tm, tm), 0)
    col = lax.broadcasted_iota(jnp.int32, (tm, tm), 1)
    upper = jnp.where(row <= col, 1.0, 0.0).astype(F32)
    c = jnp.dot(log_f, upper, preferred_element_type=F32, precision=HI) + carry[...]
    c_ref[0] = c
    carry[...] = carry[...] + jnp.sum(log_f, axis=-1, keepdims=True)


def _fgate(x2d, wf_t, b_f, batch, tm=512):
    n, d = x2d.shape
    h = wf_t.shape[0]
    t = n // batch
    nt = t // tm
    return pl.pallas_call(
        _fgate_kernel,
        grid=(batch, nt),
        in_specs=[pl.BlockSpec((tm, d), lambda b, i: (b * nt + i, 0)), pl.BlockSpec((h, d), lambda b, i: (0, 0)),
                  pl.BlockSpec((h, 1), lambda b, i: (0, 0))],
        out_specs=pl.BlockSpec((1, h, tm), lambda b, i: (b, 0, i)),
        out_shape=jax.ShapeDtypeStruct((batch, h, t), F32),
        scratch_shapes=[pltpu.VMEM((h, 1), F32)],
        compiler_params=_cparams(("parallel", "arbitrary")),
        name="fgate",
    )(x2d, wf_t, b_f.reshape(h, 1))


def _gelu_tanh(x):
    return 0.5 * x * (1.0 + jnp.tanh(0.7978845608028654 * (x + 0.044715 * (x * x * x))))


def _sgu_kernel(z_ref, ba_ref, ws_ref, bs_ref, gv_ref, bv_ref, o_ref, *, chunk, groups):
    aw = o_ref.shape[1]
    gd = aw // groups
    z = _gelu_tanh(z_ref[...].astype(F32) + ba_ref[...])
    u = z[:, :aw]
    v = _layer_norm(z[:, aw:], gv_ref[...], bv_ref[...], LN_EPS).astype(BF16)
    row = lax.broadcasted_iota(jnp.int32, (chunk, chunk), 0)
    col = lax.broadcasted_iota(jnp.int32, (chunk, chunk), 1)
    causal = row >= col
    bs = bs_ref[...]
    for g in range(groups):
        w_g = jnp.where(causal, ws_ref[g], 0.0).astype(BF16)
        for c in range(z.shape[0] // chunk):
            rs = slice(c * chunk, (c + 1) * chunk)
            cs = slice(g * gd, (g + 1) * gd)
            s = jnp.dot(w_g, v[rs, cs], preferred_element_type=F32) + bs[:, g:g + 1]
            o_ref[rs, cs] = (u[rs, cs] * s).astype(o_ref.dtype)


def _sgu(proj, b_a, w_s, b_s, g_v, b_v, tm=512):
    n = proj.shape[0]
    groups, chunk, _ = w_s.shape
    aw = g_v.shape[0]
    return pl.pallas_call(
        functools.partial(_sgu_kernel, chunk=chunk, groups=groups),
        grid=(n // tm,),
        in_specs=[pl.BlockSpec((tm, 2 * aw), lambda i: (i, 0)), pl.BlockSpec((1, 2 * aw), lambda i: (0, 0)),
                  pl.BlockSpec((groups, chunk, chunk), lambda i: (0, 0, 0)),
                  pl.BlockSpec((chunk, groups), lambda i: (0, 0)), pl.BlockSpec((1, aw), lambda i: (0, 0)),
                  pl.BlockSpec((1, aw), lambda i: (0, 0))],
        out_specs=pl.BlockSpec((tm, aw), lambda i: (i, 0)),
        out_shape=jax.ShapeDtypeStruct((n, aw), BF16),
        compiler_params=_cparams(("parallel",)),
        name="sgu",
    )(proj, b_a.reshape(1, -1), w_s, b_s.T, g_v.reshape(1, aw), b_v.reshape(1, aw))


def _vt_kernel(x_ref, w_ref, o_ref):
    o_ref[0] = _nt(w_ref[...], x_ref[...].astype(BF16)).astype(o_ref.dtype)


def _proj_transposed(x, w_t, blk):
    n, d = x.shape
    rows = w_t.shape[0]
    return pl.pallas_call(
        _vt_kernel,
        grid=(n // blk,),
        in_specs=[pl.BlockSpec((blk, d), lambda i: (i, 0)), pl.BlockSpec((rows, d), lambda i: (0, 0))],
        out_specs=pl.BlockSpec((1, rows, blk), lambda i: (i, 0, 0)),
        out_shape=jax.ShapeDtypeStruct((n // blk, rows, blk), BF16),
        compiler_params=_cparams(("parallel",)),
        name="proj_transposed",
    )(x, w_t)


def _fox_kernel(q_ref, k_ref, vt_ref, cq_ref, ck_ref, o_ref, m_scr, l_scr, acc_scr, *, blk, nh):
    qi = pl.program_id(2)
    heads = range(nh)
    hs = [slice(h * LANES, (h + 1) * LANES) for h in heads]
    m_scr[...] = jnp.full_like(m_scr, -jnp.inf)
    l_scr[...] = jnp.zeros_like(l_scr)
    acc_scr[...] = jnp.zeros_like(acc_scr)
    q = [q_ref[:, hs[h]] for h in heads]
    cq = [cq_ref[0, h, pl.ds(qi, 1), :] * LOG2E for h in heads]

    def step(ki, masked):
        ks = pl.multiple_of(ki * blk, blk)
        t = [_nt(k_ref[pl.ds(ks, blk), hs[h]], q[h]) - ck_ref[0, h, pl.ds(ks, blk), :] * LOG2E for h in heads]
        if masked:
            row = lax.broadcasted_iota(jnp.int32, (blk, blk), 0)
            col = lax.broadcasted_iota(jnp.int32, (blk, blk), 1)
            t = [jnp.where(row <= col, x, -jnp.inf) for x in t]
        m_prev = [m_scr[h] for h in heads]
        m_new = [jnp.maximum(m_prev[h], cq[h] + jnp.max(t[h], axis=0, keepdims=True)) for h in heads]
        p = [jnp.exp2(t[h] - (m_new[h] - cq[h])) for h in heads]
        corr = [jnp.exp2(m_prev[h] - m_new[h]) for h in heads]
        for h in heads:
            l_scr[h] = corr[h] * l_scr[h] + jnp.sum(p[h], axis=0, keepdims=True)
            acc_scr[h] = corr[h] * acc_scr[h] + jnp.dot(vt_ref[ki, hs[h], :], p[h].astype(BF16),
                                                         preferred_element_type=F32)
            m_scr[h] = m_new[h]

    def body(ki, carry):
        step(ki, False)
        return carry

    lax.fori_loop(0, qi, body, 0)
    step(qi, True)
    for h in heads:
        o_ref[:, hs[h]] = jnp.transpose(acc_scr[h] / l_scr[h]).astype(o_ref.dtype)


def _fox_attention(proj, vt, c, batch, heads, q_col, blk, nh=4):
    n = proj.shape[0]
    t = n // batch
    nb = t // blk
    dh = LANES
    nh = min(nh, heads)
    w = nh * dh
    q0, k0 = q_col // w, (q_col + heads * dh) // w
    c_col = c.reshape(batch, heads, t, 1)
    c_row = c.reshape(batch, heads, nb, blk)
    return pl.pallas_call(
        functools.partial(_fox_kernel, blk=blk, nh=nh),
        grid=(batch, heads // nh, nb),
        in_specs=[pl.BlockSpec((blk, w), lambda b, h, i: (b * nb + i, q0 + h)),
                  pl.BlockSpec((t, w), lambda b, h, i: (b, k0 + h)),
                  pl.BlockSpec((nb, w, blk), lambda b, h, i: (b, h, 0)),
                  pl.BlockSpec((1, nh, nb, blk), lambda b, h, i: (b, h, 0, 0)),
                  pl.BlockSpec((1, nh, t, 1), lambda b, h, i: (b, h, 0, 0))],
        out_specs=pl.BlockSpec((blk, w), lambda b, h, i: (b * nb + i, h)),
        out_shape=jax.ShapeDtypeStruct((n, heads * dh), BF16),
        scratch_shapes=[pltpu.VMEM((nh, 1, blk), F32), pltpu.VMEM((nh, 1, blk), F32),
                        pltpu.VMEM((nh, dh, blk), F32)],
        compiler_params=_cparams(("parallel", "parallel", "arbitrary")),
        name="fox_attention",
    )(proj, proj, vt, c_row, c_col)


def _first_max(p, lane, valid):
    pm = jnp.where(valid, p, -2.0)
    m = jnp.max(pm, axis=-1, keepdims=True)
    idx = jnp.min(jnp.where(pm == m, lane, float(LANES)), axis=-1, keepdims=True)
    return m, idx


def _router_top2(x, wr_ref, br_ref, n_experts):
    rows = x.shape[0]
    per = n_experts // N_GROUPS
    lane = lax.broadcasted_iota(jnp.int32, (rows, LANES), 1).astype(F32)
    real = lane < n_experts
    x_hi = x.astype(BF16)
    x_lo = (x - x_hi.astype(F32)).astype(BF16)
    logits = (jnp.dot(x_hi, wr_ref[0], preferred_element_type=F32) + jnp.dot(x_lo, wr_ref[0], preferred_element_type=F32)
              + jnp.dot(x_hi, wr_ref[1], preferred_element_type=F32) + br_ref[...])
    logits = jnp.where(real, logits, -jnp.inf)
    e = jnp.exp(logits - jnp.max(logits, axis=-1, keepdims=True))
    probs = e / jnp.sum(e, axis=-1, keepdims=True)
    best_score = jnp.full((rows, 1), -1.0, F32)
    best_group = jnp.zeros((rows, 1), F32)
    for grp in range(N_GROUPS):
        in_g = (lane >= grp * per) & (lane < (grp + 1) * per)
        m1, i1 = _first_max(probs, lane, in_g)
        m2, _ = _first_max(probs, lane, in_g & (lane != i1))
        score = m1 + m2
        take = score > best_score
        best_score = jnp.where(take, score, best_score)
        best_group = jnp.where(take, float(grp), best_group)
    in_sel = (lane >= best_group * per) & (lane < (best_group + 1) * per)
    p1, i1 = _first_max(probs, lane, in_sel)
    p2, i2 = _first_max(probs, lane, in_sel & (lane != i1))
    tot = p1 + p2
    return lane, i1, i2, p1 / tot, p2 / tot


def _route_kernel(x_ref, wr_ref, br_ref, meta_ref, cnt_ref, carry, *, n_experts):
    @pl.when(pl.program_id(0) == 0)
    def _():
        carry[...] = jnp.zeros_like(carry)

    tm = x_ref.shape[0]
    lane, i1, i2, p1, p2 = _router_top2(x_ref[...], wr_ref, br_ref, n_experts)
    row = lax.broadcasted_iota(jnp.int32, (tm, tm), 0)
    col = lax.broadcasted_iota(jnp.int32, (tm, tm), 1)
    before = jnp.where(row > col, 1.0, 0.0).astype(BF16)
    onehot = jnp.where((lane == i1) | (lane == i2), 1.0, 0.0)
    seen = jnp.dot(before, onehot.astype(BF16), preferred_element_type=F32) + carry[...]
    r1 = jnp.sum(jnp.where(lane == i1, seen, 0.0), axis=-1, keepdims=True)
    r2 = jnp.sum(jnp.where(lane == i2, seen, 0.0), axis=-1, keepdims=True)
    meta = jnp.zeros((tm, LANES), F32)
    for j, val in enumerate((i1, i2, p1, p2, r1, r2)):
        meta = jnp.where(lane == j, val, meta)
    meta_ref[...] = meta
    carry[...] = carry[...] + jnp.sum(onehot, axis=0, keepdims=True)
    cnt_ref[...] = carry[...]


def _route(x, w_router, b_router, tm=512):
    n, d = x.shape
    n_experts = w_router.shape[1]
    wr = jnp.zeros((d, LANES), F32).at[:, :n_experts].set(w_router)
    wr_hi = wr.astype(BF16)
    wr = jnp.stack([wr_hi, (wr - wr_hi.astype(F32)).astype(BF16)])
    br = jnp.zeros((1, LANES), F32).at[0, :n_experts].set(b_router)
    return pl.pallas_call(
        functools.partial(_route_kernel, n_experts=n_experts),
        grid=(n // tm,),
        in_specs=[pl.BlockSpec((tm, d), lambda i: (i, 0)), pl.BlockSpec((2, d, LANES), lambda i: (0, 0, 0)),
                  pl.BlockSpec((1, LANES), lambda i: (0, 0))],
        out_specs=[pl.BlockSpec((tm, LANES), lambda i: (i, 0)), pl.BlockSpec((1, LANES), lambda i: (0, 0))],
        out_shape=[jax.ShapeDtypeStruct((n, LANES), F32), jax.ShapeDtypeStruct((1, LANES), F32)],
        scratch_shapes=[pltpu.VMEM((1, LANES), F32)],
        compiler_params=_cparams(("arbitrary",)),
        name="route",
    )(x, wr, br)


def _row_copy(src_ref, src_row, dst_ref, dst_row, sem):
    return pltpu.make_async_copy(src_ref.at[pl.ds(src_row, 1)], dst_ref.at[pl.ds(dst_row, 1)], sem)


def _dispatch_kernel(pos_ref, x_ref, _, xs_ref, sem):
    tm = x_ref.shape[0]

    def start(r, carry):
        for s in range(pos_ref.shape[0]):
            _row_copy(x_ref, r, xs_ref, pos_ref[s, r], sem).start()
        return carry

    def wait(r, carry):
        for s in range(pos_ref.shape[0]):
            _row_copy(x_ref, r, xs_ref, pos_ref[s, r], sem).wait()
        return carry

    lax.fori_loop(0, tm, start, 0, unroll=8)
    lax.fori_loop(0, tm, wait, 0, unroll=8)


def _dispatch(x, pos, n_rows, tm=256):
    n, d = x.shape
    return pl.pallas_call(
        _dispatch_kernel,
        grid=(n // tm,),
        in_specs=[pl.BlockSpec((pos.shape[0], tm), lambda i: (0, i), memory_space=pltpu.SMEM),
                  pl.BlockSpec((tm, d), lambda i: (i, 0)), pl.BlockSpec(memory_space=pl.ANY)],
        out_specs=pl.BlockSpec(memory_space=pl.ANY),
        out_shape=jax.ShapeDtypeStruct((n_rows, d), x.dtype),
        scratch_shapes=[pltpu.SemaphoreType.DMA],
        input_output_aliases={2: 0},
        compiler_params=_cparams(("arbitrary",)),
        name="moe_dispatch",
    )(pos, x, jnp.zeros((n_rows, d), x.dtype))


def _experts_kernel(te_ref, nx_ref, na_ref, xs_ref, wgu_hbm, wd_hbm, ys_ref, wgu_f, wd_f, wgu_b, wd_b, sems, *, layer):
    j = pl.program_id(0)
    live = j < na_ref[0]
    fresh = (j == 0) | (te_ref[j] != te_ref[jnp.maximum(j - 1, 0)])

    def fetch(e):
        return (pltpu.make_async_copy(wgu_hbm.at[layer, e], wgu_f, sems.at[0]),
                pltpu.make_async_copy(wd_hbm.at[layer, e], wd_f, sems.at[1]))

    @pl.when(live & (j == 0))
    def _():
        for cp in fetch(te_ref[0]):
            cp.start()

    @pl.when(live & fresh)
    def _():
        for cp in fetch(te_ref[j]):
            cp.wait()
        wgu_b[...] = wgu_f[...].astype(BF16)
        wd_b[...] = wd_f[...].astype(BF16)

    @pl.when(live & fresh & (nx_ref[j] >= 0))
    def _():
        for cp in fetch(nx_ref[j]):
            cp.start()

    @pl.when(live)
    def _():
        de = wd_b.shape[0]
        gu = jnp.dot(xs_ref[...].astype(BF16), wgu_b[...], preferred_element_type=F32)
        gpart, upart = gu[:, :de], gu[:, de:]
        h = (gpart / (1.0 + jnp.exp(-gpart))) * upart
        ys_ref[...] = jnp.dot(h.astype(BF16), wd_b[...], preferred_element_type=F32)

    @pl.when(jnp.logical_not(live))
    def _():
        ys_ref[...] = jnp.zeros_like(ys_ref)


def _experts(xs, w_gu, w_down, layer, tile_expert, next_expert, n_active, tm):
    p, d = xs.shape
    de = w_down.shape[2]
    live = lambda j, na: jnp.maximum(jnp.minimum(j, na[0] - 1), 0)
    grid_spec = pltpu.PrefetchScalarGridSpec(
        num_scalar_prefetch=3,
        grid=(p // tm,),
        in_specs=[pl.BlockSpec((tm, d), lambda j, te, nx, na: (live(j, na), 0)),
                  pl.BlockSpec(memory_space=pl.ANY), pl.BlockSpec(memory_space=pl.ANY)],
        out_specs=pl.BlockSpec((tm, d), lambda j, te, nx, na: (j, 0)),
        scratch_shapes=[pltpu.VMEM((d, 2 * de), F32), pltpu.VMEM((de, d), F32), pltpu.VMEM((d, 2 * de), BF16),
                        pltpu.VMEM((de, d), BF16), pltpu.SemaphoreType.DMA((2,))],
    )
    return pl.pallas_call(
        functools.partial(_experts_kernel, layer=layer),
        grid_spec=grid_spec,
        out_shape=jax.ShapeDtypeStruct((p, d), F32),
        compiler_params=_cparams(("arbitrary",)),
        name="moe_experts",
    )(tile_expert, next_expert, n_active, xs, w_gu, w_down)


def _combine_kernel(pos_ref, x_ref, gw_ref, ys_ref, g_ref, b_ref, o_ref, buf, sem, *, alpha):
    tm = x_ref.shape[0]
    n_slots = pos_ref.shape[0]

    def start(r, carry):
        for s in range(n_slots):
            _row_copy(ys_ref, pos_ref[s, r], buf.at[s], r, sem).start()
        return carry

    def wait(r, carry):
        for s in range(n_slots):
            _row_copy(ys_ref, pos_ref[s, r], buf.at[s], r, sem).wait()
        return carry

    lax.fori_loop(0, tm, start, 0, unroll=8)
    lax.fori_loop(0, tm, wait, 0, unroll=8)
    gw = gw_ref[...]
    y = alpha * x_ref[...]
    for s in range(n_slots):
        y = y + gw[:, s:s + 1] * buf[s]
    o_ref[...] = _layer_norm(y, g_ref[...], b_ref[...], LN_EPS)


def _combine_ln(x, ys, pos, gw, g, b, alpha, tm=256):
    n, d = x.shape
    n_slots = pos.shape[0]
    return pl.pallas_call(
        functools.partial(_combine_kernel, alpha=alpha),
        grid=(n // tm,),
        in_specs=[pl.BlockSpec((n_slots, tm), lambda i: (0, i), memory_space=pltpu.SMEM),
                  pl.BlockSpec((tm, d), lambda i: (i, 0)), pl.BlockSpec((tm, n_slots), lambda i: (i, 0)),
                  pl.BlockSpec(memory_space=pl.ANY), pl.BlockSpec((1, d), lambda i: (0, 0)),
                  pl.BlockSpec((1, d), lambda i: (0, 0))],
        out_specs=pl.BlockSpec((tm, d), lambda i: (i, 0)),
        out_shape=jax.ShapeDtypeStruct((n, d), F32),
        scratch_shapes=[pltpu.VMEM((n_slots, tm, d), F32), pltpu.SemaphoreType.DMA],
        compiler_params=_cparams(("arbitrary",)),
        name="moe_combine",
    )(pos, x, gw, ys, g.reshape(1, d), b.reshape(1, d))


def _moe_ln(x, w_router, b_router, w_gu, w_down, layer, g, b, alpha, tm=256):
    n, d = x.shape
    n_experts = w_down.shape[1]
    meta, counts = _route(x, w_router, b_router)
    eid = meta[:, 0:2].astype(jnp.int32)
    gw = meta[:, 2:4]
    rank = meta[:, 4:6].astype(jnp.int32)
    counts = counts[0, :n_experts].astype(jnp.int32)
    tiles = (counts + tm - 1) // tm
    tile_end = jnp.cumsum(tiles)
    row_off = (tile_end - tiles) * tm
    n_tiles = (eid.size + tm - 1) // tm + n_experts
    tile_expert = jnp.sum(jnp.arange(n_tiles)[:, None] >= tile_end[None, :], axis=1).astype(jnp.int32)
    tile_expert = jnp.minimum(tile_expert, n_experts - 1)
    n_active = tile_end[-1:].astype(jnp.int32)
    ids = jnp.arange(n_experts)
    later = (ids[None, :] > ids[:, None]) & (tiles[None, :] > 0)
    next_of = jnp.min(jnp.where(later, ids[None, :], n_experts), axis=1)
    next_of = jnp.where(next_of < n_experts, next_of, -1).astype(jnp.int32)
    next_expert = next_of[tile_expert]
    sel = eid[:, :, None] == jnp.arange(n_experts)[None, None, :]
    pos = (jnp.sum(jnp.where(sel, row_off[None, None, :], 0), axis=-1) + rank).T
    xs = _dispatch(x, pos, n_tiles * tm)
    ys = _experts(xs, w_gu, w_down, layer, tile_expert, next_expert, n_active, tm)
    return _combine_ln(x, ys, pos, gw, g, b, alpha)


def _token_shift(x_ref, prev_ref, first):
    x = x_ref[...]
    prev_row = jnp.where(first, 0.0, prev_ref[7:8, :])
    row = lax.broadcasted_iota(jnp.int32, x.shape, 0)
    return x, jnp.where(row == 0, prev_row, pltpu.roll(x, 1, 0))


def _rkv_kernel(x_ref, prev_ref, mu_ref, w_ref, o_ref, *, tiles_per_seq):
    first = pl.program_id(0) % tiles_per_seq == 0
    x, xp = _token_shift(x_ref, prev_ref, first)
    xm = x + (xp - x) * mu_ref[0]
    o_ref[0] = jnp.dot(xm.astype(BF16), w_ref[0], preferred_element_type=F32).astype(o_ref.dtype)


def _rkv_proj(x, mu3, w_rkv, t, tm=512):
    n, d = x.shape
    sub = tm // 8
    return pl.pallas_call(
        functools.partial(_rkv_kernel, tiles_per_seq=t // tm),
        grid=(n // tm, 3),
        in_specs=[pl.BlockSpec((tm, d), lambda i, j: (i, 0)),
                  pl.BlockSpec((8, d), lambda i, j: (jnp.maximum(i * sub - 1, 0), 0)),
                  pl.BlockSpec((1, 1, d), lambda i, j: (j, 0, 0)), pl.BlockSpec((1, d, d), lambda i, j: (j, 0, 0))],
        out_specs=pl.BlockSpec((1, tm, d), lambda i, j: (j, i, 0)),
        out_shape=jax.ShapeDtypeStruct((3, n, d), BF16),
        compiler_params=_cparams(("parallel", "arbitrary")),
        name="rkv_proj",
    )(x, x, mu3.reshape(3, 1, d), w_rkv)


def _lora_kernel(x_ref, prev_ref, mu_ref, w0_ref, w1_ref, w2_ref, a0_ref, a1_ref, a2_ref, g1_ref, g2_ref, wl_ref,
                 a_ref, g_ref, *, tiles_per_seq):
    first = pl.program_id(0) % tiles_per_seq == 0
    x, xp = _token_shift(x_ref, prev_ref, first)
    dx = xp - x

    def mm(a, w_ref):
        return jnp.dot(a.astype(BF16), w_ref[...], preferred_element_type=F32)

    zw = w0_ref[...] + mm(jnp.tanh(mm(x + dx * mu_ref[0:1, :], w1_ref)), w2_ref)
    wl_ref[...] = jnp.minimum(zw, 0.0) - jnp.log1p(jnp.exp(-jnp.abs(zw))) - 0.5
    za = a0_ref[...] + mm(mm(x + dx * mu_ref[1:2, :], a1_ref), a2_ref)
    a_ref[...] = (1.0 / (1.0 + jnp.exp(-za))).astype(a_ref.dtype)
    zg = mm(x + dx * mu_ref[2:3, :], g1_ref)
    g_ref[...] = mm(1.0 / (1.0 + jnp.exp(-zg)), g2_ref).astype(g_ref.dtype)


def _lora(x, mu3, w0, w1, w2, a0, a1, a2, g1, g2, t, tm=512):
    n, d = x.shape
    sub = tm // 8
    const = lambda shape: pl.BlockSpec(shape, lambda i: (0, 0))
    row = pl.BlockSpec((tm, d), lambda i: (i, 0))
    return pl.pallas_call(
        functools.partial(_lora_kernel, tiles_per_seq=t // tm),
        grid=(n // tm,),
        in_specs=[row, pl.BlockSpec((8, d), lambda i: (jnp.maximum(i * sub - 1, 0), 0)), const((3, d)),
                  const((1, d)), const(w1.shape), const(w2.shape), const((1, d)), const(a1.shape), const(a2.shape),
                  const(g1.shape), const(g2.shape)],
        out_specs=[row, row, row],
        out_shape=[jax.ShapeDtypeStruct((n, d), F32), jax.ShapeDtypeStruct((n, d), BF16),
                   jax.ShapeDtypeStruct((n, d), BF16)],
        compiler_params=_cparams(("parallel",)),
        name="lora",
    )(x, x, mu3, w0.reshape(1, d), w1, w2, a0.reshape(1, d), a1, a2, g1, g2)


def _scan_groups(r, k, v, wl, a_gate, g, kk_w, ka_w, rk_w, gg, gb, s0, *, hd, gn_eps):
    c = r[0].shape[0]
    nh = LANES // hd
    hc = nh * c
    each = lambda f, *xs: [f(*x) for x in zip(*xs)]
    lane = lax.broadcasted_iota(jnp.int32, (1, LANES), 1)
    head_masks = [(lane >= h * hd) & (lane < (h + 1) * hd) for h in range(nh)]
    lr = lax.broadcasted_iota(jnp.int32, (LANES, LANES), 0) // hd
    lc = lax.broadcasted_iota(jnp.int32, (LANES, LANES), 1) // hd
    same_head = jnp.where(lr == lc, 1.0, 0.0).astype(BF16)
    trow = lax.broadcasted_iota(jnp.int32, (c, c), 0)
    tcol = lax.broadcasted_iota(jnp.int32, (c, c), 1)
    lower = jnp.where(trow >= tcol, 1.0, 0.0).astype(BF16)
    prow = lax.broadcasted_iota(jnp.int32, (hc, hc), 0)
    pcol = lax.broadcasted_iota(jnp.int32, (hc, hc), 1)
    same_blk = (prow // c) == (pcol // c)
    strict = same_blk & (prow > pcol)
    incl = same_blk & (prow >= pcol)
    eye = jnp.where(prow == pcol, 1.0, 0.0).astype(F32)

    def split(x):
        hi = x.astype(BF16)
        return hi, (x - hi.astype(F32)).astype(BF16)

    def head_sum(xs):
        return [jnp.dot(x.astype(BF16), same_head, preferred_element_type=F32) for x in xs]

    def per_head(x):
        return jnp.concatenate([jnp.where(hm, x, 0.0) for hm in head_masks], axis=0).astype(BF16)

    kk = each(lambda k_, w_: k_ * w_, k, kk_w)
    kk_n = head_sum(each(lambda x: x * x, kk))
    kk = each(lambda x, n_: x / jnp.maximum(jnp.sqrt(n_), 1e-12), kk, kk_n)
    k2 = each(lambda k_, a_, w_: k_ * (1.0 + (a_ - 1.0) * w_), k, a_gate, ka_w)
    lw = each(lambda w_: -jnp.exp(w_), wl)
    lw_parts = each(split, lw)
    cum = [jnp.dot(lower, hi, preferred_element_type=F32) + jnp.dot(lower, lo, preferred_element_type=F32)
           for hi, lo in lw_parts]
    gam = each(jnp.exp, cum)
    inv_gam = each(lambda x: jnp.exp(-x), cum)
    gam_end = each(lambda x: x[c - 1:c, :], gam)
    a_t = each(lambda kk_, cum_, lw_: -kk_ * jnp.exp(cum_ - lw_), kk, cum, lw)
    b_t = each(lambda kk_, a_, ig: kk_ * a_ * ig, kk, a_gate, inv_gam)
    k_t = each(lambda k2_, ig: k2_ * ig, k2, inv_gam)
    r_t = each(lambda r_, gm: r_ * gm, r, gam)

    s0b = each(lambda x: x.astype(BF16), s0)
    a_s0 = each(lambda x, s_: _nt(x.astype(BF16), s_), a_t, s0b)
    r_s0 = each(lambda x, s_: _nt(x.astype(BF16), s_), r_t, s0b)
    a2, r2, b2, k2h, v2 = (each(per_head, x) for x in (a_t, r_t, b_t, k_t, v))
    a_ab = each(lambda x, y_: jnp.where(strict, _nt(x, y_), 0.0), a2, b2)
    a_ak = each(lambda x, y_: jnp.where(strict, _nt(x, y_), 0.0), a2, k2h)
    a_rb = each(lambda x, y_: jnp.where(incl, _nt(x, y_), 0.0), r2, b2)
    a_rk = each(lambda x, y_: jnp.where(incl, _nt(x, y_), 0.0), r2, k2h)
    rhs = each(lambda as0, ak, v_: jnp.concatenate([jnp.where(hm, as0, 0.0) for hm in head_masks], axis=0)
               + _bdot(ak, v_), a_s0, a_ak, v2)
    m = a_ab
    inv = each(lambda x: eye + x, m)
    n_pow = 1
    while 2 * n_pow < c:
        m = each(lambda x: _bdot(x, x), m)
        inv = each(lambda p_, x: p_ + _bdot(p_, x), inv, m)
        n_pow *= 2
    u2 = each(_bdot, inv, rhs)
    y2 = each(lambda rb, u_, rk, v_: _bdot(rb, u_) + _bdot(rk, v_), a_rb, u2, a_rk, v2)
    y = each(lambda rs, y2_: rs + sum(y2_[h * c:(h + 1) * c, :] for h in range(nh)), r_s0, y2)
    uv = each(lambda u_, v_: jnp.concatenate([u_.astype(BF16), v_], axis=0), u2, v2)
    bkg = each(lambda b_, k_, ge: jnp.concatenate([per_head(b_ * ge), per_head(k_ * ge)], axis=0), b_t, k_t, gam_end)
    s_new = each(lambda s_, ge, uv_, bkg_: s_ * ge + _tn(uv_, bkg_), s0, gam_end, uv, bkg)

    inv_hd = 1.0 / hd
    mean = head_sum(y)
    yc = each(lambda y_, m_: y_ - m_ * inv_hd, y, mean)
    var = head_sum(each(lambda x: x * x, yc))
    yn = each(lambda yc_, var_, gg_, gb_: yc_ * lax.rsqrt(var_ * inv_hd + gn_eps) * gg_ + gb_, yc, var, gg, gb)
    rk_sum = head_sum(each(lambda r_, k2_, w_: r_ * k2_ * w_, r, k2, rk_w))
    out = each(lambda yn_, rk_, v_, g_: (yn_ + rk_ * v_) * g_, yn, rk_sum, v, g)
    return out, s_new


def _scan_kernel(r_ref, k_ref, v_ref, wl_ref, a_ref, g_ref, kk_ref, ka_ref, rk_ref, gg_ref, gb_ref, o_ref, state, *,
                 hd, gn_eps):
    @pl.when(pl.program_id(2) == 0)
    def _():
        state[...] = jnp.zeros_like(state)

    ng = state.shape[0]
    sls = [slice(p * LANES, (p + 1) * LANES) for p in range(ng)]
    tok3 = lambda ref: [ref[0, :, sl].astype(F32) for sl in sls]
    tok2 = lambda ref: [ref[:, sl].astype(F32) for sl in sls]
    out, s_new = _scan_groups(tok3(r_ref), tok3(k_ref), tok3(v_ref), tok2(wl_ref), tok2(a_ref), tok2(g_ref),
                              tok2(kk_ref), tok2(ka_ref), tok2(rk_ref), tok2(gg_ref), tok2(gb_ref),
                              [state[p] for p in range(ng)], hd=hd, gn_eps=gn_eps)
    for p in range(ng):
        state[p] = s_new[p]
        o_ref[:, sls[p]] = out[p].astype(o_ref.dtype)


def _rwkv_scan(rkv, wl, a, g, k_k, k_a, r_k, gn_g, gn_b, batch, hd, gn_eps, chunk=64, groups=16):
    _, n, d = rkv.shape
    t = n // batch
    nc = t // chunk
    groups = min(groups, d // LANES)
    w = groups * LANES
    tok = lambda j: pl.BlockSpec((1, chunk, w), functools.partial(lambda b, p, c, j: (j, b * nc + c, p), j=j))
    tok2 = pl.BlockSpec((chunk, w), lambda b, p, c: (b * nc + c, p))
    par = pl.BlockSpec((1, w), lambda b, p, c: (0, p))
    return pl.pallas_call(
        functools.partial(_scan_kernel, hd=hd, gn_eps=gn_eps),
        grid=(batch, d // w, nc),
        in_specs=[tok(0), tok(1), tok(2), tok2, tok2, tok2, par, par, par, par, par],
        out_specs=tok2,
        out_shape=jax.ShapeDtypeStruct((n, d), BF16),
        scratch_shapes=[pltpu.VMEM((groups, LANES, LANES), F32)],
        compiler_params=_cparams(("parallel", "parallel", "arbitrary")),
        name="rwkv_scan",
    )(rkv, rkv, rkv, wl, a, g, k_k.reshape(1, d), k_a.reshape(1, d), r_k.reshape(1, d), gn_g.reshape(1, d),
      gn_b.reshape(1, d))


def _pad_lora(w_in, w_out):
    r = w_in.shape[1]
    rp = -(-r // LANES) * LANES
    return (jnp.pad(w_in, ((0, 0), (0, rp - r))).astype(BF16), jnp.pad(w_out, ((0, rp - r), (0, 0))).astype(BF16))


def kernel(x, ev_w_in, ev_b_a, ev_w_s, ev_b_s, ev_g_v, ev_b_v, ev_b_f, ev_w_out, rw_mu, rw_w_rkv, rw_w0, rw_w1, rw_w2, rw_a0, rw_a1, rw_a2, rw_g1, rw_g2, rw_k_k, rw_k_a, rw_r_k, rw_gn_g, rw_gn_b, rw_w_o, ln_g, ln_b, w_router, b_router, w_gu, w_down):
    batch, t, d = x.shape
    depth = ln_g.shape[0]
    alpha = (2 * depth) ** 0.25
    h = x.reshape(batch * t, d)
    for layer in range(depth):
        i = layer // 2
        if layer % 2 == 0:
            aw = ev_g_v.shape[1]
            heads = ev_b_f.shape[1]
            q_col = 2 * aw
            bw = heads * LANES
            k_col, v_col, f_col = q_col + bw, q_col + 2 * bw, q_col + 3 * bw
            w_in = ev_w_in[i]
            col = jnp.arange(v_col)
            q_scale = jnp.where((col >= q_col) & (col < k_col), LANES ** -0.5 * LOG2E, 1.0)
            proj = _matmul(h, (w_in[:, :v_col] * q_scale).astype(BF16), v_col, BF16)
            vt = _proj_transposed(h, w_in[:, v_col:f_col].T.astype(BF16), FOX_BLOCK)
            c = _fgate(h, w_in[:, f_col:].T, ev_b_f[i], batch)
            y_a = _sgu(proj, ev_b_a[i], ev_w_s[i], ev_b_s[i], ev_g_v[i], ev_b_v[i])
            y_b = _fox_attention(proj, vt, c, batch, heads, q_col, FOX_BLOCK)
            h = _proj_ln([y_a, y_b], ev_w_out[i].astype(BF16), h, ln_g[layer, 0], ln_b[layer, 0], alpha)
        else:
            hd = rw_r_k.shape[2]
            mu = rw_mu[i]
            rkv = _rkv_proj(h, mu[:3], rw_w_rkv[i].astype(BF16), t)
            w1, w2 = _pad_lora(rw_w1[i], rw_w2[i])
            a1, a2 = _pad_lora(rw_a1[i], rw_a2[i])
            g1, g2 = _pad_lora(rw_g1[i], rw_g2[i])
            wl, a, g = _lora(h, mu[3:], rw_w0[i], w1, w2, rw_a0[i], a1, a2, g1, g2, t)
            y = _rwkv_scan(rkv, wl, a, g, rw_k_k[i], rw_k_a[i], rw_r_k[i].reshape(-1), rw_gn_g[i], rw_gn_b[i],
                           batch, hd, hd * 1e-5)
            h = _proj_ln([y], rw_w_o[i].astype(BF16), h, ln_g[layer, 0], ln_b[layer, 0], alpha)
        h = _moe_ln(h, w_router, b_router, w_gu, w_down, layer, ln_g[layer, 1], ln_b[layer, 1], alpha)
    return h.reshape(batch, t, d)
```

```python
import functools

import jax
import jax.numpy as jnp
from jax import lax
from jax.experimental import pallas as pl
from jax.experimental.pallas import tpu as pltpu

F32 = jnp.float32
BF16 = jnp.bfloat16
HI = lax.Precision.HIGHEST

LN_EPS = 1e-5
N_GROUPS = 4
LANES = 128
FOX_BLOCK = 512
LOG2E = 1.4426950408889634
HALF_DECAY = 0.6065306597126334
VMEM_LIMIT = 56 * 1024 * 1024


def _cparams(sem):
    return pltpu.CompilerParams(dimension_semantics=sem, vmem_limit_bytes=VMEM_LIMIT)


def _layer_norm(x, g, b, eps):
    mu = jnp.mean(x, -1, keepdims=True)
    xc = x - mu
    var = jnp.mean(xc * xc, -1, keepdims=True)
    return xc * lax.rsqrt(var + eps) * g + b


def _nt(a, b, **kw):
    return lax.dot_general(a, b, (((1,), (1,)), ((), ())), preferred_element_type=F32, **kw)


def _tn(a, b, **kw):
    return lax.dot_general(a, b, (((0,), (0,)), ((), ())), preferred_element_type=F32, **kw)


def _bdot(a, b):
    return jnp.dot(a.astype(BF16), b.astype(BF16), preferred_element_type=F32)


def _mm_kernel(a_ref, w_ref, o_ref, a_bf16):
    @pl.when(pl.program_id(1) == 0)
    def _():
        a_bf16[...] = a_ref[...].astype(BF16)

    o_ref[...] = jnp.dot(a_bf16[...], w_ref[...], preferred_element_type=F32).astype(o_ref.dtype)


def _matmul(a, w, n_cols, out_dtype, tm=1024, tn=512):
    m, k = a.shape
    tm = min(tm, m)
    tn = next(c for c in (tn, 256, LANES) if n_cols % c == 0)
    return pl.pallas_call(
        _mm_kernel,
        grid=(m // tm, n_cols // tn),
        in_specs=[pl.BlockSpec((tm, k), lambda i, j: (i, 0)), pl.BlockSpec((k, tn), lambda i, j: (0, j))],
        out_specs=pl.BlockSpec((tm, tn), lambda i, j: (i, j)),
        out_shape=jax.ShapeDtypeStruct((m, n_cols), out_dtype),
        scratch_shapes=[pltpu.VMEM((tm, k), BF16)],
        compiler_params=_cparams(("parallel", "arbitrary")),
        name="matmul",
    )(a, w)


def _proj_ln_kernel(*refs, n_in, alpha):
    a_refs, w_refs = refs[:n_in], refs[n_in:2 * n_in]
    res_ref, g_ref, b_ref, o_ref = refs[2 * n_in:]
    acc = alpha * res_ref[...]
    for a_ref, w_ref in zip(a_refs, w_refs):
        acc = acc + jnp.dot(a_ref[...], w_ref[...], preferred_element_type=F32)
    o_ref[...] = _layer_norm(acc, g_ref[...], b_ref[...], LN_EPS)


def _proj_ln(a_list, w, res, g, b, alpha, tm=256):
    m, d = res.shape
    n_in = len(a_list)
    kc = a_list[0].shape[1]
    in_specs = [pl.BlockSpec((tm, kc), lambda i: (i, 0)) for _ in a_list]
    in_specs += [pl.BlockSpec((kc, d), functools.partial(lambda i, r: (r, 0), r=r)) for r in range(n_in)]
    in_specs += [pl.BlockSpec((tm, d), lambda i: (i, 0)), pl.BlockSpec((1, d), lambda i: (0, 0)),
                 pl.BlockSpec((1, d), lambda i: (0, 0))]
    return pl.pallas_call(
        functools.partial(_proj_ln_kernel, n_in=n_in, alpha=alpha),
        grid=(m // tm,),
        in_specs=in_specs,
        out_specs=pl.BlockSpec((tm, d), lambda i: (i, 0)),
        out_shape=jax.ShapeDtypeStruct((m, d), F32),
        compiler_params=_cparams(("parallel",)),
        name="proj_ln",
    )(*a_list, *([w] * n_in), res, g.reshape(1, d), b.reshape(1, d))


def _fgate_kernel(z_ref, bf_ref, c_ref, carry):
    @pl.when(pl.program_id(1) == 0)
    def _():
        carry[...] = jnp.zeros_like(carry)

    tm = z_ref.shape[2]
    z = z_ref[0] + bf_ref[...]
    log_f = jnp.minimum(z, 0.0) - jnp.log1p(jnp.exp(-jnp.abs(z)))
    row = lax.broadcasted_iota(jnp.int32, (tm, tm), 0)
    col = lax.broadcasted_iota(jnp.int32, (tm, tm), 1)
    upper = jnp.where(row <= col, 1.0, 0.0).astype(F32)
    c = jnp.dot(log_f, upper, preferred_element_type=F32, precision=HI) + carry[...]
    c_ref[0] = c
    carry[...] = carry[...] + jnp.sum(log_f, axis=-1, keepdims=True)


def _fgate(f_logit, b_f, batch):
    tiles, h, tm = f_logit.shape
    nt = tiles // batch
    return pl.pallas_call(
        _fgate_kernel,
        grid=(batch, nt),
        in_specs=[pl.BlockSpec((1, h, tm), lambda b, i: (b * nt + i, 0, 0)), pl.BlockSpec((h, 1), lambda b, i: (0, 0))],
        out_specs=pl.BlockSpec((1, h, tm), lambda b, i: (b, 0, i)),
        out_shape=jax.ShapeDtypeStruct((batch, h, nt * tm), F32),
        scratch_shapes=[pltpu.VMEM((h, 1), F32)],
        compiler_params=_cparams(("parallel", "arbitrary")),
        name="fgate",
    )(f_logit, b_f.reshape(h, 1))


def _gelu_tanh(x):
    return 0.5 * x * (1.0 + jnp.tanh(0.7978845608028654 * (x + 0.044715 * (x * x * x))))


def _sgu_kernel(z_ref, ba_ref, ws_ref, bs_ref, gv_ref, bv_ref, o_ref, *, chunk, groups):
    aw = o_ref.shape[1]
    gd = aw // groups
    z = _gelu_tanh(z_ref[...].astype(F32) + ba_ref[...])
    u = z[:, :aw]
    v = _layer_norm(z[:, aw:], gv_ref[...], bv_ref[...], LN_EPS).astype(BF16)
    row = lax.broadcasted_iota(jnp.int32, (chunk, chunk), 0)
    col = lax.broadcasted_iota(jnp.int32, (chunk, chunk), 1)
    causal = row >= col
    bs = bs_ref[...]
    for g in range(groups):
        w_g = jnp.where(causal, ws_ref[g], 0.0).astype(BF16)
        for c in range(z.shape[0] // chunk):
            rs = slice(c * chunk, (c + 1) * chunk)
            cs = slice(g * gd, (g + 1) * gd)
            s = jnp.dot(w_g, v[rs, cs], preferred_element_type=F32) + bs[:, g:g + 1]
            o_ref[rs, cs] = (u[rs, cs] * s).astype(o_ref.dtype)


def _sgu(proj, b_a, w_s, b_s, g_v, b_v, tm=512):
    n = proj.shape[0]
    groups, chunk, _ = w_s.shape
    aw = g_v.shape[0]
    return pl.pallas_call(
        functools.partial(_sgu_kernel, chunk=chunk, groups=groups),
        grid=(n // tm,),
        in_specs=[pl.BlockSpec((tm, 2 * aw), lambda i: (i, 0)), pl.BlockSpec((1, 2 * aw), lambda i: (0, 0)),
                  pl.BlockSpec((groups, chunk, chunk), lambda i: (0, 0, 0)),
                  pl.BlockSpec((chunk, groups), lambda i: (0, 0)), pl.BlockSpec((1, aw), lambda i: (0, 0)),
                  pl.BlockSpec((1, aw), lambda i: (0, 0))],
        out_specs=pl.BlockSpec((tm, aw), lambda i: (i, 0)),
        out_shape=jax.ShapeDtypeStruct((n, aw), BF16),
        compiler_params=_cparams(("parallel",)),
        name="sgu",
    )(proj, b_a.reshape(1, -1), w_s, b_s.T, g_v.reshape(1, aw), b_v.reshape(1, aw))


def _vt_kernel(x_ref, w_ref, wf_ref, o_ref, f_ref):
    x = x_ref[...]
    x_hi = x.astype(BF16)
    x_lo = (x - x_hi.astype(F32)).astype(BF16)
    o_ref[0] = _nt(w_ref[...], x_hi).astype(o_ref.dtype)
    f_ref[0] = _nt(wf_ref[0], x_hi) + _nt(wf_ref[0], x_lo) + _nt(wf_ref[1], x_hi)


def _proj_transposed(x, w_t, wf_t, blk):
    n, d = x.shape
    rows, rows_f = w_t.shape[0], wf_t.shape[0]
    wf_hi = wf_t.astype(BF16)
    wf = jnp.stack([wf_hi, (wf_t - wf_hi.astype(F32)).astype(BF16)])
    return pl.pallas_call(
        _vt_kernel,
        grid=(n // blk,),
        in_specs=[pl.BlockSpec((blk, d), lambda i: (i, 0)), pl.BlockSpec((rows, d), lambda i: (0, 0)),
                  pl.BlockSpec((2, rows_f, d), lambda i: (0, 0, 0))],
        out_specs=[pl.BlockSpec((1, rows, blk), lambda i: (i, 0, 0)),
                   pl.BlockSpec((1, rows_f, blk), lambda i: (i, 0, 0))],
        out_shape=[jax.ShapeDtypeStruct((n // blk, rows, blk), BF16),
                   jax.ShapeDtypeStruct((n // blk, rows_f, blk), F32)],
        compiler_params=_cparams(("parallel",)),
        name="proj_transposed",
    )(x, w_t, wf)


def _fox_kernel(q_ref, k_ref, vt_ref, cq_ref, ck_ref, o_ref, m_scr, l_scr, acc_scr, *, blk, nh):
    qi = pl.program_id(2)
    heads = range(nh)
    hs = [slice(h * LANES, (h + 1) * LANES) for h in heads]
    m_scr[...] = jnp.full_like(m_scr, -jnp.inf)
    l_scr[...] = jnp.zeros_like(l_scr)
    acc_scr[...] = jnp.zeros_like(acc_scr)
    q = [q_ref[:, hs[h]] for h in heads]
    cq = [cq_ref[0, h, pl.ds(qi, 1), :] * LOG2E for h in heads]

    def step(ki, masked):
        ks = pl.multiple_of(ki * blk, blk)
        t = [_nt(k_ref[pl.ds(ks, blk), hs[h]], q[h]) - ck_ref[0, h, pl.ds(ks, blk), :] * LOG2E for h in heads]
        if masked:
            row = lax.broadcasted_iota(jnp.int32, (blk, blk), 0)
            col = lax.broadcasted_iota(jnp.int32, (blk, blk), 1)
            t = [jnp.where(row <= col, x, -jnp.inf) for x in t]
        m_prev = [m_scr[h] for h in heads]
        m_new = [jnp.maximum(m_prev[h], cq[h] + jnp.max(t[h], axis=0, keepdims=True)) for h in heads]
        p = [jnp.exp2(t[h] - (m_new[h] - cq[h])) for h in heads]
        corr = [jnp.exp2(m_prev[h] - m_new[h]) for h in heads]
        for h in heads:
            l_scr[h] = corr[h] * l_scr[h] + jnp.sum(p[h], axis=0, keepdims=True)
            acc_scr[h] = corr[h] * acc_scr[h] + jnp.dot(vt_ref[ki, hs[h], :], p[h].astype(BF16),
                                                         preferred_element_type=F32)
            m_scr[h] = m_new[h]

    def body(ki, carry):
        step(ki, False)
        return carry

    lax.fori_loop(0, qi, body, 0)
    step(qi, True)
    for h in heads:
        o_ref[:, hs[h]] = jnp.transpose(acc_scr[h] / l_scr[h]).astype(o_ref.dtype)


def _fox_attention(proj, vt, c, batch, heads, q_col, blk, nh=4):
    n = proj.shape[0]
    t = n // batch
    nb = t // blk
    dh = LANES
    nh = min(nh, heads)
    w = nh * dh
    q0, k0 = q_col // w, (q_col + heads * dh) // w
    c_col = c.reshape(batch, heads, t, 1)
    c_row = c.reshape(batch, heads, nb, blk)
    return pl.pallas_call(
        functools.partial(_fox_kernel, blk=blk, nh=nh),
        grid=(batch, heads // nh, nb),
        in_specs=[pl.BlockSpec((blk, w), lambda b, h, i: (b * nb + i, q0 + h)),
                  pl.BlockSpec((t, w), lambda b, h, i: (b, k0 + h)),
                  pl.BlockSpec((nb, w, blk), lambda b, h, i: (b, h, 0)),
                  pl.BlockSpec((1, nh, nb, blk), lambda b, h, i: (b, h, 0, 0)),
                  pl.BlockSpec((1, nh, t, 1), lambda b, h, i: (b, h, 0, 0))],
        out_specs=pl.BlockSpec((blk, w), lambda b, h, i: (b * nb + i, h)),
        out_shape=jax.ShapeDtypeStruct((n, heads * dh), BF16),
        scratch_shapes=[pltpu.VMEM((nh, 1, blk), F32), pltpu.VMEM((nh, 1, blk), F32),
                        pltpu.VMEM((nh, dh, blk), F32)],
        compiler_params=_cparams(("parallel", "parallel", "arbitrary")),
        name="fox_attention",
    )(proj, proj, vt, c_row, c_col)


def _first_max(p, lane, valid):
    pm = jnp.where(valid, p, -2.0)
    m = jnp.max(pm, axis=-1, keepdims=True)
    idx = jnp.min(jnp.where(pm == m, lane, float(LANES)), axis=-1, keepdims=True)
    return m, idx


def _router_top2(x, wr_ref, br_ref, n_experts):
    rows = x.shape[0]
    per = n_experts // N_GROUPS
    lane = lax.broadcasted_iota(jnp.int32, (rows, LANES), 1).astype(F32)
    real = lane < n_experts
    x_hi = x.astype(BF16)
    x_lo = (x - x_hi.astype(F32)).astype(BF16)
    logits = (jnp.dot(x_hi, wr_ref[0], preferred_element_type=F32) + jnp.dot(x_lo, wr_ref[0], preferred_element_type=F32)
              + jnp.dot(x_hi, wr_ref[1], preferred_element_type=F32) + br_ref[...])
    logits = jnp.where(real, logits, -jnp.inf)
    e = jnp.exp(logits - jnp.max(logits, axis=-1, keepdims=True))
    probs = e / jnp.sum(e, axis=-1, keepdims=True)
    best_score = jnp.full((rows, 1), -1.0, F32)
    best_group = jnp.zeros((rows, 1), F32)
    for grp in range(N_GROUPS):
        in_g = (lane >= grp * per) & (lane < (grp + 1) * per)
        m1, i1 = _first_max(probs, lane, in_g)
        m2, _ = _first_max(probs, lane, in_g & (lane != i1))
        score = m1 + m2
        take = score > best_score
        best_score = jnp.where(take, score, best_score)
        best_group = jnp.where(take, float(grp), best_group)
    in_sel = (lane >= best_group * per) & (lane < (best_group + 1) * per)
    p1, i1 = _first_max(probs, lane, in_sel)
    p2, i2 = _first_max(probs, lane, in_sel & (lane != i1))
    tot = p1 + p2
    return lane, i1, i2, p1 / tot, p2 / tot


def _route_kernel(x_ref, wr_ref, br_ref, meta_ref, cnt_ref, carry, *, n_experts):
    @pl.when(pl.program_id(0) == 0)
    def _():
        carry[...] = jnp.zeros_like(carry)

    tm = x_ref.shape[0]
    lane, i1, i2, p1, p2 = _router_top2(x_ref[...], wr_ref, br_ref, n_experts)
    row = lax.broadcasted_iota(jnp.int32, (tm, tm), 0)
    col = lax.broadcasted_iota(jnp.int32, (tm, tm), 1)
    before = jnp.where(row > col, 1.0, 0.0).astype(BF16)
    onehot = jnp.where((lane == i1) | (lane == i2), 1.0, 0.0)
    seen = jnp.dot(before, onehot.astype(BF16), preferred_element_type=F32) + carry[...]
    r1 = jnp.sum(jnp.where(lane == i1, seen, 0.0), axis=-1, keepdims=True)
    r2 = jnp.sum(jnp.where(lane == i2, seen, 0.0), axis=-1, keepdims=True)
    meta = jnp.zeros((tm, LANES), F32)
    for j, val in enumerate((i1, i2, p1, p2, r1, r2)):
        meta = jnp.where(lane == j, val, meta)
    meta_ref[...] = meta
    carry[...] = carry[...] + jnp.sum(onehot, axis=0, keepdims=True)
    cnt_ref[...] = carry[...]


def _route(x, w_router, b_router, tm=512):
    n, d = x.shape
    n_experts = w_router.shape[1]
    wr = jnp.zeros((d, LANES), F32).at[:, :n_experts].set(w_router)
    wr_hi = wr.astype(BF16)
    wr = jnp.stack([wr_hi, (wr - wr_hi.astype(F32)).astype(BF16)])
    br = jnp.zeros((1, LANES), F32).at[0, :n_experts].set(b_router)
    return pl.pallas_call(
        functools.partial(_route_kernel, n_experts=n_experts),
        grid=(n // tm,),
        in_specs=[pl.BlockSpec((tm, d), lambda i: (i, 0)), pl.BlockSpec((2, d, LANES), lambda i: (0, 0, 0)),
                  pl.BlockSpec((1, LANES), lambda i: (0, 0))],
        out_specs=[pl.BlockSpec((tm, LANES), lambda i: (i, 0)), pl.BlockSpec((1, LANES), lambda i: (0, 0))],
        out_shape=[jax.ShapeDtypeStruct((n, LANES), F32), jax.ShapeDtypeStruct((1, LANES), F32)],
        scratch_shapes=[pltpu.VMEM((1, LANES), F32)],
        compiler_params=_cparams(("arbitrary",)),
        name="route",
    )(x, wr, br)


def _row_copy(src_ref, src_row, dst_ref, dst_row, sem):
    return pltpu.make_async_copy(src_ref.at[pl.ds(src_row, 1)], dst_ref.at[pl.ds(dst_row, 1)], sem)


def _dispatch_kernel(pos_ref, x_ref, _, xs_ref, sem):
    tm = x_ref.shape[0]

    def start(r, carry):
        for s in range(pos_ref.shape[0]):
            _row_copy(x_ref, r, xs_ref, pos_ref[s, r], sem).start()
        return carry

    def wait(r, carry):
        for s in range(pos_ref.shape[0]):
            _row_copy(x_ref, r, xs_ref, pos_ref[s, r], sem).wait()
        return carry

    lax.fori_loop(0, tm, start, 0, unroll=8)
    lax.fori_loop(0, tm, wait, 0, unroll=8)


def _dispatch(x, pos, n_rows, tm=256):
    n, d = x.shape
    return pl.pallas_call(
        _dispatch_kernel,
        grid=(n // tm,),
        in_specs=[pl.BlockSpec((pos.shape[0], tm), lambda i: (0, i), memory_space=pltpu.SMEM),
                  pl.BlockSpec((tm, d), lambda i: (i, 0)), pl.BlockSpec(memory_space=pl.ANY)],
        out_specs=pl.BlockSpec(memory_space=pl.ANY),
        out_shape=jax.ShapeDtypeStruct((n_rows, d), x.dtype),
        scratch_shapes=[pltpu.SemaphoreType.DMA],
        input_output_aliases={2: 0},
        compiler_params=_cparams(("arbitrary",)),
        name="moe_dispatch",
    )(pos, x, jnp.zeros((n_rows, d), x.dtype))


def _experts_kernel(te_ref, nx_ref, na_ref, xs_ref, wgu_hbm, wd_hbm, ys_ref, wgu_f, wd_f, wgu_b, wd_b, sems, *, layer):
    j = pl.program_id(0)
    live = j < na_ref[0]
    fresh = (j == 0) | (te_ref[j] != te_ref[jnp.maximum(j - 1, 0)])

    def fetch(e):
        return (pltpu.make_async_copy(wgu_hbm.at[layer, e], wgu_f, sems.at[0]),
                pltpu.make_async_copy(wd_hbm.at[layer, e], wd_f, sems.at[1]))

    @pl.when(live & (j == 0))
    def _():
        for cp in fetch(te_ref[0]):
            cp.start()

    @pl.when(live & fresh)
    def _():
        for cp in fetch(te_ref[j]):
            cp.wait()
        wgu_b[...] = wgu_f[...].astype(BF16)
        wd_b[...] = wd_f[...].astype(BF16)

    @pl.when(live & fresh & (nx_ref[j] >= 0))
    def _():
        for cp in fetch(nx_ref[j]):
            cp.start()

    @pl.when(live)
    def _():
        de = wd_b.shape[0]
        gu = jnp.dot(xs_ref[...].astype(BF16), wgu_b[...], preferred_element_type=F32)
        gpart, upart = gu[:, :de], gu[:, de:]
        h = (gpart / (1.0 + jnp.exp(-gpart))) * upart
        ys_ref[...] = jnp.dot(h.astype(BF16), wd_b[...], preferred_element_type=F32)

    @pl.when(jnp.logical_not(live))
    def _():
        ys_ref[...] = jnp.zeros_like(ys_ref)


def _experts(xs, w_gu, w_down, layer, tile_expert, next_expert, n_active, tm):
    p, d = xs.shape
    de = w_down.shape[2]
    live = lambda j, na: jnp.maximum(jnp.minimum(j, na[0] - 1), 0)
    grid_spec = pltpu.PrefetchScalarGridSpec(
        num_scalar_prefetch=3,
        grid=(p // tm,),
        in_specs=[pl.BlockSpec((tm, d), lambda j, te, nx, na: (live(j, na), 0)),
                  pl.BlockSpec(memory_space=pl.ANY), pl.BlockSpec(memory_space=pl.ANY)],
        out_specs=pl.BlockSpec((tm, d), lambda j, te, nx, na: (j, 0)),
        scratch_shapes=[pltpu.VMEM((d, 2 * de), F32), pltpu.VMEM((de, d), F32), pltpu.VMEM((d, 2 * de), BF16),
                        pltpu.VMEM((de, d), BF16), pltpu.SemaphoreType.DMA((2,))],
    )
    return pl.pallas_call(
        functools.partial(_experts_kernel, layer=layer),
        grid_spec=grid_spec,
        out_shape=jax.ShapeDtypeStruct((p, d), F32),
        compiler_params=_cparams(("arbitrary",)),
        name="moe_experts",
    )(tile_expert, next_expert, n_active, xs, w_gu, w_down)


def _combine_kernel(pos_ref, x_ref, gw_ref, ys_ref, g_ref, b_ref, o_ref, buf, sem, *, alpha):
    tm = x_ref.shape[0]
    n_slots = pos_ref.shape[0]

    def start(r, carry):
        for s in range(n_slots):
            _row_copy(ys_ref, pos_ref[s, r], buf.at[s], r, sem).start()
        return carry

    def wait(r, carry):
        for s in range(n_slots):
            _row_copy(ys_ref, pos_ref[s, r], buf.at[s], r, sem).wait()
        return carry

    lax.fori_loop(0, tm, start, 0, unroll=8)
    lax.fori_loop(0, tm, wait, 0, unroll=8)
    gw = gw_ref[...]
    y = alpha * x_ref[...]
    for s in range(n_slots):
        y = y + gw[:, s:s + 1] * buf[s]
    o_ref[...] = _layer_norm(y, g_ref[...], b_ref[...], LN_EPS)


def _combine_ln(x, ys, pos, gw, g, b, alpha, tm=256):
    n, d = x.shape
    n_slots = pos.shape[0]
    return pl.pallas_call(
        functools.partial(_combine_kernel, alpha=alpha),
        grid=(n // tm,),
        in_specs=[pl.BlockSpec((n_slots, tm), lambda i: (0, i), memory_space=pltpu.SMEM),
                  pl.BlockSpec((tm, d), lambda i: (i, 0)), pl.BlockSpec((tm, n_slots), lambda i: (i, 0)),
                  pl.BlockSpec(memory_space=pl.ANY), pl.BlockSpec((1, d), lambda i: (0, 0)),
                  pl.BlockSpec((1, d), lambda i: (0, 0))],
        out_specs=pl.BlockSpec((tm, d), lambda i: (i, 0)),
        out_shape=jax.ShapeDtypeStruct((n, d), F32),
        scratch_shapes=[pltpu.VMEM((n_slots, tm, d), F32), pltpu.SemaphoreType.DMA],
        compiler_params=_cparams(("arbitrary",)),
        name="moe_combine",
    )(pos, x, gw, ys, g.reshape(1, d), b.reshape(1, d))


def _moe_ln(x, w_router, b_router, w_gu, w_down, layer, g, b, alpha, tm=256):
    n, d = x.shape
    n_experts = w_down.shape[1]
    meta, counts = _route(x, w_router, b_router)
    eid = meta[:, 0:2].astype(jnp.int32)
    gw = meta[:, 2:4]
    rank = meta[:, 4:6].astype(jnp.int32)
    counts = counts[0, :n_experts].astype(jnp.int32)
    tiles = (counts + tm - 1) // tm
    tile_end = jnp.cumsum(tiles)
    row_off = (tile_end - tiles) * tm
    n_tiles = (eid.size + tm - 1) // tm + n_experts
    tile_expert = jnp.sum(jnp.arange(n_tiles)[:, None] >= tile_end[None, :], axis=1).astype(jnp.int32)
    tile_expert = jnp.minimum(tile_expert, n_experts - 1)
    n_active = tile_end[-1:].astype(jnp.int32)
    ids = jnp.arange(n_experts)
    later = (ids[None, :] > ids[:, None]) & (tiles[None, :] > 0)
    next_of = jnp.min(jnp.where(later, ids[None, :], n_experts), axis=1)
    next_of = jnp.where(next_of < n_experts, next_of, -1).astype(jnp.int32)
    next_expert = next_of[tile_expert]
    sel = eid[:, :, None] == jnp.arange(n_experts)[None, None, :]
    pos = (jnp.sum(jnp.where(sel, row_off[None, None, :], 0), axis=-1) + rank).T
    xs = _dispatch(x, pos, n_tiles * tm)
    ys = _experts(xs, w_gu, w_down, layer, tile_expert, next_expert, n_active, tm)
    return _combine_ln(x, ys, pos, gw, g, b, alpha)


def _token_shift(x_ref, prev_ref, first):
    x = x_ref[...]
    prev_row = jnp.where(first, 0.0, prev_ref[7:8, :])
    row = lax.broadcasted_iota(jnp.int32, x.shape, 0)
    return x, jnp.where(row == 0, prev_row, pltpu.roll(x, 1, 0))


def _rkv_kernel(x_ref, prev_ref, mu_ref, w_ref, o_ref, *, tiles_per_seq):
    first = pl.program_id(0) % tiles_per_seq == 0
    x, xp = _token_shift(x_ref, prev_ref, first)
    xm = x + (xp - x) * mu_ref[0]
    o_ref[0] = jnp.dot(xm.astype(BF16), w_ref[0], preferred_element_type=F32).astype(o_ref.dtype)


def _rkv_proj(x, mu3, w_rkv, t, tm=512):
    n, d = x.shape
    sub = tm // 8
    return pl.pallas_call(
        functools.partial(_rkv_kernel, tiles_per_seq=t // tm),
        grid=(n // tm, 3),
        in_specs=[pl.BlockSpec((tm, d), lambda i, j: (i, 0)),
                  pl.BlockSpec((8, d), lambda i, j: (jnp.maximum(i * sub - 1, 0), 0)),
                  pl.BlockSpec((1, 1, d), lambda i, j: (j, 0, 0)), pl.BlockSpec((1, d, d), lambda i, j: (j, 0, 0))],
        out_specs=pl.BlockSpec((1, tm, d), lambda i, j: (j, i, 0)),
        out_shape=jax.ShapeDtypeStruct((3, n, d), BF16),
        compiler_params=_cparams(("parallel", "arbitrary")),
        name="rkv_proj",
    )(x, x, mu3.reshape(3, 1, d), w_rkv)


def _lora_kernel(x_ref, prev_ref, mu_ref, w0_ref, w1_ref, w2_ref, a0_ref, a1_ref, a2_ref, g1_ref, g2_ref, wl_ref,
                 a_ref, g_ref, *, tiles_per_seq):
    first = pl.program_id(0) % tiles_per_seq == 0
    x, xp = _token_shift(x_ref, prev_ref, first)
    dx = xp - x

    def mm(a, w_ref):
        return jnp.dot(a.astype(BF16), w_ref[...], preferred_element_type=F32)

    zw = w0_ref[...] + mm(jnp.tanh(mm(x + dx * mu_ref[0:1, :], w1_ref)), w2_ref)
    wl_ref[...] = -HALF_DECAY / (1.0 + jnp.exp(-zw))
    za = a0_ref[...] + mm(mm(x + dx * mu_ref[1:2, :], a1_ref), a2_ref)
    a_ref[...] = (1.0 / (1.0 + jnp.exp(-za))).astype(a_ref.dtype)
    zg = mm(x + dx * mu_ref[2:3, :], g1_ref)
    g_ref[...] = mm(1.0 / (1.0 + jnp.exp(-zg)), g2_ref).astype(g_ref.dtype)


def _lora(x, mu3, w0, w1, w2, a0, a1, a2, g1, g2, t, tm=512):
    n, d = x.shape
    sub = tm // 8
    const = lambda shape: pl.BlockSpec(shape, lambda i: (0, 0))
    row = pl.BlockSpec((tm, d), lambda i: (i, 0))
    return pl.pallas_call(
        functools.partial(_lora_kernel, tiles_per_seq=t // tm),
        grid=(n // tm,),
        in_specs=[row, pl.BlockSpec((8, d), lambda i: (jnp.maximum(i * sub - 1, 0), 0)), const((3, d)),
                  const((1, d)), const(w1.shape), const(w2.shape), const((1, d)), const(a1.shape), const(a2.shape),
                  const(g1.shape), const(g2.shape)],
        out_specs=[row, row, row],
        out_shape=[jax.ShapeDtypeStruct((n, d), F32), jax.ShapeDtypeStruct((n, d), BF16),
                   jax.ShapeDtypeStruct((n, d), BF16)],
        compiler_params=_cparams(("parallel",)),
        name="lora",
    )(x, x, mu3, w0.reshape(1, d), w1, w2, a0.reshape(1, d), a1, a2, g1, g2)


def _scan_groups(r, k, v, lw, a_gate, g, kk_w, ka_w, rk_w, gg, gb, s0, *, hd, gn_eps):
    c = r[0].shape[0]
    nh = LANES // hd
    hc = nh * c
    each = lambda f, *xs: [f(*x) for x in zip(*xs)]
    lane = lax.broadcasted_iota(jnp.int32, (1, LANES), 1)
    head_masks = [(lane >= h * hd) & (lane < (h + 1) * hd) for h in range(nh)]
    lr = lax.broadcasted_iota(jnp.int32, (LANES, LANES), 0) // hd
    lc = lax.broadcasted_iota(jnp.int32, (LANES, LANES), 1) // hd
    same_head = jnp.where(lr == lc, 1.0, 0.0).astype(BF16)
    trow = lax.broadcasted_iota(jnp.int32, (c, c), 0)
    tcol = lax.broadcasted_iota(jnp.int32, (c, c), 1)
    lower = jnp.where(trow >= tcol, 1.0, 0.0).astype(BF16)
    prow = lax.broadcasted_iota(jnp.int32, (hc, hc), 0)
    pcol = lax.broadcasted_iota(jnp.int32, (hc, hc), 1)
    same_blk = (prow // c) == (pcol // c)
    strict = same_blk & (prow > pcol)
    incl = same_blk & (prow >= pcol)
    eye = jnp.where(prow == pcol, 1.0, 0.0).astype(F32)

    def split(x):
        hi = x.astype(BF16)
        return hi, (x - hi.astype(F32)).astype(BF16)

    def head_sum(xs):
        return [jnp.dot(x.astype(BF16), same_head, preferred_element_type=F32) for x in xs]

    def per_head(x):
        return jnp.concatenate([jnp.where(hm, x, 0.0) for hm in head_masks], axis=0).astype(BF16)

    kk = each(lambda k_, w_: k_ * w_, k, kk_w)
    kk_n = head_sum(each(lambda x: x * x, kk))
    kk = each(lambda x, n_: x / jnp.maximum(jnp.sqrt(n_), 1e-12), kk, kk_n)
    k2 = each(lambda k_, a_, w_: k_ * (1.0 + (a_ - 1.0) * w_), k, a_gate, ka_w)
    lw_parts = each(split, lw)
    cum = [jnp.dot(lower, hi, preferred_element_type=F32) + jnp.dot(lower, lo, preferred_element_type=F32)
           for hi, lo in lw_parts]
    gam = each(jnp.exp, cum)
    inv_gam = each(lambda x: jnp.exp(-x), cum)
    gam_end = each(lambda x: x[c - 1:c, :], gam)
    a_t = each(lambda kk_, cum_, lw_: -kk_ * jnp.exp(cum_ - lw_), kk, cum, lw)
    b_t = each(lambda kk_, a_, ig: kk_ * a_ * ig, kk, a_gate, inv_gam)
    k_t = each(lambda k2_, ig: k2_ * ig, k2, inv_gam)
    r_t = each(lambda r_, gm: r_ * gm, r, gam)

    s0b = each(lambda x: x.astype(BF16), s0)
    ar_s0 = each(lambda a_, r_, s_: _nt(jnp.concatenate([a_, r_], axis=0).astype(BF16), s_), a_t, r_t, s0b)
    ar2 = each(lambda a_, r_: jnp.concatenate([per_head(a_), per_head(r_)], axis=0), a_t, r_t)
    bk2 = each(lambda b_, k_: jnp.concatenate([per_head(b_), per_head(k_)], axis=0), b_t, k_t)
    v2 = each(per_head, v)
    gmat = each(_nt, ar2, bk2)
    a_ab = each(lambda x: jnp.where(strict, x[:hc, :hc], 0.0), gmat)
    a_ak = each(lambda x: jnp.where(strict, x[:hc, hc:], 0.0), gmat)
    a_r = each(lambda x: jnp.concatenate([jnp.where(incl, x[hc:, :hc], 0.0), jnp.where(incl, x[hc:, hc:], 0.0)],
                                         axis=1).astype(BF16), gmat)
    rhs = each(lambda as0, ak, v_: jnp.concatenate([jnp.where(hm, as0[:c], 0.0) for hm in head_masks], axis=0)
               + _bdot(ak, v_), ar_s0, a_ak, v2)
    inv = each(lambda x: eye + x, a_ab)
    m = each(lambda x: _bdot(x, x), a_ab)
    n_pow = 2
    while 2 * n_pow < c:
        mp = each(lambda m_, p_: _bdot(jnp.concatenate([m_, p_], axis=0), m_), m, inv)
        inv = each(lambda p_, mp_: p_ + mp_[hc:], inv, mp)
        m = each(lambda mp_: mp_[:hc], mp)
        n_pow *= 2
    rhs = each(lambda rhs_, m_: rhs_ + _bdot(m_, rhs_), rhs, m)
    u2 = each(_bdot, inv, rhs)
    uv = each(lambda u_, v_: jnp.concatenate([u_.astype(BF16), v_], axis=0), u2, v2)
    y2 = each(lambda ar_, uv_: jnp.dot(ar_, uv_, preferred_element_type=F32), a_r, uv)
    y = each(lambda rs, y2_: rs[c:] + sum(y2_[h * c:(h + 1) * c, :] for h in range(nh)), ar_s0, y2)
    bkg = each(lambda b_, k_, ge: jnp.concatenate([per_head(b_ * ge), per_head(k_ * ge)], axis=0), b_t, k_t, gam_end)
    s_new = each(lambda s_, ge, uv_, bkg_: s_ * ge + _tn(uv_, bkg_), s0, gam_end, uv, bkg)

    inv_hd = 1.0 / hd
    mean = head_sum(y)
    yc = each(lambda y_, m_: y_ - m_ * inv_hd, y, mean)
    var = head_sum(each(lambda x: x * x, yc))
    yn = each(lambda yc_, var_, gg_, gb_: yc_ * lax.rsqrt(var_ * inv_hd + gn_eps) * gg_ + gb_, yc, var, gg, gb)
    rk_sum = head_sum(each(lambda r_, k2_, w_: r_ * k2_ * w_, r, k2, rk_w))
    out = each(lambda yn_, rk_, v_, g_: (yn_ + rk_ * v_) * g_, yn, rk_sum, v, g)
    return out, s_new


def _scan_kernel(r_ref, k_ref, v_ref, wl_ref, a_ref, g_ref, kk_ref, ka_ref, rk_ref, gg_ref, gb_ref, o_ref, state, *,
                 hd, gn_eps):
    @pl.when(pl.program_id(2) == 0)
    def _():
        state[...] = jnp.zeros_like(state)

    ng = state.shape[0]
    sls = [slice(p * LANES, (p + 1) * LANES) for p in range(ng)]
    tok3 = lambda ref: [ref[0, :, sl].astype(F32) for sl in sls]
    tok2 = lambda ref: [ref[:, sl].astype(F32) for sl in sls]
    out, s_new = _scan_groups(tok3(r_ref), tok3(k_ref), tok3(v_ref), tok2(wl_ref), tok2(a_ref), tok2(g_ref),
                              tok2(kk_ref), tok2(ka_ref), tok2(rk_ref), tok2(gg_ref), tok2(gb_ref),
                              [state[p] for p in range(ng)], hd=hd, gn_eps=gn_eps)
    for p in range(ng):
        state[p] = s_new[p]
        o_ref[:, sls[p]] = out[p].astype(o_ref.dtype)


def _rwkv_scan(rkv, wl, a, g, k_k, k_a, r_k, gn_g, gn_b, batch, hd, gn_eps, chunk=64, groups=16):
    _, n, d = rkv.shape
    t = n // batch
    nc = t // chunk
    groups = min(groups, d // LANES)
    w = groups * LANES
    tok = lambda j: pl.BlockSpec((1, chunk, w), functools.partial(lambda b, p, c, j: (j, b * nc + c, p), j=j))
    tok2 = pl.BlockSpec((chunk, w), lambda b, p, c: (b * nc + c, p))
    par = pl.BlockSpec((1, w), lambda b, p, c: (0, p))
    return pl.pallas_call(
        functools.partial(_scan_kernel, hd=hd, gn_eps=gn_eps),
        grid=(batch, d // w, nc),
        in_specs=[tok(0), tok(1), tok(2), tok2, tok2, tok2, par, par, par, par, par],
        out_specs=tok2,
        out_shape=jax.ShapeDtypeStruct((n, d), BF16),
        scratch_shapes=[pltpu.VMEM((groups, LANES, LANES), F32)],
        compiler_params=_cparams(("parallel", "parallel", "arbitrary")),
        name="rwkv_scan",
    )(rkv, rkv, rkv, wl, a, g, k_k.reshape(1, d), k_a.reshape(1, d), r_k.reshape(1, d), gn_g.reshape(1, d),
      gn_b.reshape(1, d))


def _pad_lora(w_in, w_out):
    r = w_in.shape[1]
    rp = -(-r // LANES) * LANES
    return (jnp.pad(w_in, ((0, 0), (0, rp - r))).astype(BF16), jnp.pad(w_out, ((0, rp - r), (0, 0))).astype(BF16))


def kernel(x, ev_w_in, ev_b_a, ev_w_s, ev_b_s, ev_g_v, ev_b_v, ev_b_f, ev_w_out, rw_mu, rw_w_rkv, rw_w0, rw_w1, rw_w2, rw_a0, rw_a1, rw_a2, rw_g1, rw_g2, rw_k_k, rw_k_a, rw_r_k, rw_gn_g, rw_gn_b, rw_w_o, ln_g, ln_b, w_router, b_router, w_gu, w_down):
    batch, t, d = x.shape
    depth = ln_g.shape[0]
    alpha = (2 * depth) ** 0.25
    h = x.reshape(batch * t, d)
    for layer in range(depth):
        i = layer // 2
        if layer % 2 == 0:
            aw = ev_g_v.shape[1]
            heads = ev_b_f.shape[1]
            q_col = 2 * aw
            bw = heads * LANES
            k_col, v_col, f_col = q_col + bw, q_col + 2 * bw, q_col + 3 * bw
            w_in = ev_w_in[i]
            col = jnp.arange(v_col)
            q_scale = jnp.where((col >= q_col) & (col < k_col), LANES ** -0.5 * LOG2E, 1.0)
            proj = _matmul(h, (w_in[:, :v_col] * q_scale).astype(BF16), v_col, BF16)
            vt, f_logit = _proj_transposed(h, w_in[:, v_col:f_col].T.astype(BF16), w_in[:, f_col:].T, FOX_BLOCK)
            c = _fgate(f_logit, ev_b_f[i], batch)
            y_a = _sgu(proj, ev_b_a[i], ev_w_s[i], ev_b_s[i], ev_g_v[i], ev_b_v[i])
            y_b = _fox_attention(proj, vt, c, batch, heads, q_col, FOX_BLOCK)
            h = _proj_ln([y_a, y_b], ev_w_out[i].astype(BF16), h, ln_g[layer, 0], ln_b[layer, 0], alpha)
        else:
            hd = rw_r_k.shape[2]
            mu = rw_mu[i]
            rkv = _rkv_proj(h, mu[:3], rw_w_rkv[i].astype(BF16), t)
            w1, w2 = _pad_lora(rw_w1[i], rw_w2[i])
            a1, a2 = _pad_lora(rw_a1[i], rw_a2[i])
            g1, g2 = _pad_lora(rw_g1[i], rw_g2[i])
            wl, a, g = _lora(h, mu[3:], rw_w0[i], w1, w2, rw_a0[i], a1, a2, g1, g2, t)
            y = _rwkv_scan(rkv, wl, a, g, rw_k_k[i], rw_k_a[i], rw_r_k[i].reshape(-1), rw_gn_g[i], rw_gn_b[i],
                           batch, hd, hd * 1e-5)
            h = _proj_ln([y], rw_w_o[i].astype(BF16), h, ln_g[layer, 0], ln_b[layer, 0], alpha)
        h = _moe_ln(h, w_router, b_router, w_gu, w_down, layer, ln_g[layer, 1], ln_b[layer, 1], alpha)
    return h.reshape(batch, t, d)
```

```python
import functools

import jax
import jax.numpy as jnp
from jax import lax
from jax.experimental import pallas as pl
from jax.experimental.pallas import tpu as pltpu

F32 = jnp.float32
BF16 = jnp.bfloat16
HI = lax.Precision.HIGHEST

LN_EPS = 1e-5
N_GROUPS = 4
LANES = 128
FOX_BLOCK = 512
LOG2E = 1.4426950408889634
HALF_DECAY = 0.6065306597126334
VMEM_LIMIT = 56 * 1024 * 1024


def _cparams(sem):
    return pltpu.CompilerParams(dimension_semantics=sem, vmem_limit_bytes=VMEM_LIMIT)


def _layer_norm(x, g, b, eps):
    mu = jnp.mean(x, -1, keepdims=True)
    xc = x - mu
    var = jnp.mean(xc * xc, -1, keepdims=True)
    return xc * lax.rsqrt(var + eps) * g + b


def _sigmoid(x):
    return 0.5 + 0.5 * jnp.tanh(0.5 * x)


def _nt(a, b, **kw):
    return lax.dot_general(a, b, (((1,), (1,)), ((), ())), preferred_element_type=F32, **kw)


def _tn(a, b, **kw):
    return lax.dot_general(a, b, (((0,), (0,)), ((), ())), preferred_element_type=F32, **kw)


def _bdot(a, b):
    return jnp.dot(a.astype(BF16), b.astype(BF16), preferred_element_type=F32)


def _mm_kernel(a_ref, w_ref, o_ref, a_bf16):
    @pl.when(pl.program_id(1) == 0)
    def _():
        a_bf16[...] = a_ref[...].astype(BF16)

    o_ref[...] = jnp.dot(a_bf16[...], w_ref[...], preferred_element_type=F32).astype(o_ref.dtype)


def _matmul(a, w, n_cols, out_dtype, tm=1024, tn=512):
    m, k = a.shape
    tm = min(tm, m)
    tn = next(c for c in (tn, 256, LANES) if n_cols % c == 0)
    return pl.pallas_call(
        _mm_kernel,
        grid=(m // tm, n_cols // tn),
        in_specs=[pl.BlockSpec((tm, k), lambda i, j: (i, 0)), pl.BlockSpec((k, tn), lambda i, j: (0, j))],
        out_specs=pl.BlockSpec((tm, tn), lambda i, j: (i, j)),
        out_shape=jax.ShapeDtypeStruct((m, n_cols), out_dtype),
        scratch_shapes=[pltpu.VMEM((tm, k), BF16)],
        compiler_params=_cparams(("parallel", "arbitrary")),
        name="matmul",
    )(a, w)


def _proj_ln_kernel(*refs, n_in, alpha):
    a_refs, w_refs = refs[:n_in], refs[n_in:2 * n_in]
    res_ref, g_ref, b_ref, o_ref = refs[2 * n_in:]
    acc = alpha * res_ref[...]
    for a_ref, w_ref in zip(a_refs, w_refs):
        acc = acc + jnp.dot(a_ref[...], w_ref[...], preferred_element_type=F32)
    o_ref[...] = _layer_norm(acc, g_ref[...], b_ref[...], LN_EPS)


def _proj_ln(a_list, w, res, g, b, alpha, tm=512):
    m, d = res.shape
    n_in = len(a_list)
    kc = a_list[0].shape[1]
    in_specs = [pl.BlockSpec((tm, kc), lambda i: (i, 0)) for _ in a_list]
    in_specs += [pl.BlockSpec((kc, d), functools.partial(lambda i, r: (r, 0), r=r)) for r in range(n_in)]
    in_specs += [pl.BlockSpec((tm, d), lambda i: (i, 0)), pl.BlockSpec((1, d), lambda i: (0, 0)),
                 pl.BlockSpec((1, d), lambda i: (0, 0))]
    return pl.pallas_call(
        functools.partial(_proj_ln_kernel, n_in=n_in, alpha=alpha),
        grid=(m // tm,),
        in_specs=in_specs,
        out_specs=pl.BlockSpec((tm, d), lambda i: (i, 0)),
        out_shape=jax.ShapeDtypeStruct((m, d), F32),
        compiler_params=_cparams(("parallel",)),
        name="proj_ln",
    )(*a_list, *([w] * n_in), res, g.reshape(1, d), b.reshape(1, d))


def _fgate_kernel(z_ref, bf_ref, c_ref, carry):
    @pl.when(pl.program_id(1) == 0)
    def _():
        carry[...] = jnp.zeros_like(carry)

    tm = z_ref.shape[2]
    z = z_ref[0] + bf_ref[...]
    log_f = jnp.minimum(z, 0.0) - jnp.log1p(jnp.exp(-jnp.abs(z)))
    row = lax.broadcasted_iota(jnp.int32, (tm, tm), 0)
    col = lax.broadcasted_iota(jnp.int32, (tm, tm), 1)
    upper = jnp.where(row <= col, 1.0, 0.0).astype(F32)
    c = jnp.dot(log_f, upper, preferred_element_type=F32, precision=HI) + carry[...]
    c_ref[0] = c
    carry[...] = carry[...] + jnp.sum(log_f, axis=-1, keepdims=True)


def _fgate(f_logit, b_f, batch):
    tiles, h, tm = f_logit.shape
    nt = tiles // batch
    return pl.pallas_call(
        _fgate_kernel,
        grid=(batch, nt),
        in_specs=[pl.BlockSpec((1, h, tm), lambda b, i: (b * nt + i, 0, 0)), pl.BlockSpec((h, 1), lambda b, i: (0, 0))],
        out_specs=pl.BlockSpec((1, h, tm), lambda b, i: (b, 0, i)),
        out_shape=jax.ShapeDtypeStruct((batch, h, nt * tm), F32),
        scratch_shapes=[pltpu.VMEM((h, 1), F32)],
        compiler_params=_cparams(("parallel", "arbitrary")),
        name="fgate",
    )(f_logit, b_f.reshape(h, 1))


def _gelu_tanh(x):
    return 0.5 * x * (1.0 + jnp.tanh(0.7978845608028654 * (x + 0.044715 * (x * x * x))))


def _sgu_kernel(z_ref, ba_ref, ws_ref, bs_ref, gv_ref, bv_ref, o_ref, *, chunk, groups):
    aw = o_ref.shape[1]
    gd = aw // groups
    z = _gelu_tanh(z_ref[...].astype(F32) + ba_ref[...])
    u = z[:, :aw]
    v = _layer_norm(z[:, aw:], gv_ref[...], bv_ref[...], LN_EPS).astype(BF16)
    row = lax.broadcasted_iota(jnp.int32, (chunk, chunk), 0)
    col = lax.broadcasted_iota(jnp.int32, (chunk, chunk), 1)
    causal = row >= col
    bs = bs_ref[...]
    for g in range(groups):
        w_g = jnp.where(causal, ws_ref[g], 0.0).astype(BF16)
        for c in range(z.shape[0] // chunk):
            rs = slice(c * chunk, (c + 1) * chunk)
            cs = slice(g * gd, (g + 1) * gd)
            s = jnp.dot(w_g, v[rs, cs], preferred_element_type=F32) + bs[:, g:g + 1]
            o_ref[rs, cs] = (u[rs, cs] * s).astype(o_ref.dtype)


def _sgu(proj, b_a, w_s, b_s, g_v, b_v, tm=512):
    n = proj.shape[0]
    groups, chunk, _ = w_s.shape
    aw = g_v.shape[0]
    return pl.pallas_call(
        functools.partial(_sgu_kernel, chunk=chunk, groups=groups),
        grid=(n // tm,),
        in_specs=[pl.BlockSpec((tm, 2 * aw), lambda i: (i, 0)), pl.BlockSpec((1, 2 * aw), lambda i: (0, 0)),
                  pl.BlockSpec((groups, chunk, chunk), lambda i: (0, 0, 0)),
                  pl.BlockSpec((chunk, groups), lambda i: (0, 0)), pl.BlockSpec((1, aw), lambda i: (0, 0)),
                  pl.BlockSpec((1, aw), lambda i: (0, 0))],
        out_specs=pl.BlockSpec((tm, aw), lambda i: (i, 0)),
        out_shape=jax.ShapeDtypeStruct((n, aw), BF16),
        compiler_params=_cparams(("parallel",)),
        name="sgu",
    )(proj, b_a.reshape(1, -1), w_s, b_s.T, g_v.reshape(1, aw), b_v.reshape(1, aw))


def _vt_kernel(x_ref, w_ref, wf_ref, o_ref, f_ref):
    x = x_ref[...]
    x_hi = x.astype(BF16)
    x_lo = (x - x_hi.astype(F32)).astype(BF16)
    o_ref[0] = _nt(w_ref[...], x_hi).astype(o_ref.dtype)
    f_ref[0] = _nt(wf_ref[0], x_hi) + _nt(wf_ref[0], x_lo) + _nt(wf_ref[1], x_hi)


def _proj_transposed(x, w_t, wf_t, blk):
    n, d = x.shape
    rows, rows_f = w_t.shape[0], wf_t.shape[0]
    wf_hi = wf_t.astype(BF16)
    wf = jnp.stack([wf_hi, (wf_t - wf_hi.astype(F32)).astype(BF16)])
    return pl.pallas_call(
        _vt_kernel,
        grid=(n // blk,),
        in_specs=[pl.BlockSpec((blk, d), lambda i: (i, 0)), pl.BlockSpec((rows, d), lambda i: (0, 0)),
                  pl.BlockSpec((2, rows_f, d), lambda i: (0, 0, 0))],
        out_specs=[pl.BlockSpec((1, rows, blk), lambda i: (i, 0, 0)),
                   pl.BlockSpec((1, rows_f, blk), lambda i: (i, 0, 0))],
        out_shape=[jax.ShapeDtypeStruct((n // blk, rows, blk), BF16),
                   jax.ShapeDtypeStruct((n // blk, rows_f, blk), F32)],
        compiler_params=_cparams(("parallel",)),
        name="proj_transposed",
    )(x, w_t, wf)


def _fox_kernel(q_ref, k_ref, vt_ref, cq_ref, ck_ref, o_ref, m_scr, l_scr, acc_scr, *, blk, nh):
    qi = pl.program_id(2)
    heads = range(nh)
    hs = [slice(h * LANES, (h + 1) * LANES) for h in heads]
    m_scr[...] = jnp.full_like(m_scr, -jnp.inf)
    l_scr[...] = jnp.zeros_like(l_scr)
    acc_scr[...] = jnp.zeros_like(acc_scr)
    q = [q_ref[:, hs[h]] for h in heads]
    cq = [cq_ref[0, h, pl.ds(qi, 1), :] * LOG2E for h in heads]

    def step(ki, masked):
        ks = pl.multiple_of(ki * blk, blk)
        t = [_nt(k_ref[pl.ds(ks, blk), hs[h]], q[h]) - ck_ref[0, h, pl.ds(ks, blk), :] * LOG2E for h in heads]
        if masked:
            row = lax.broadcasted_iota(jnp.int32, (blk, blk), 0)
            col = lax.broadcasted_iota(jnp.int32, (blk, blk), 1)
            t = [jnp.where(row <= col, x, -jnp.inf) for x in t]
        m_prev = [m_scr[h] for h in heads]
        m_new = [jnp.maximum(m_prev[h], cq[h] + jnp.max(t[h], axis=0, keepdims=True)) for h in heads]
        p = [jnp.exp2(t[h] - (m_new[h] - cq[h])) for h in heads]
        corr = [jnp.exp2(m_prev[h] - m_new[h]) for h in heads]
        for h in heads:
            l_scr[h] = corr[h] * l_scr[h] + jnp.sum(p[h], axis=0, keepdims=True)
            acc_scr[h] = corr[h] * acc_scr[h] + jnp.dot(vt_ref[ki, hs[h], :], p[h].astype(BF16),
                                                         preferred_element_type=F32)
            m_scr[h] = m_new[h]

    def body(ki, carry):
        step(ki, False)
        return carry

    lax.fori_loop(0, qi, body, 0)
    step(qi, True)
    for h in heads:
        o_ref[:, hs[h]] = jnp.transpose(acc_scr[h] / l_scr[h]).astype(o_ref.dtype)


def _fox_attention(proj, vt, c, batch, heads, q_col, blk, nh=4):
    n = proj.shape[0]
    t = n // batch
    nb = t // blk
    dh = LANES
    nh = min(nh, heads)
    w = nh * dh
    q0, k0 = q_col // w, (q_col + heads * dh) // w
    c_col = c.reshape(batch, heads, t, 1)
    c_row = c.reshape(batch, heads, nb, blk)
    return pl.pallas_call(
        functools.partial(_fox_kernel, blk=blk, nh=nh),
        grid=(batch, heads // nh, nb),
        in_specs=[pl.BlockSpec((blk, w), lambda b, h, i: (b * nb + i, q0 + h)),
                  pl.BlockSpec((t, w), lambda b, h, i: (b, k0 + h)),
                  pl.BlockSpec((nb, w, blk), lambda b, h, i: (b, h, 0)),
                  pl.BlockSpec((1, nh, nb, blk), lambda b, h, i: (b, h, 0, 0)),
                  pl.BlockSpec((1, nh, t, 1), lambda b, h, i: (b, h, 0, 0))],
        out_specs=pl.BlockSpec((blk, w), lambda b, h, i: (b * nb + i, h)),
        out_shape=jax.ShapeDtypeStruct((n, heads * dh), BF16),
        scratch_shapes=[pltpu.VMEM((nh, 1, blk), F32), pltpu.VMEM((nh, 1, blk), F32),
                        pltpu.VMEM((nh, dh, blk), F32)],
        compiler_params=_cparams(("parallel", "parallel", "arbitrary")),
        name="fox_attention",
    )(proj, proj, vt, c_row, c_col)


def _first_max(p, idx, valid):
    pm = jnp.where(valid, p, -2.0)
    m = jnp.max(pm, axis=0, keepdims=True)
    first = jnp.min(jnp.where(pm == m, idx, float(p.shape[0])), axis=0, keepdims=True)
    return m, first


def _route_kernel(x_ref, wr_ref, br_ref, meta_ref, cnt_ref, carry):
    @pl.when(pl.program_id(0) == 0)
    def _():
        carry[...] = jnp.zeros_like(carry)

    tm = x_ref.shape[0]
    n_experts = br_ref.shape[0]
    per = n_experts // N_GROUPS
    x = x_ref[...]
    x_hi = x.astype(BF16)
    x_lo = (x - x_hi.astype(F32)).astype(BF16)
    logits = _nt(wr_ref[0], x_hi) + _nt(wr_ref[0], x_lo) + _nt(wr_ref[1], x_hi) + br_ref[...]
    e = jnp.exp(logits - jnp.max(logits, axis=0, keepdims=True))
    probs = e / jnp.sum(e, axis=0, keepdims=True)
    idx = lax.broadcasted_iota(jnp.int32, (n_experts, tm), 0).astype(F32)
    best_score = jnp.full((1, tm), -1.0, F32)
    best_group = jnp.zeros((1, tm), F32)
    for grp in range(N_GROUPS):
        in_g = (idx >= grp * per) & (idx < (grp + 1) * per)
        m1, i1 = _first_max(probs, idx, in_g)
        m2, _ = _first_max(probs, idx, in_g & (idx != i1))
        score = m1 + m2
        take = score > best_score
        best_score = jnp.where(take, score, best_score)
        best_group = jnp.where(take, float(grp), best_group)
    in_sel = (idx >= best_group * per) & (idx < (best_group + 1) * per)
    p1, i1 = _first_max(probs, idx, in_sel)
    p2, i2 = _first_max(probs, idx, in_sel & (idx != i1))
    tot = p1 + p2
    row = lax.broadcasted_iota(jnp.int32, (tm, tm), 0)
    col = lax.broadcasted_iota(jnp.int32, (tm, tm), 1)
    earlier = jnp.where(row < col, 1.0, 0.0).astype(BF16)
    onehot = jnp.where((idx == i1) | (idx == i2), 1.0, 0.0)
    seen = jnp.dot(onehot.astype(BF16), earlier, preferred_element_type=F32) + carry[...]
    r1 = jnp.sum(jnp.where(idx == i1, seen, 0.0), axis=0, keepdims=True)
    r2 = jnp.sum(jnp.where(idx == i2, seen, 0.0), axis=0, keepdims=True)
    zero = jnp.zeros_like(r1)
    meta_ref[...] = jnp.concatenate([i1, i2, p1 / tot, p2 / tot, r1, r2, zero, zero], axis=0)
    carry[...] = carry[...] + jnp.sum(onehot, axis=1, keepdims=True)
    cnt_ref[...] = jnp.broadcast_to(carry[...], cnt_ref.shape)


def _route(x, w_router, b_router, tm=512):
    n, d = x.shape
    n_experts = w_router.shape[1]
    wr = w_router.T
    wr_hi = wr.astype(BF16)
    wr = jnp.stack([wr_hi, (wr - wr_hi.astype(F32)).astype(BF16)])
    return pl.pallas_call(
        _route_kernel,
        grid=(n // tm,),
        in_specs=[pl.BlockSpec((tm, d), lambda i: (i, 0)), pl.BlockSpec((2, n_experts, d), lambda i: (0, 0, 0)),
                  pl.BlockSpec((n_experts, 1), lambda i: (0, 0))],
        out_specs=[pl.BlockSpec((8, tm), lambda i: (0, i)), pl.BlockSpec((n_experts, LANES), lambda i: (0, 0))],
        out_shape=[jax.ShapeDtypeStruct((8, n), F32), jax.ShapeDtypeStruct((n_experts, LANES), F32)],
        scratch_shapes=[pltpu.VMEM((n_experts, 1), F32)],
        compiler_params=_cparams(("arbitrary",)),
        name="route",
    )(x, wr, b_router.reshape(n_experts, 1))


def _row_copy(src_ref, src_row, dst_ref, dst_row, sem):
    return pltpu.make_async_copy(src_ref.at[pl.ds(src_row, 1)], dst_ref.at[pl.ds(dst_row, 1)], sem)


def _dispatch_kernel(zid_ref, pos_ref, x_ref, xs_ref, zbuf, sem, zsem, *, tm):
    def zero_tiles(fn):
        for i in range(zid_ref.shape[0]):
            new_id = zid_ref[i] != zid_ref[max(i - 1, 0)] if i else True

            @pl.when(new_id)
            def _():
                fn(pltpu.make_async_copy(zbuf, xs_ref.at[pl.ds(zid_ref[i] * tm, tm)], zsem))

    @pl.when(pl.program_id(0) == 0)
    def _():
        zbuf[...] = jnp.zeros_like(zbuf)
        zero_tiles(lambda cp: cp.start())
        zero_tiles(lambda cp: cp.wait())

    def rows(fn):
        def body(r, carry):
            for s in range(pos_ref.shape[0]):
                fn(_row_copy(x_ref, r, xs_ref, pos_ref[s, r], sem))
            return carry
        lax.fori_loop(0, x_ref.shape[0], body, 0, unroll=8)

    rows(lambda cp: cp.start())
    rows(lambda cp: cp.wait())


def _dispatch(x, pos, zero_ids, n_rows, tm):
    n, d = x.shape
    tok = 256
    grid_spec = pltpu.PrefetchScalarGridSpec(
        num_scalar_prefetch=1,
        grid=(n // tok,),
        in_specs=[pl.BlockSpec((pos.shape[0], tok), lambda i, z: (0, i), memory_space=pltpu.SMEM),
                  pl.BlockSpec((tok, d), lambda i, z: (i, 0))],
        out_specs=pl.BlockSpec(memory_space=pl.ANY),
        scratch_shapes=[pltpu.VMEM((tm, d), x.dtype), pltpu.SemaphoreType.DMA, pltpu.SemaphoreType.DMA],
    )
    return pl.pallas_call(
        functools.partial(_dispatch_kernel, tm=tm),
        grid_spec=grid_spec,
        out_shape=jax.ShapeDtypeStruct((n_rows, d), x.dtype),
        compiler_params=_cparams(("arbitrary",)),
        name="moe_dispatch",
    )(zero_ids, pos, x)


def _experts_kernel(te_ref, nx_ref, na_ref, xs_ref, wgu_hbm, wd_hbm, ys_ref, wgu_f, wd_f, wgu_b, wd_b, sems, *, layer):
    j = pl.program_id(0)
    live = j < na_ref[0]
    fresh = (j == 0) | (te_ref[j] != te_ref[jnp.maximum(j - 1, 0)])

    def fetch(e):
        return (pltpu.make_async_copy(wgu_hbm.at[layer, e], wgu_f, sems.at[0]),
                pltpu.make_async_copy(wd_hbm.at[layer, e], wd_f, sems.at[1]))

    @pl.when(live & (j == 0))
    def _():
        for cp in fetch(te_ref[0]):
            cp.start()

    @pl.when(live & fresh)
    def _():
        for cp in fetch(te_ref[j]):
            cp.wait()
        wgu_b[...] = wgu_f[...].astype(BF16)
        wd_b[...] = wd_f[...].astype(BF16)

    @pl.when(live & fresh & (nx_ref[j] >= 0))
    def _():
        for cp in fetch(nx_ref[j]):
            cp.start()

    @pl.when(live)
    def _():
        de = wd_b.shape[0]
        gu = jnp.dot(xs_ref[...].astype(BF16), wgu_b[...], preferred_element_type=F32)
        gpart, upart = gu[:, :de], gu[:, de:]
        h = (gpart / (1.0 + jnp.exp(-gpart))) * upart
        ys_ref[...] = jnp.dot(h.astype(BF16), wd_b[...], preferred_element_type=F32)

    @pl.when(jnp.logical_not(live))
    def _():
        ys_ref[...] = jnp.zeros_like(ys_ref)


def _experts(xs, w_gu, w_down, layer, tile_expert, next_expert, n_active, tm):
    p, d = xs.shape
    de = w_down.shape[2]
    live = lambda j, na: jnp.maximum(jnp.minimum(j, na[0] - 1), 0)
    grid_spec = pltpu.PrefetchScalarGridSpec(
        num_scalar_prefetch=3,
        grid=(p // tm,),
        in_specs=[pl.BlockSpec((tm, d), lambda j, te, nx, na: (live(j, na), 0)),
                  pl.BlockSpec(memory_space=pl.ANY), pl.BlockSpec(memory_space=pl.ANY)],
        out_specs=pl.BlockSpec((tm, d), lambda j, te, nx, na: (j, 0)),
        scratch_shapes=[pltpu.VMEM((d, 2 * de), F32), pltpu.VMEM((de, d), F32), pltpu.VMEM((d, 2 * de), BF16),
                        pltpu.VMEM((de, d), BF16), pltpu.SemaphoreType.DMA((2,))],
    )
    return pl.pallas_call(
        functools.partial(_experts_kernel, layer=layer),
        grid_spec=grid_spec,
        out_shape=jax.ShapeDtypeStruct((p, d), F32),
        compiler_params=_cparams(("arbitrary",)),
        name="moe_experts",
    )(tile_expert, next_expert, n_active, xs, w_gu, w_down)


def _combine_kernel(pos_ref, x_ref, gw_ref, ys_ref, g_ref, b_ref, o_ref, buf, sems, *, alpha):
    tm = x_ref.shape[0]
    n_slots = pos_ref.shape[0]
    half = tm // 2

    def rows(h, fn):
        def body(r, carry):
            for s in range(n_slots):
                fn(_row_copy(ys_ref, pos_ref[s, r], buf.at[s], r, sems.at[h]))
            return carry
        lax.fori_loop(h * half, (h + 1) * half, body, 0, unroll=8)

    for h in range(2):
        rows(h, lambda cp: cp.start())
    for h in range(2):
        rows(h, lambda cp: cp.wait())
        rs = pl.ds(h * half, half)
        gw = gw_ref[rs, :]
        y = alpha * x_ref[rs, :]
        for s in range(n_slots):
            y = y + gw[:, s:s + 1] * buf[s, rs, :]
        o_ref[rs, :] = _layer_norm(y, g_ref[...], b_ref[...], LN_EPS)


def _combine_ln(x, ys, pos, gw, g, b, alpha, tm=256):
    n, d = x.shape
    n_slots = pos.shape[0]
    return pl.pallas_call(
        functools.partial(_combine_kernel, alpha=alpha),
        grid=(n // tm,),
        in_specs=[pl.BlockSpec((n_slots, tm), lambda i: (0, i), memory_space=pltpu.SMEM),
                  pl.BlockSpec((tm, d), lambda i: (i, 0)), pl.BlockSpec((tm, n_slots), lambda i: (i, 0)),
                  pl.BlockSpec(memory_space=pl.ANY), pl.BlockSpec((1, d), lambda i: (0, 0)),
                  pl.BlockSpec((1, d), lambda i: (0, 0))],
        out_specs=pl.BlockSpec((tm, d), lambda i: (i, 0)),
        out_shape=jax.ShapeDtypeStruct((n, d), F32),
        scratch_shapes=[pltpu.VMEM((n_slots, tm, d), F32), pltpu.SemaphoreType.DMA((2,))],
        compiler_params=_cparams(("arbitrary",)),
        name="moe_combine",
    )(pos, x, gw, ys, g.reshape(1, d), b.reshape(1, d))


def _moe_ln(x, w_router, b_router, w_gu, w_down, layer, g, b, alpha, tm=256):
    n, d = x.shape
    n_experts = w_down.shape[1]
    meta, counts = _route(x, w_router, b_router)
    eid = meta[0:2].astype(jnp.int32)
    gw = meta[2:4].T
    rank = meta[4:6].astype(jnp.int32)
    counts = counts[:, 0].astype(jnp.int32)
    tiles = (counts + tm - 1) // tm
    tile_end = jnp.cumsum(tiles)
    row_off = (tile_end - tiles) * tm
    n_tiles = (eid.size + tm - 1) // tm + n_experts
    tile_expert = jnp.sum(jnp.arange(n_tiles)[:, None] >= tile_end[None, :], axis=1).astype(jnp.int32)
    tile_expert = jnp.minimum(tile_expert, n_experts - 1)
    n_active = tile_end[-1:].astype(jnp.int32)
    ids = jnp.arange(n_experts)
    later = (ids[None, :] > ids[:, None]) & (tiles[None, :] > 0)
    next_of = jnp.min(jnp.where(later, ids[None, :], n_experts), axis=1)
    next_of = jnp.where(next_of < n_experts, next_of, -1).astype(jnp.int32)
    next_expert = next_of[tile_expert]
    pos = row_off[eid] + rank
    last_tile = jnp.maximum(tile_end - 1, 0)
    idle_tile = jnp.minimum(tile_end[-1] + jnp.arange(n_experts), n_tiles - 1)
    zero_ids = jnp.concatenate([last_tile, idle_tile]).astype(jnp.int32)
    xs = _dispatch(x, pos, zero_ids, n_tiles * tm, tm)
    ys = _experts(xs, w_gu, w_down, layer, tile_expert, next_expert, n_active, tm)
    return _combine_ln(x, ys, pos, gw, g, b, alpha)


def _token_shift(x_ref, prev_ref, first):
    x = x_ref[...]
    prev_row = jnp.where(first, 0.0, prev_ref[7:8, :])
    row = lax.broadcasted_iota(jnp.int32, x.shape, 0)
    return x, jnp.where(row == 0, prev_row, pltpu.roll(x, 1, 0))


def _rkv_kernel(x_ref, prev_ref, mu_ref, w_ref, o_ref, *, tiles_per_seq):
    first = pl.program_id(0) % tiles_per_seq == 0
    x, xp = _token_shift(x_ref, prev_ref, first)
    xm = x + (xp - x) * mu_ref[0]
    o_ref[0] = jnp.dot(xm.astype(BF16), w_ref[0], preferred_element_type=F32).astype(o_ref.dtype)


def _rkv_proj(x, mu3, w_rkv, t, tm=512):
    n, d = x.shape
    sub = tm // 8
    return pl.pallas_call(
        functools.partial(_rkv_kernel, tiles_per_seq=t // tm),
        grid=(n // tm, 3),
        in_specs=[pl.BlockSpec((tm, d), lambda i, j: (i, 0)),
                  pl.BlockSpec((8, d), lambda i, j: (jnp.maximum(i * sub - 1, 0), 0)),
                  pl.BlockSpec((1, 1, d), lambda i, j: (j, 0, 0)), pl.BlockSpec((1, d, d), lambda i, j: (j, 0, 0))],
        out_specs=pl.BlockSpec((1, tm, d), lambda i, j: (j, i, 0)),
        out_shape=jax.ShapeDtypeStruct((3, n, d), BF16),
        compiler_params=_cparams(("parallel", "arbitrary")),
        name="rkv_proj",
    )(x, x, mu3.reshape(3, 1, d), w_rkv)


def _lora_kernel(x_ref, prev_ref, mu_ref, w0_ref, w1_ref, w2_ref, a0_ref, a1_ref, a2_ref, g1_ref, g2_ref, wl_ref,
                 a_ref, g_ref, *, tiles_per_seq):
    first = pl.program_id(0) % tiles_per_seq == 0
    x, xp = _token_shift(x_ref, prev_ref, first)
    dx = xp - x

    def mm(a, w_ref):
        return jnp.dot(a.astype(BF16), w_ref[...], preferred_element_type=F32)

    zw = w0_ref[...] + mm(jnp.tanh(mm(x + dx * mu_ref[0:1, :], w1_ref)), w2_ref)
    wl_ref[...] = -HALF_DECAY * _sigmoid(zw)
    za = a0_ref[...] + mm(mm(x + dx * mu_ref[1:2, :], a1_ref), a2_ref)
    a_ref[...] = _sigmoid(za).astype(a_ref.dtype)
    zg = mm(x + dx * mu_ref[2:3, :], g1_ref)
    g_ref[...] = mm(_sigmoid(zg), g2_ref).astype(g_ref.dtype)


def _lora(x, mu3, w0, w1, w2, a0, a1, a2, g1, g2, t, tm=512):
    n, d = x.shape
    sub = tm // 8
    const = lambda shape: pl.BlockSpec(shape, lambda i: (0, 0))
    row = pl.BlockSpec((tm, d), lambda i: (i, 0))
    return pl.pallas_call(
        functools.partial(_lora_kernel, tiles_per_seq=t // tm),
        grid=(n // tm,),
        in_specs=[row, pl.BlockSpec((8, d), lambda i: (jnp.maximum(i * sub - 1, 0), 0)), const((3, d)),
                  const((1, d)), const(w1.shape), const(w2.shape), const((1, d)), const(a1.shape), const(a2.shape),
                  const(g1.shape), const(g2.shape)],
        out_specs=[row, row, row],
        out_shape=[jax.ShapeDtypeStruct((n, d), F32), jax.ShapeDtypeStruct((n, d), BF16),
                   jax.ShapeDtypeStruct((n, d), BF16)],
        compiler_params=_cparams(("parallel",)),
        name="lora",
    )(x, x, mu3, w0.reshape(1, d), w1, w2, a0.reshape(1, d), a1, a2, g1, g2)


def _scan_groups(r, k, v, lw, a_gate, g, kk_w, ka_w, rk_w, gg, gb, s0, *, hd, gn_eps):
    c = r[0].shape[0]
    nh = LANES // hd
    hc = nh * c
    each = lambda f, *xs: [f(*x) for x in zip(*xs)]
    lane = lax.broadcasted_iota(jnp.int32, (1, LANES), 1)
    head_masks = [(lane >= h * hd) & (lane < (h + 1) * hd) for h in range(nh)]
    lr = lax.broadcasted_iota(jnp.int32, (LANES, LANES), 0) // hd
    lc = lax.broadcasted_iota(jnp.int32, (LANES, LANES), 1) // hd
    same_head = jnp.where(lr == lc, 1.0, 0.0).astype(BF16)
    trow = lax.broadcasted_iota(jnp.int32, (c, c), 0)
    tcol = lax.broadcasted_iota(jnp.int32, (c, c), 1)
    lower = jnp.where(trow >= tcol, 1.0, 0.0).astype(BF16)
    prow = lax.broadcasted_iota(jnp.int32, (hc, hc), 0)
    pcol = lax.broadcasted_iota(jnp.int32, (hc, hc), 1)
    same_blk = (prow // c) == (pcol // c)
    strict = same_blk & (prow > pcol)
    incl = same_blk & (prow >= pcol)
    eye = jnp.where(prow == pcol, 1.0, 0.0).astype(F32)

    def split(x):
        hi = x.astype(BF16)
        return hi, (x - hi.astype(F32)).astype(BF16)

    def head_sum(xs):
        return [jnp.dot(x.astype(BF16), same_head, preferred_element_type=F32) for x in xs]

    def per_head(x):
        return jnp.concatenate([jnp.where(hm, x, 0.0) for hm in head_masks], axis=0).astype(BF16)

    kk = each(lambda k_, w_: k_ * w_, k, kk_w)
    kk_n = head_sum(each(lambda x: x * x, kk))
    kk = each(lambda x, n_: x / jnp.maximum(jnp.sqrt(n_), 1e-12), kk, kk_n)
    k2 = each(lambda k_, a_, w_: k_ * (1.0 + (a_ - 1.0) * w_), k, a_gate, ka_w)
    lw_parts = each(split, lw)
    cum = [jnp.dot(lower, hi, preferred_element_type=F32) + jnp.dot(lower, lo, preferred_element_type=F32)
           for hi, lo in lw_parts]
    gam = each(jnp.exp, cum)
    inv_gam = each(lambda x: jnp.exp(-x), cum)
    gam_end = each(lambda x: x[c - 1:c, :], gam)
    a_t = each(lambda kk_, cum_, lw_: -kk_ * jnp.exp(cum_ - lw_), kk, cum, lw)
    b_t = each(lambda kk_, a_, ig: kk_ * a_ * ig, kk, a_gate, inv_gam)
    k_t = each(lambda k2_, ig: k2_ * ig, k2, inv_gam)
    r_t = each(lambda r_, gm: r_ * gm, r, gam)

    s0b = each(lambda x: x.astype(BF16), s0)
    ar_s0 = each(lambda a_, r_, s_: _nt(jnp.concatenate([a_, r_], axis=0).astype(BF16), s_), a_t, r_t, s0b)
    ar2 = each(lambda a_, r_: jnp.concatenate([per_head(a_), per_head(r_)], axis=0), a_t, r_t)
    bk2 = each(lambda b_, k_: jnp.concatenate([per_head(b_), per_head(k_)], axis=0), b_t, k_t)
    v2 = each(per_head, v)
    gmat = each(_nt, ar2, bk2)
    a_ab = each(lambda x: jnp.where(strict, x[:hc, :hc], 0.0), gmat)
    a_ak = each(lambda x: jnp.where(strict, x[:hc, hc:], 0.0), gmat)
    a_r = each(lambda x: jnp.concatenate([jnp.where(incl, x[hc:, :hc], 0.0), jnp.where(incl, x[hc:, hc:], 0.0)],
                                         axis=1).astype(BF16), gmat)
    rhs = each(lambda as0, ak, v_: jnp.concatenate([jnp.where(hm, as0[:c], 0.0) for hm in head_masks], axis=0)
               + _bdot(ak, v_), ar_s0, a_ak, v2)
    inv = each(lambda x: eye + x, a_ab)
    m = each(lambda x: _bdot(x, x), a_ab)
    n_pow = 2
    while 2 * n_pow < c:
        mp = each(lambda m_, p_: _bdot(jnp.concatenate([m_, p_], axis=0), m_), m, inv)
        inv = each(lambda p_, mp_: p_ + mp_[hc:], inv, mp)
        m = each(lambda mp_: mp_[:hc], mp)
        n_pow *= 2
    rhs = each(lambda rhs_, m_: rhs_ + _bdot(m_, rhs_), rhs, m)
    u2 = each(_bdot, inv, rhs)
    uv = each(lambda u_, v_: jnp.concatenate([u_.astype(BF16), v_], axis=0), u2, v2)
    y2 = each(lambda ar_, uv_: jnp.dot(ar_, uv_, preferred_element_type=F32), a_r, uv)
    y = each(lambda rs, y2_: rs[c:] + sum(y2_[h * c:(h + 1) * c, :] for h in range(nh)), ar_s0, y2)
    bkg = each(lambda b_, k_, ge: jnp.concatenate([per_head(b_ * ge), per_head(k_ * ge)], axis=0), b_t, k_t, gam_end)
    s_new = each(lambda s_, ge, uv_, bkg_: s_ * ge + _tn(uv_, bkg_), s0, gam_end, uv, bkg)

    inv_hd = 1.0 / hd
    mean = head_sum(y)
    yc = each(lambda y_, m_: y_ - m_ * inv_hd, y, mean)
    var = head_sum(each(lambda x: x * x, yc))
    yn = each(lambda yc_, var_, gg_, gb_: yc_ * lax.rsqrt(var_ * inv_hd + gn_eps) * gg_ + gb_, yc, var, gg, gb)
    rk_sum = head_sum(each(lambda r_, k2_, w_: r_ * k2_ * w_, r, k2, rk_w))
    out = each(lambda yn_, rk_, v_, g_: (yn_ + rk_ * v_) * g_, yn, rk_sum, v, g)
    return out, s_new


def _scan_kernel(r_ref, k_ref, v_ref, wl_ref, a_ref, g_ref, kk_ref, ka_ref, rk_ref, gg_ref, gb_ref, o_ref, state, *,
                 hd, gn_eps):
    @pl.when(pl.program_id(2) == 0)
    def _():
        state[...] = jnp.zeros_like(state)

    ng = state.shape[0]
    sls = [slice(p * LANES, (p + 1) * LANES) for p in range(ng)]
    tok3 = lambda ref: [ref[0, :, sl].astype(F32) for sl in sls]
    tok2 = lambda ref: [ref[:, sl].astype(F32) for sl in sls]
    out, s_new = _scan_groups(tok3(r_ref), tok3(k_ref), tok3(v_ref), tok2(wl_ref), tok2(a_ref), tok2(g_ref),
                              tok2(kk_ref), tok2(ka_ref), tok2(rk_ref), tok2(gg_ref), tok2(gb_ref),
                              [state[p] for p in range(ng)], hd=hd, gn_eps=gn_eps)
    for p in range(ng):
        state[p] = s_new[p]
        o_ref[:, sls[p]] = out[p].astype(o_ref.dtype)


def _rwkv_scan(rkv, wl, a, g, k_k, k_a, r_k, gn_g, gn_b, batch, hd, gn_eps, chunk=64, groups=16):
    _, n, d = rkv.shape
    t = n // batch
    nc = t // chunk
    groups = min(groups, d // LANES)
    w = groups * LANES
    tok = lambda j: pl.BlockSpec((1, chunk, w), functools.partial(lambda b, p, c, j: (j, b * nc + c, p), j=j))
    tok2 = pl.BlockSpec((chunk, w), lambda b, p, c: (b * nc + c, p))
    par = pl.BlockSpec((1, w), lambda b, p, c: (0, p))
    return pl.pallas_call(
        functools.partial(_scan_kernel, hd=hd, gn_eps=gn_eps),
        grid=(batch, d // w, nc),
        in_specs=[tok(0), tok(1), tok(2), tok2, tok2, tok2, par, par, par, par, par],
        out_specs=tok2,
        out_shape=jax.ShapeDtypeStruct((n, d), BF16),
        scratch_shapes=[pltpu.VMEM((groups, LANES, LANES), F32)],
        compiler_params=_cparams(("parallel", "parallel", "arbitrary")),
        name="rwkv_scan",
    )(rkv, rkv, rkv, wl, a, g, k_k.reshape(1, d), k_a.reshape(1, d), r_k.reshape(1, d), gn_g.reshape(1, d),
      gn_b.reshape(1, d))


def _pad_lora(w_in, w_out):
    r = w_in.shape[1]
    rp = -(-r // LANES) * LANES
    return (jnp.pad(w_in, ((0, 0), (0, rp - r))).astype(BF16), jnp.pad(w_out, ((0, rp - r), (0, 0))).astype(BF16))


def kernel(x, ev_w_in, ev_b_a, ev_w_s, ev_b_s, ev_g_v, ev_b_v, ev_b_f, ev_w_out, rw_mu, rw_w_rkv, rw_w0, rw_w1, rw_w2, rw_a0, rw_a1, rw_a2, rw_g1, rw_g2, rw_k_k, rw_k_a, rw_r_k, rw_gn_g, rw_gn_b, rw_w_o, ln_g, ln_b, w_router, b_router, w_gu, w_down):
    batch, t, d = x.shape
    depth = ln_g.shape[0]
    alpha = (2 * depth) ** 0.25
    h = x.reshape(batch * t, d)
    for layer in range(depth):
        i = layer // 2
        if layer % 2 == 0:
            aw = ev_g_v.shape[1]
            heads = ev_b_f.shape[1]
            q_col = 2 * aw
            bw = heads * LANES
            k_col, v_col, f_col = q_col + bw, q_col + 2 * bw, q_col + 3 * bw
            w_in = ev_w_in[i]
            col = jnp.arange(v_col)
            q_scale = jnp.where((col >= q_col) & (col < k_col), LANES ** -0.5 * LOG2E, 1.0)
            proj = _matmul(h, (w_in[:, :v_col] * q_scale).astype(BF16), v_col, BF16)
            vt, f_logit = _proj_transposed(h, w_in[:, v_col:f_col].T.astype(BF16), w_in[:, f_col:].T, FOX_BLOCK)
            c = _fgate(f_logit, ev_b_f[i], batch)
            y_a = _sgu(proj, ev_b_a[i], ev_w_s[i], ev_b_s[i], ev_g_v[i], ev_b_v[i])
            y_b = _fox_attention(proj, vt, c, batch, heads, q_col, FOX_BLOCK)
            h = _proj_ln([y_a, y_b], ev_w_out[i].astype(BF16), h, ln_g[layer, 0], ln_b[layer, 0], alpha)
        else:
            hd = rw_r_k.shape[2]
            mu = rw_mu[i]
            rkv = _rkv_proj(h, mu[:3], rw_w_rkv[i].astype(BF16), t)
            w1, w2 = _pad_lora(rw_w1[i], rw_w2[i])
            a1, a2 = _pad_lora(rw_a1[i], rw_a2[i])
            g1, g2 = _pad_lora(rw_g1[i], rw_g2[i])
            wl, a, g = _lora(h, mu[3:], rw_w0[i], w1, w2, rw_a0[i], a1, a2, g1, g2, t)
            y = _rwkv_scan(rkv, wl, a, g, rw_k_k[i], rw_k_a[i], rw_r_k[i].reshape(-1), rw_gn_g[i], rw_gn_b[i],
                           batch, hd, hd * 1e-5)
            h = _proj_ln([y], rw_w_o[i].astype(BF16), h, ln_g[layer, 0], ln_b[layer, 0], alpha)
        h = _moe_ln(h, w_router, b_router, w_gu, w_down, layer, ln_g[layer, 1], ln_b[layer, 1], alpha)
    return h.reshape(batch, t, d)
```

```python
import functools

import jax
import jax.numpy as jnp
from jax import lax
from jax.experimental import pallas as pl
from jax.experimental.pallas import tpu as pltpu

F32 = jnp.float32
BF16 = jnp.bfloat16
HI = lax.Precision.HIGHEST

LN_EPS = 1e-5
N_GROUPS = 4
LANES = 128
FOX_BLOCK = 512
LOG2E = 1.4426950408889634
HALF_DECAY = 0.6065306597126334
VMEM_LIMIT = 56 * 1024 * 1024


def _cparams(sem):
    return pltpu.CompilerParams(dimension_semantics=sem, vmem_limit_bytes=VMEM_LIMIT)


def _layer_norm(x, g, b, eps):
    mu = jnp.mean(x, -1, keepdims=True)
    xc = x - mu
    var = jnp.mean(xc * xc, -1, keepdims=True)
    return xc * lax.rsqrt(var + eps) * g + b


def _sigmoid(x):
    return 0.5 + 0.5 * jnp.tanh(0.5 * x)


def _nt(a, b, **kw):
    return lax.dot_general(a, b, (((1,), (1,)), ((), ())), preferred_element_type=F32, **kw)


def _tn(a, b, **kw):
    return lax.dot_general(a, b, (((0,), (0,)), ((), ())), preferred_element_type=F32, **kw)


def _bdot(a, b):
    return jnp.dot(a.astype(BF16), b.astype(BF16), preferred_element_type=F32)


def _mm_kernel(a_ref, w_ref, o_ref, a_bf16):
    @pl.when(pl.program_id(1) == 0)
    def _():
        a_bf16[...] = a_ref[...].astype(BF16)

    o_ref[...] = jnp.dot(a_bf16[...], w_ref[...], preferred_element_type=F32).astype(o_ref.dtype)


def _matmul(a, w, n_cols, out_dtype, tm=1024, tn=512):
    m, k = a.shape
    tm = min(tm, m)
    tn = next(c for c in (tn, 256, LANES) if n_cols % c == 0)
    return pl.pallas_call(
        _mm_kernel,
        grid=(m // tm, n_cols // tn),
        in_specs=[pl.BlockSpec((tm, k), lambda i, j: (i, 0)), pl.BlockSpec((k, tn), lambda i, j: (0, j))],
        out_specs=pl.BlockSpec((tm, tn), lambda i, j: (i, j)),
        out_shape=jax.ShapeDtypeStruct((m, n_cols), out_dtype),
        scratch_shapes=[pltpu.VMEM((tm, k), BF16)],
        compiler_params=_cparams(("parallel", "arbitrary")),
        name="matmul",
    )(a, w)


def _proj_ln_kernel(*refs, n_in, alpha):
    a_refs, w_refs = refs[:n_in], refs[n_in:2 * n_in]
    res_ref, g_ref, b_ref, o_ref = refs[2 * n_in:]
    acc = alpha * res_ref[...]
    for a_ref, w_ref in zip(a_refs, w_refs):
        acc = acc + jnp.dot(a_ref[...], w_ref[...], preferred_element_type=F32)
    o_ref[...] = _layer_norm(acc, g_ref[...], b_ref[...], LN_EPS)


def _proj_ln(a_list, w, res, g, b, alpha, tm=512):
    m, d = res.shape
    n_in = len(a_list)
    kc = a_list[0].shape[1]
    in_specs = [pl.BlockSpec((tm, kc), lambda i: (i, 0)) for _ in a_list]
    in_specs += [pl.BlockSpec((kc, d), functools.partial(lambda i, r: (r, 0), r=r)) for r in range(n_in)]
    in_specs += [pl.BlockSpec((tm, d), lambda i: (i, 0)), pl.BlockSpec((1, d), lambda i: (0, 0)),
                 pl.BlockSpec((1, d), lambda i: (0, 0))]
    return pl.pallas_call(
        functools.partial(_proj_ln_kernel, n_in=n_in, alpha=alpha),
        grid=(m // tm,),
        in_specs=in_specs,
        out_specs=pl.BlockSpec((tm, d), lambda i: (i, 0)),
        out_shape=jax.ShapeDtypeStruct((m, d), F32),
        compiler_params=_cparams(("parallel",)),
        name="proj_ln",
    )(*a_list, *([w] * n_in), res, g.reshape(1, d), b.reshape(1, d))


def _fgate_kernel(z_ref, bf_ref, c_ref, carry):
    @pl.when(pl.program_id(1) == 0)
    def _():
        carry[...] = jnp.zeros_like(carry)

    tm = z_ref.shape[2]
    z = z_ref[0] + bf_ref[...]
    log_f = jnp.minimum(z, 0.0) - jnp.log1p(jnp.exp(-jnp.abs(z)))
    row = lax.broadcasted_iota(jnp.int32, (tm, tm), 0)
    col = lax.broadcasted_iota(jnp.int32, (tm, tm), 1)
    upper = jnp.where(row <= col, 1.0, 0.0).astype(F32)
    c = jnp.dot(log_f, upper, preferred_element_type=F32, precision=HI) + carry[...]
    c_ref[0] = c
    carry[...] = carry[...] + jnp.sum(log_f, axis=-1, keepdims=True)


def _fgate(f_logit, b_f, batch):
    tiles, h, tm = f_logit.shape
    nt = tiles // batch
    return pl.pallas_call(
        _fgate_kernel,
        grid=(batch, nt),
        in_specs=[pl.BlockSpec((1, h, tm), lambda b, i: (b * nt + i, 0, 0)), pl.BlockSpec((h, 1), lambda b, i: (0, 0))],
        out_specs=pl.BlockSpec((1, h, tm), lambda b, i: (b, 0, i)),
        out_shape=jax.ShapeDtypeStruct((batch, h, nt * tm), F32),
        scratch_shapes=[pltpu.VMEM((h, 1), F32)],
        compiler_params=_cparams(("parallel", "arbitrary")),
        name="fgate",
    )(f_logit, b_f.reshape(h, 1))


def _gelu_tanh(x):
    return 0.5 * x * (1.0 + jnp.tanh(0.7978845608028654 * (x + 0.044715 * (x * x * x))))


def _sgu_kernel(z_ref, ba_ref, ws_ref, bs_ref, gv_ref, bv_ref, o_ref, *, chunk, groups):
    aw = o_ref.shape[1]
    gd = aw // groups
    z = _gelu_tanh(z_ref[...].astype(F32) + ba_ref[...])
    u = z[:, :aw]
    v = _layer_norm(z[:, aw:], gv_ref[...], bv_ref[...], LN_EPS).astype(BF16)
    row = lax.broadcasted_iota(jnp.int32, (chunk, chunk), 0)
    col = lax.broadcasted_iota(jnp.int32, (chunk, chunk), 1)
    causal = row >= col
    bs = bs_ref[...]
    for g in range(groups):
        w_g = jnp.where(causal, ws_ref[g], 0.0).astype(BF16)
        for c in range(z.shape[0] // chunk):
            rs = slice(c * chunk, (c + 1) * chunk)
            cs = slice(g * gd, (g + 1) * gd)
            s = jnp.dot(w_g, v[rs, cs], preferred_element_type=F32) + bs[:, g:g + 1]
            o_ref[rs, cs] = (u[rs, cs] * s).astype(o_ref.dtype)


def _sgu(proj, b_a, w_s, b_s, g_v, b_v, tm=512):
    n = proj.shape[0]
    groups, chunk, _ = w_s.shape
    aw = g_v.shape[0]
    return pl.pallas_call(
        functools.partial(_sgu_kernel, chunk=chunk, groups=groups),
        grid=(n // tm,),
        in_specs=[pl.BlockSpec((tm, 2 * aw), lambda i: (i, 0)), pl.BlockSpec((1, 2 * aw), lambda i: (0, 0)),
                  pl.BlockSpec((groups, chunk, chunk), lambda i: (0, 0, 0)),
                  pl.BlockSpec((chunk, groups), lambda i: (0, 0)), pl.BlockSpec((1, aw), lambda i: (0, 0)),
                  pl.BlockSpec((1, aw), lambda i: (0, 0))],
        out_specs=pl.BlockSpec((tm, aw), lambda i: (i, 0)),
        out_shape=jax.ShapeDtypeStruct((n, aw), BF16),
        compiler_params=_cparams(("parallel",)),
        name="sgu",
    )(proj, b_a.reshape(1, -1), w_s, b_s.T, g_v.reshape(1, aw), b_v.reshape(1, aw))


def _vt_kernel(x_ref, w_ref, wf_ref, o_ref, f_ref):
    x = x_ref[...]
    x_hi = x.astype(BF16)
    x_lo = (x - x_hi.astype(F32)).astype(BF16)
    o_ref[0] = _nt(w_ref[...], x_hi).astype(o_ref.dtype)
    f_ref[0] = _nt(wf_ref[0], x_hi) + _nt(wf_ref[0], x_lo) + _nt(wf_ref[1], x_hi)


def _proj_transposed(x, w_t, wf_t, blk):
    n, d = x.shape
    rows, rows_f = w_t.shape[0], wf_t.shape[0]
    wf_hi = wf_t.astype(BF16)
    wf = jnp.stack([wf_hi, (wf_t - wf_hi.astype(F32)).astype(BF16)])
    return pl.pallas_call(
        _vt_kernel,
        grid=(n // blk,),
        in_specs=[pl.BlockSpec((blk, d), lambda i: (i, 0)), pl.BlockSpec((rows, d), lambda i: (0, 0)),
                  pl.BlockSpec((2, rows_f, d), lambda i: (0, 0, 0))],
        out_specs=[pl.BlockSpec((1, rows, blk), lambda i: (i, 0, 0)),
                   pl.BlockSpec((1, rows_f, blk), lambda i: (i, 0, 0))],
        out_shape=[jax.ShapeDtypeStruct((n // blk, rows, blk), BF16),
                   jax.ShapeDtypeStruct((n // blk, rows_f, blk), F32)],
        compiler_params=_cparams(("parallel",)),
        name="proj_transposed",
    )(x, w_t, wf)


def _fox_kernel(q_ref, k_ref, vt_ref, cq_ref, ck_ref, o_ref, m_scr, l_scr, acc_scr, *, blk, nh):
    qi = pl.program_id(2)
    heads = range(nh)
    hs = [slice(h * LANES, (h + 1) * LANES) for h in heads]
    m_scr[...] = jnp.full_like(m_scr, -jnp.inf)
    l_scr[...] = jnp.zeros_like(l_scr)
    acc_scr[...] = jnp.zeros_like(acc_scr)
    q = [q_ref[:, hs[h]] for h in heads]
    cq = [cq_ref[0, h, pl.ds(qi, 1), :] * LOG2E for h in heads]

    def step(ki, masked):
        ks = pl.multiple_of(ki * blk, blk)
        t = [_nt(k_ref[pl.ds(ks, blk), hs[h]], q[h]) - ck_ref[0, h, pl.ds(ks, blk), :] * LOG2E for h in heads]
        if masked:
            row = lax.broadcasted_iota(jnp.int32, (blk, blk), 0)
            col = lax.broadcasted_iota(jnp.int32, (blk, blk), 1)
            t = [jnp.where(row <= col, x, -jnp.inf) for x in t]
        m_prev = [m_scr[h] for h in heads]
        m_new = [jnp.maximum(m_prev[h], cq[h] + jnp.max(t[h], axis=0, keepdims=True)) for h in heads]
        p = [jnp.exp2(t[h] - (m_new[h] - cq[h])) for h in heads]
        corr = [jnp.exp2(m_prev[h] - m_new[h]) for h in heads]
        for h in heads:
            l_scr[h] = corr[h] * l_scr[h] + jnp.sum(p[h], axis=0, keepdims=True)
            acc_scr[h] = corr[h] * acc_scr[h] + jnp.dot(vt_ref[ki, hs[h], :], p[h].astype(BF16),
                                                         preferred_element_type=F32)
            m_scr[h] = m_new[h]

    def body(ki, carry):
        step(ki, False)
        return carry

    lax.fori_loop(0, qi, body, 0)
    step(qi, True)
    for h in heads:
        o_ref[:, hs[h]] = jnp.transpose(acc_scr[h] / l_scr[h]).astype(o_ref.dtype)


def _fox_attention(proj, vt, c, batch, heads, q_col, blk, nh=4):
    n = proj.shape[0]
    t = n // batch
    nb = t // blk
    dh = LANES
    nh = min(nh, heads)
    w = nh * dh
    q0, k0 = q_col // w, (q_col + heads * dh) // w
    c_col = c.reshape(batch, heads, t, 1)
    c_row = c.reshape(batch, heads, nb, blk)
    return pl.pallas_call(
        functools.partial(_fox_kernel, blk=blk, nh=nh),
        grid=(batch, heads // nh, nb),
        in_specs=[pl.BlockSpec((blk, w), lambda b, h, i: (b * nb + i, q0 + h)),
                  pl.BlockSpec((t, w), lambda b, h, i: (b, k0 + h)),
                  pl.BlockSpec((nb, w, blk), lambda b, h, i: (b, h, 0)),
                  pl.BlockSpec((1, nh, nb, blk), lambda b, h, i: (b, h, 0, 0)),
                  pl.BlockSpec((1, nh, t, 1), lambda b, h, i: (b, h, 0, 0))],
        out_specs=pl.BlockSpec((blk, w), lambda b, h, i: (b * nb + i, h)),
        out_shape=jax.ShapeDtypeStruct((n, heads * dh), BF16),
        scratch_shapes=[pltpu.VMEM((nh, 1, blk), F32), pltpu.VMEM((nh, 1, blk), F32),
                        pltpu.VMEM((nh, dh, blk), F32)],
        compiler_params=_cparams(("parallel", "parallel", "arbitrary")),
        name="fox_attention",
    )(proj, proj, vt, c_row, c_col)


def _first_max(p, idx, valid):
    pm = jnp.where(valid, p, -2.0)
    m = jnp.max(pm, axis=0, keepdims=True)
    first = jnp.min(jnp.where(pm == m, idx, float(p.shape[0])), axis=0, keepdims=True)
    return m, first


def _route_kernel(x_ref, wr_ref, br_ref, meta_ref, cnt_ref, carry):
    @pl.when(pl.program_id(0) == 0)
    def _():
        carry[...] = jnp.zeros_like(carry)

    tm = x_ref.shape[0]
    n_experts = br_ref.shape[0]
    per = n_experts // N_GROUPS
    x = x_ref[...]
    x_hi = x.astype(BF16)
    x_lo = (x - x_hi.astype(F32)).astype(BF16)
    logits = _nt(wr_ref[0], x_hi) + _nt(wr_ref[0], x_lo) + _nt(wr_ref[1], x_hi) + br_ref[...]
    e = jnp.exp(logits - jnp.max(logits, axis=0, keepdims=True))
    probs = e / jnp.sum(e, axis=0, keepdims=True)
    idx = lax.broadcasted_iota(jnp.int32, (n_experts, tm), 0).astype(F32)
    best_score = jnp.full((1, tm), -1.0, F32)
    best_group = jnp.zeros((1, tm), F32)
    for grp in range(N_GROUPS):
        in_g = (idx >= grp * per) & (idx < (grp + 1) * per)
        m1, i1 = _first_max(probs, idx, in_g)
        m2, _ = _first_max(probs, idx, in_g & (idx != i1))
        score = m1 + m2
        take = score > best_score
        best_score = jnp.where(take, score, best_score)
        best_group = jnp.where(take, float(grp), best_group)
    in_sel = (idx >= best_group * per) & (idx < (best_group + 1) * per)
    p1, i1 = _first_max(probs, idx, in_sel)
    p2, i2 = _first_max(probs, idx, in_sel & (idx != i1))
    tot = p1 + p2
    row = lax.broadcasted_iota(jnp.int32, (tm, tm), 0)
    col = lax.broadcasted_iota(jnp.int32, (tm, tm), 1)
    earlier = jnp.where(row < col, 1.0, 0.0).astype(BF16)
    onehot = jnp.where((idx == i1) | (idx == i2), 1.0, 0.0)
    seen = jnp.dot(onehot.astype(BF16), earlier, preferred_element_type=F32) + carry[...]
    r1 = jnp.sum(jnp.where(idx == i1, seen, 0.0), axis=0, keepdims=True)
    r2 = jnp.sum(jnp.where(idx == i2, seen, 0.0), axis=0, keepdims=True)
    zero = jnp.zeros_like(r1)
    meta_ref[...] = jnp.concatenate([i1, i2, p1 / tot, p2 / tot, r1, r2, zero, zero], axis=0)
    carry[...] = carry[...] + jnp.sum(onehot, axis=1, keepdims=True)
    cnt_ref[...] = jnp.broadcast_to(carry[...], cnt_ref.shape)


def _route(x, w_router, b_router, tm=512):
    n, d = x.shape
    n_experts = w_router.shape[1]
    wr = w_router.T
    wr_hi = wr.astype(BF16)
    wr = jnp.stack([wr_hi, (wr - wr_hi.astype(F32)).astype(BF16)])
    return pl.pallas_call(
        _route_kernel,
        grid=(n // tm,),
        in_specs=[pl.BlockSpec((tm, d), lambda i: (i, 0)), pl.BlockSpec((2, n_experts, d), lambda i: (0, 0, 0)),
                  pl.BlockSpec((n_experts, 1), lambda i: (0, 0))],
        out_specs=[pl.BlockSpec((8, tm), lambda i: (0, i)), pl.BlockSpec((n_experts, LANES), lambda i: (0, 0))],
        out_shape=[jax.ShapeDtypeStruct((8, n), F32), jax.ShapeDtypeStruct((n_experts, LANES), F32)],
        scratch_shapes=[pltpu.VMEM((n_experts, 1), F32)],
        compiler_params=_cparams(("arbitrary",)),
        name="route",
    )(x, wr, b_router.reshape(n_experts, 1))


def _row_copy(src_ref, src_row, dst_ref, dst_row, sem):
    return pltpu.make_async_copy(src_ref.at[pl.ds(src_row, 1)], dst_ref.at[pl.ds(dst_row, 1)], sem)


def _dispatch_kernel(zid_ref, pos_ref, x_ref, xs_ref, zbuf, sem, zsem, *, tm):
    def zero_tiles(fn):
        for i in range(zid_ref.shape[0]):
            new_id = zid_ref[i] != zid_ref[max(i - 1, 0)] if i else True

            @pl.when(new_id)
            def _():
                fn(pltpu.make_async_copy(zbuf, xs_ref.at[pl.ds(zid_ref[i] * tm, tm)], zsem))

    @pl.when(pl.program_id(0) == 0)
    def _():
        zbuf[...] = jnp.zeros_like(zbuf)
        zero_tiles(lambda cp: cp.start())
        zero_tiles(lambda cp: cp.wait())

    def rows(fn):
        for r in range(x_ref.shape[0]):
            for s in range(pos_ref.shape[0]):
                fn(_row_copy(x_ref, r, xs_ref, pos_ref[s, r], sem))

    rows(lambda cp: cp.start())
    rows(lambda cp: cp.wait())


def _dispatch(x, pos, zero_ids, n_rows, tm):
    n, d = x.shape
    tok = 256
    grid_spec = pltpu.PrefetchScalarGridSpec(
        num_scalar_prefetch=1,
        grid=(n // tok,),
        in_specs=[pl.BlockSpec((pos.shape[0], tok), lambda i, z: (0, i), memory_space=pltpu.SMEM),
                  pl.BlockSpec((tok, d), lambda i, z: (i, 0))],
        out_specs=pl.BlockSpec(memory_space=pl.ANY),
        scratch_shapes=[pltpu.VMEM((tm, d), x.dtype), pltpu.SemaphoreType.DMA, pltpu.SemaphoreType.DMA],
    )
    return pl.pallas_call(
        functools.partial(_dispatch_kernel, tm=tm),
        grid_spec=grid_spec,
        out_shape=jax.ShapeDtypeStruct((n_rows, d), x.dtype),
        compiler_params=_cparams(("arbitrary",)),
        name="moe_dispatch",
    )(zero_ids, pos, x)


def _experts_kernel(te_ref, nx_ref, na_ref, xs_ref, wgu_hbm, wd_hbm, ys_ref, wgu_f, wd_f, wgu_b, wd_b, sems, *, layer):
    j = pl.program_id(0)
    live = j < na_ref[0]
    fresh = (j == 0) | (te_ref[j] != te_ref[jnp.maximum(j - 1, 0)])

    def fetch(e):
        return (pltpu.make_async_copy(wgu_hbm.at[layer, e], wgu_f, sems.at[0]),
                pltpu.make_async_copy(wd_hbm.at[layer, e], wd_f, sems.at[1]))

    @pl.when(live & (j == 0))
    def _():
        for cp in fetch(te_ref[0]):
            cp.start()

    @pl.when(live & fresh)
    def _():
        for cp in fetch(te_ref[j]):
            cp.wait()
        wgu_b[...] = wgu_f[...].astype(BF16)
        wd_b[...] = wd_f[...].astype(BF16)

    @pl.when(live & fresh & (nx_ref[j] >= 0))
    def _():
        for cp in fetch(nx_ref[j]):
            cp.start()

    @pl.when(live)
    def _():
        de = wd_b.shape[0]
        gu = jnp.dot(xs_ref[...].astype(BF16), wgu_b[...], preferred_element_type=F32)
        gpart, upart = gu[:, :de], gu[:, de:]
        h = (gpart / (1.0 + jnp.exp(-gpart))) * upart
        ys_ref[...] = jnp.dot(h.astype(BF16), wd_b[...], preferred_element_type=F32)

    @pl.when(jnp.logical_not(live))
    def _():
        ys_ref[...] = jnp.zeros_like(ys_ref)


def _experts(xs, w_gu, w_down, layer, tile_expert, next_expert, n_active, tm):
    p, d = xs.shape
    de = w_down.shape[2]
    live = lambda j, na: jnp.maximum(jnp.minimum(j, na[0] - 1), 0)
    grid_spec = pltpu.PrefetchScalarGridSpec(
        num_scalar_prefetch=3,
        grid=(p // tm,),
        in_specs=[pl.BlockSpec((tm, d), lambda j, te, nx, na: (live(j, na), 0)),
                  pl.BlockSpec(memory_space=pl.ANY), pl.BlockSpec(memory_space=pl.ANY)],
        out_specs=pl.BlockSpec((tm, d), lambda j, te, nx, na: (j, 0)),
        scratch_shapes=[pltpu.VMEM((d, 2 * de), F32), pltpu.VMEM((de, d), F32), pltpu.VMEM((d, 2 * de), BF16),
                        pltpu.VMEM((de, d), BF16), pltpu.SemaphoreType.DMA((2,))],
    )
    return pl.pallas_call(
        functools.partial(_experts_kernel, layer=layer),
        grid_spec=grid_spec,
        out_shape=jax.ShapeDtypeStruct((p, d), F32),
        compiler_params=_cparams(("arbitrary",)),
        name="moe_experts",
    )(tile_expert, next_expert, n_active, xs, w_gu, w_down)


def _combine_kernel(pos_ref, x_ref, gw_ref, ys_ref, g_ref, b_ref, o_ref, buf, sems, *, alpha):
    tm = x_ref.shape[0]
    n_slots = pos_ref.shape[0]
    half = tm // 2

    def rows(h, fn):
        for r in range(h * half, (h + 1) * half):
            for s in range(n_slots):
                fn(_row_copy(ys_ref, pos_ref[s, r], buf.at[s], r, sems.at[h]))

    for h in range(2):
        rows(h, lambda cp: cp.start())
    for h in range(2):
        rows(h, lambda cp: cp.wait())
        rs = pl.ds(h * half, half)
        gw = gw_ref[rs, :]
        y = alpha * x_ref[rs, :]
        for s in range(n_slots):
            y = y + gw[:, s:s + 1] * buf[s, rs, :]
        o_ref[rs, :] = _layer_norm(y, g_ref[...], b_ref[...], LN_EPS)


def _combine_ln(x, ys, pos, gw, g, b, alpha, tm=256):
    n, d = x.shape
    n_slots = pos.shape[0]
    return pl.pallas_call(
        functools.partial(_combine_kernel, alpha=alpha),
        grid=(n // tm,),
        in_specs=[pl.BlockSpec((n_slots, tm), lambda i: (0, i), memory_space=pltpu.SMEM),
                  pl.BlockSpec((tm, d), lambda i: (i, 0)), pl.BlockSpec((tm, n_slots), lambda i: (i, 0)),
                  pl.BlockSpec(memory_space=pl.ANY), pl.BlockSpec((1, d), lambda i: (0, 0)),
                  pl.BlockSpec((1, d), lambda i: (0, 0))],
        out_specs=pl.BlockSpec((tm, d), lambda i: (i, 0)),
        out_shape=jax.ShapeDtypeStruct((n, d), F32),
        scratch_shapes=[pltpu.VMEM((n_slots, tm, d), F32), pltpu.SemaphoreType.DMA((2,))],
        compiler_params=_cparams(("arbitrary",)),
        name="moe_combine",
    )(pos, x, gw, ys, g.reshape(1, d), b.reshape(1, d))


def _moe_ln(x, w_router, b_router, w_gu, w_down, layer, g, b, alpha, tm=256):
    n, d = x.shape
    n_experts = w_down.shape[1]
    meta, counts = _route(x, w_router, b_router)
    eid = meta[0:2].astype(jnp.int32)
    gw = meta[2:4].T
    rank = meta[4:6].astype(jnp.int32)
    counts = counts[:, 0].astype(jnp.int32)
    tiles = (counts + tm - 1) // tm
    tile_end = jnp.cumsum(tiles)
    row_off = (tile_end - tiles) * tm
    n_tiles = (eid.size + tm - 1) // tm + n_experts
    tile_expert = jnp.sum(jnp.arange(n_tiles)[:, None] >= tile_end[None, :], axis=1).astype(jnp.int32)
    tile_expert = jnp.minimum(tile_expert, n_experts - 1)
    n_active = tile_end[-1:].astype(jnp.int32)
    ids = jnp.arange(n_experts)
    later = (ids[None, :] > ids[:, None]) & (tiles[None, :] > 0)
    next_of = jnp.min(jnp.where(later, ids[None, :], n_experts), axis=1)
    next_of = jnp.where(next_of < n_experts, next_of, -1).astype(jnp.int32)
    lookup = lambda table, ids: jnp.sum(jnp.where(ids[..., None] == jnp.arange(n_experts), table, 0), axis=-1)
    next_expert = lookup(next_of, tile_expert)
    pos = lookup(row_off, eid) + rank
    last_tile = jnp.maximum(tile_end - 1, 0)
    idle_tile = jnp.minimum(tile_end[-1] + jnp.arange(n_experts), n_tiles - 1)
    zero_ids = jnp.concatenate([last_tile, idle_tile]).astype(jnp.int32)
    xs = _dispatch(x, pos, zero_ids, n_tiles * tm, tm)
    ys = _experts(xs, w_gu, w_down, layer, tile_expert, next_expert, n_active, tm)
    return _combine_ln(x, ys, pos, gw, g, b, alpha)


def _token_shift(x_ref, prev_ref, first):
    x = x_ref[...]
    prev_row = jnp.where(first, 0.0, prev_ref[7:8, :])
    row = lax.broadcasted_iota(jnp.int32, x.shape, 0)
    return x, jnp.where(row == 0, prev_row, pltpu.roll(x, 1, 0))


def _rkv_kernel(x_ref, prev_ref, mu_ref, w_ref, o_ref, *, tiles_per_seq):
    first = pl.program_id(0) % tiles_per_seq == 0
    x, xp = _token_shift(x_ref, prev_ref, first)
    xm = x + (xp - x) * mu_ref[0]
    o_ref[0] = jnp.dot(xm.astype(BF16), w_ref[0], preferred_element_type=F32).astype(o_ref.dtype)


def _rkv_proj(x, mu3, w_rkv, t, tm=512):
    n, d = x.shape
    sub = tm // 8
    return pl.pallas_call(
        functools.partial(_rkv_kernel, tiles_per_seq=t // tm),
        grid=(n // tm, 3),
        in_specs=[pl.BlockSpec((tm, d), lambda i, j: (i, 0)),
                  pl.BlockSpec((8, d), lambda i, j: (jnp.maximum(i * sub - 1, 0), 0)),
                  pl.BlockSpec((1, 1, d), lambda i, j: (j, 0, 0)), pl.BlockSpec((1, d, d), lambda i, j: (j, 0, 0))],
        out_specs=pl.BlockSpec((1, tm, d), lambda i, j: (j, i, 0)),
        out_shape=jax.ShapeDtypeStruct((3, n, d), BF16),
        compiler_params=_cparams(("parallel", "arbitrary")),
        name="rkv_proj",
    )(x, x, mu3.reshape(3, 1, d), w_rkv)


def _lora_kernel(x_ref, prev_ref, mu_ref, w0_ref, w1_ref, w2_ref, a0_ref, a1_ref, a2_ref, g1_ref, g2_ref, wl_ref,
                 a_ref, g_ref, *, tiles_per_seq):
    first = pl.program_id(0) % tiles_per_seq == 0
    x, xp = _token_shift(x_ref, prev_ref, first)
    dx = xp - x

    def mm(a, w_ref):
        return jnp.dot(a.astype(BF16), w_ref[...], preferred_element_type=F32)

    zw = w0_ref[...] + mm(jnp.tanh(mm(x + dx * mu_ref[0:1, :], w1_ref)), w2_ref)
    wl_ref[...] = -HALF_DECAY * _sigmoid(zw)
    za = a0_ref[...] + mm(mm(x + dx * mu_ref[1:2, :], a1_ref), a2_ref)
    a_ref[...] = _sigmoid(za).astype(a_ref.dtype)
    zg = mm(x + dx * mu_ref[2:3, :], g1_ref)
    g_ref[...] = mm(_sigmoid(zg), g2_ref).astype(g_ref.dtype)


def _lora(x, mu3, w0, w1, w2, a0, a1, a2, g1, g2, t, tm=512):
    n, d = x.shape
    sub = tm // 8
    const = lambda shape: pl.BlockSpec(shape, lambda i: (0, 0))
    row = pl.BlockSpec((tm, d), lambda i: (i, 0))
    return pl.pallas_call(
        functools.partial(_lora_kernel, tiles_per_seq=t // tm),
        grid=(n // tm,),
        in_specs=[row, pl.BlockSpec((8, d), lambda i: (jnp.maximum(i * sub - 1, 0), 0)), const((3, d)),
                  const((1, d)), const(w1.shape), const(w2.shape), const((1, d)), const(a1.shape), const(a2.shape),
                  const(g1.shape), const(g2.shape)],
        out_specs=[row, row, row],
        out_shape=[jax.ShapeDtypeStruct((n, d), F32), jax.ShapeDtypeStruct((n, d), BF16),
                   jax.ShapeDtypeStruct((n, d), BF16)],
        compiler_params=_cparams(("parallel",)),
        name="lora",
    )(x, x, mu3, w0.reshape(1, d), w1, w2, a0.reshape(1, d), a1, a2, g1, g2)


def _scan_groups(r, k, v, lw, a_gate, g, kk_w, ka_w, rk_w, gg, gb, s0, *, hd, gn_eps):
    c = r[0].shape[0]
    nh = LANES // hd
    hc = nh * c
    each = lambda f, *xs: [f(*x) for x in zip(*xs)]
    lane = lax.broadcasted_iota(jnp.int32, (1, LANES), 1)
    head_masks = [(lane >= h * hd) & (lane < (h + 1) * hd) for h in range(nh)]
    lr = lax.broadcasted_iota(jnp.int32, (LANES, LANES), 0) // hd
    lc = lax.broadcasted_iota(jnp.int32, (LANES, LANES), 1) // hd
    same_head = jnp.where(lr == lc, 1.0, 0.0).astype(BF16)
    trow = lax.broadcasted_iota(jnp.int32, (c, c), 0)
    tcol = lax.broadcasted_iota(jnp.int32, (c, c), 1)
    lower = jnp.where(trow >= tcol, 1.0, 0.0).astype(BF16)
    prow = lax.broadcasted_iota(jnp.int32, (hc, hc), 0)
    pcol = lax.broadcasted_iota(jnp.int32, (hc, hc), 1)
    same_blk = (prow // c) == (pcol // c)
    strict = same_blk & (prow > pcol)
    incl = same_blk & (prow >= pcol)
    eye = jnp.where(prow == pcol, 1.0, 0.0).astype(F32)

    def split(x):
        hi = x.astype(BF16)
        return hi, (x - hi.astype(F32)).astype(BF16)

    def head_sum(xs):
        return [jnp.dot(x.astype(BF16), same_head, preferred_element_type=F32) for x in xs]

    def per_head(x):
        return jnp.concatenate([jnp.where(hm, x, 0.0) for hm in head_masks], axis=0).astype(BF16)

    kk = each(lambda k_, w_: k_ * w_, k, kk_w)
    kk_n = head_sum(each(lambda x: x * x, kk))
    kk = each(lambda x, n_: x / jnp.maximum(jnp.sqrt(n_), 1e-12), kk, kk_n)
    k2 = each(lambda k_, a_, w_: k_ * (1.0 + (a_ - 1.0) * w_), k, a_gate, ka_w)
    lw_parts = each(split, lw)
    cum = [jnp.dot(lower, hi, preferred_element_type=F32) + jnp.dot(lower, lo, preferred_element_type=F32)
           for hi, lo in lw_parts]
    gam = each(jnp.exp, cum)
    inv_gam = each(lambda x: jnp.exp(-x), cum)
    gam_end = each(lambda x: x[c - 1:c, :], gam)
    a_t = each(lambda kk_, cum_, lw_: -kk_ * jnp.exp(cum_ - lw_), kk, cum, lw)
    b_t = each(lambda kk_, a_, ig: kk_ * a_ * ig, kk, a_gate, inv_gam)
    k_t = each(lambda k2_, ig: k2_ * ig, k2, inv_gam)
    r_t = each(lambda r_, gm: r_ * gm, r, gam)

    s0b = each(lambda x: x.astype(BF16), s0)
    ar_s0 = each(lambda a_, r_, s_: _nt(jnp.concatenate([a_, r_], axis=0).astype(BF16), s_), a_t, r_t, s0b)
    ar2 = each(lambda a_, r_: jnp.concatenate([per_head(a_), per_head(r_)], axis=0), a_t, r_t)
    bk2 = each(lambda b_, k_: jnp.concatenate([per_head(b_), per_head(k_)], axis=0), b_t, k_t)
    v2 = each(per_head, v)
    gmat = each(_nt, ar2, bk2)
    a_ab = each(lambda x: jnp.where(strict, x[:hc, :hc], 0.0), gmat)
    a_ak = each(lambda x: jnp.where(strict, x[:hc, hc:], 0.0), gmat)
    a_r = each(lambda x: jnp.concatenate([jnp.where(incl, x[hc:, :hc], 0.0), jnp.where(incl, x[hc:, hc:], 0.0)],
                                         axis=1).astype(BF16), gmat)
    rhs = each(lambda as0, ak, v_: jnp.concatenate([jnp.where(hm, as0[:c], 0.0) for hm in head_masks], axis=0)
               + _bdot(ak, v_), ar_s0, a_ak, v2)
    inv = each(lambda x: eye + x, a_ab)
    m = each(lambda x: _bdot(x, x), a_ab)
    n_pow = 2
    while 2 * n_pow < c:
        mp = each(lambda m_, p_: _bdot(jnp.concatenate([m_, p_], axis=0), m_), m, inv)
        inv = each(lambda p_, mp_: p_ + mp_[hc:], inv, mp)
        m = each(lambda mp_: mp_[:hc], mp)
        n_pow *= 2
    rhs = each(lambda rhs_, m_: rhs_ + _bdot(m_, rhs_), rhs, m)
    u2 = each(_bdot, inv, rhs)
    uv = each(lambda u_, v_: jnp.concatenate([u_.astype(BF16), v_], axis=0), u2, v2)
    y2 = each(lambda ar_, uv_: jnp.dot(ar_, uv_, preferred_element_type=F32), a_r, uv)
    y = each(lambda rs, y2_: rs[c:] + sum(y2_[h * c:(h + 1) * c, :] for h in range(nh)), ar_s0, y2)
    bkg = each(lambda b_, k_, ge: jnp.concatenate([per_head(b_ * ge), per_head(k_ * ge)], axis=0), b_t, k_t, gam_end)
    s_new = each(lambda s_, ge, uv_, bkg_: s_ * ge + _tn(uv_, bkg_), s0, gam_end, uv, bkg)

    inv_hd = 1.0 / hd
    mean = head_sum(y)
    yc = each(lambda y_, m_: y_ - m_ * inv_hd, y, mean)
    var = head_sum(each(lambda x: x * x, yc))
    yn = each(lambda yc_, var_, gg_, gb_: yc_ * lax.rsqrt(var_ * inv_hd + gn_eps) * gg_ + gb_, yc, var, gg, gb)
    rk_sum = head_sum(each(lambda r_, k2_, w_: r_ * k2_ * w_, r, k2, rk_w))
    out = each(lambda yn_, rk_, v_, g_: (yn_ + rk_ * v_) * g_, yn, rk_sum, v, g)
    return out, s_new


def _scan_kernel(r_ref, k_ref, v_ref, wl_ref, a_ref, g_ref, kk_ref, ka_ref, rk_ref, gg_ref, gb_ref, o_ref, state, *,
                 hd, gn_eps):
    @pl.when(pl.program_id(2) == 0)
    def _():
        state[...] = jnp.zeros_like(state)

    ng = state.shape[0]
    sls = [slice(p * LANES, (p + 1) * LANES) for p in range(ng)]
    tok3 = lambda ref: [ref[0, :, sl].astype(F32) for sl in sls]
    tok2 = lambda ref: [ref[:, sl].astype(F32) for sl in sls]
    out, s_new = _scan_groups(tok3(r_ref), tok3(k_ref), tok3(v_ref), tok2(wl_ref), tok2(a_ref), tok2(g_ref),
                              tok2(kk_ref), tok2(ka_ref), tok2(rk_ref), tok2(gg_ref), tok2(gb_ref),
                              [state[p] for p in range(ng)], hd=hd, gn_eps=gn_eps)
    for p in range(ng):
        state[p] = s_new[p]
        o_ref[:, sls[p]] = out[p].astype(o_ref.dtype)


def _rwkv_scan(rkv, wl, a, g, k_k, k_a, r_k, gn_g, gn_b, batch, hd, gn_eps, chunk=64, groups=16):
    _, n, d = rkv.shape
    t = n // batch
    nc = t // chunk
    groups = min(groups, d // LANES)
    w = groups * LANES
    tok = lambda j: pl.BlockSpec((1, chunk, w), functools.partial(lambda b, p, c, j: (j, b * nc + c, p), j=j))
    tok2 = pl.BlockSpec((chunk, w), lambda b, p, c: (b * nc + c, p))
    par = pl.BlockSpec((1, w), lambda b, p, c: (0, p))
    return pl.pallas_call(
        functools.partial(_scan_kernel, hd=hd, gn_eps=gn_eps),
        grid=(batch, d // w, nc),
        in_specs=[tok(0), tok(1), tok(2), tok2, tok2, tok2, par, par, par, par, par],
        out_specs=tok2,
        out_shape=jax.ShapeDtypeStruct((n, d), BF16),
        scratch_shapes=[pltpu.VMEM((groups, LANES, LANES), F32)],
        compiler_params=_cparams(("parallel", "parallel", "arbitrary")),
        name="rwkv_scan",
    )(rkv, rkv, rkv, wl, a, g, k_k.reshape(1, d), k_a.reshape(1, d), r_k.reshape(1, d), gn_g.reshape(1, d),
      gn_b.reshape(1, d))


def _pad_lora(w_in, w_out):
    r = w_in.shape[1]
    rp = -(-r // LANES) * LANES
    return (jnp.pad(w_in, ((0, 0), (0, rp - r))).astype(BF16), jnp.pad(w_out, ((0, rp - r), (0, 0))).astype(BF16))


def kernel(x, ev_w_in, ev_b_a, ev_w_s, ev_b_s, ev_g_v, ev_b_v, ev_b_f, ev_w_out, rw_mu, rw_w_rkv, rw_w0, rw_w1, rw_w2, rw_a0, rw_a1, rw_a2, rw_g1, rw_g2, rw_k_k, rw_k_a, rw_r_k, rw_gn_g, rw_gn_b, rw_w_o, ln_g, ln_b, w_router, b_router, w_gu, w_down):
    batch, t, d = x.shape
    depth = ln_g.shape[0]
    alpha = (2 * depth) ** 0.25
    h = x.reshape(batch * t, d)
    for layer in range(depth):
        i = layer // 2
        if layer % 2 == 0:
            aw = ev_g_v.shape[1]
            heads = ev_b_f.shape[1]
            q_col = 2 * aw
            bw = heads * LANES
            k_col, v_col, f_col = q_col + bw, q_col + 2 * bw, q_col + 3 * bw
            w_in = ev_w_in[i]
            col = jnp.arange(v_col)
            q_scale = jnp.where((col >= q_col) & (col < k_col), LANES ** -0.5 * LOG2E, 1.0)
            proj = _matmul(h, (w_in[:, :v_col] * q_scale).astype(BF16), v_col, BF16)
            vt, f_logit = _proj_transposed(h, w_in[:, v_col:f_col].T.astype(BF16), w_in[:, f_col:].T, FOX_BLOCK)
            c = _fgate(f_logit, ev_b_f[i], batch)
            y_a = _sgu(proj, ev_b_a[i], ev_w_s[i], ev_b_s[i], ev_g_v[i], ev_b_v[i])
            y_b = _fox_attention(proj, vt, c, batch, heads, q_col, FOX_BLOCK)
            h = _proj_ln([y_a, y_b], ev_w_out[i].astype(BF16), h, ln_g[layer, 0], ln_b[layer, 0], alpha)
        else:
            hd = rw_r_k.shape[2]
            mu = rw_mu[i]
            rkv = _rkv_proj(h, mu[:3], rw_w_rkv[i].astype(BF16), t)
            w1, w2 = _pad_lora(rw_w1[i], rw_w2[i])
            a1, a2 = _pad_lora(rw_a1[i], rw_a2[i])
            g1, g2 = _pad_lora(rw_g1[i], rw_g2[i])
            wl, a, g = _lora(h, mu[3:], rw_w0[i], w1, w2, rw_a0[i], a1, a2, g1, g2, t)
            y = _rwkv_scan(rkv, wl, a, g, rw_k_k[i], rw_k_a[i], rw_r_k[i].reshape(-1), rw_gn_g[i], rw_gn_b[i],
                           batch, hd, hd * 1e-5)
            h = _proj_ln([y], rw_w_o[i].astype(BF16), h, ln_g[layer, 0], ln_b[layer, 0], alpha)
        h = _moe_ln(h, w_router, b_router, w_gu, w_down, layer, ln_g[layer, 1], ln_b[layer, 1], alpha)
    return h.reshape(batch, t, d)
```

```python
import functools

import jax
import jax.numpy as jnp
from jax import lax
from jax.experimental import pallas as pl
from jax.experimental.pallas import tpu as pltpu

F32 = jnp.float32
BF16 = jnp.bfloat16
HI = lax.Precision.HIGHEST

LN_EPS = 1e-5
N_GROUPS = 4
LANES = 128
FOX_BLOCK = 512
LOG2E = 1.4426950408889634
HALF_DECAY = 0.6065306597126334
DMA_QUEUES = 2
VMEM_LIMIT = 56 * 1024 * 1024


def _cparams(sem):
    return pltpu.CompilerParams(dimension_semantics=sem, vmem_limit_bytes=VMEM_LIMIT)


def _layer_norm(x, g, b, eps):
    mu = jnp.mean(x, -1, keepdims=True)
    xc = x - mu
    var = jnp.mean(xc * xc, -1, keepdims=True)
    return xc * lax.rsqrt(var + eps) * g + b


def _sigmoid(x):
    return 0.5 + 0.5 * jnp.tanh(0.5 * x)


def _nt(a, b, **kw):
    return lax.dot_general(a, b, (((1,), (1,)), ((), ())), preferred_element_type=F32, **kw)


def _tn(a, b, **kw):
    return lax.dot_general(a, b, (((0,), (0,)), ((), ())), preferred_element_type=F32, **kw)


def _bdot(a, b):
    return jnp.dot(a.astype(BF16), b.astype(BF16), preferred_element_type=F32)


def _mm_kernel(a_ref, w_ref, o_ref, a_bf16):
    @pl.when(pl.program_id(1) == 0)
    def _():
        a_bf16[...] = a_ref[...].astype(BF16)

    o_ref[...] = jnp.dot(a_bf16[...], w_ref[...], preferred_element_type=F32).astype(o_ref.dtype)


def _matmul(a, w, n_cols, out_dtype, tm=1024, tn=1024):
    m, k = a.shape
    tm = min(tm, m)
    tn = next(c for c in (tn, 256, LANES) if n_cols % c == 0)
    return pl.pallas_call(
        _mm_kernel,
        grid=(m // tm, n_cols // tn),
        in_specs=[pl.BlockSpec((tm, k), lambda i, j: (i, 0)), pl.BlockSpec((k, tn), lambda i, j: (0, j))],
        out_specs=pl.BlockSpec((tm, tn), lambda i, j: (i, j)),
        out_shape=jax.ShapeDtypeStruct((m, n_cols), out_dtype),
        scratch_shapes=[pltpu.VMEM((tm, k), BF16)],
        compiler_params=_cparams(("parallel", "arbitrary")),
        name="matmul",
    )(a, w)


def _proj_ln_kernel(*refs, n_in, alpha):
    a_refs, w_refs = refs[:n_in], refs[n_in:2 * n_in]
    res_ref, g_ref, b_ref, o_ref = refs[2 * n_in:]
    acc = alpha * res_ref[...]
    for a_ref, w_ref in zip(a_refs, w_refs):
        acc = acc + jnp.dot(a_ref[...], w_ref[...], preferred_element_type=F32)
    o_ref[...] = _layer_norm(acc, g_ref[...], b_ref[...], LN_EPS)


def _proj_ln(a_list, w, res, g, b, alpha, tm=512):
    m, d = res.shape
    n_in = len(a_list)
    kc = a_list[0].shape[1]
    in_specs = [pl.BlockSpec((tm, kc), lambda i: (i, 0)) for _ in a_list]
    in_specs += [pl.BlockSpec((kc, d), functools.partial(lambda i, r: (r, 0), r=r)) for r in range(n_in)]
    in_specs += [pl.BlockSpec((tm, d), lambda i: (i, 0)), pl.BlockSpec((1, d), lambda i: (0, 0)),
                 pl.BlockSpec((1, d), lambda i: (0, 0))]
    return pl.pallas_call(
        functools.partial(_proj_ln_kernel, n_in=n_in, alpha=alpha),
        grid=(m // tm,),
        in_specs=in_specs,
        out_specs=pl.BlockSpec((tm, d), lambda i: (i, 0)),
        out_shape=jax.ShapeDtypeStruct((m, d), F32),
        compiler_params=_cparams(("parallel",)),
        name="proj_ln",
    )(*a_list, *([w] * n_in), res, g.reshape(1, d), b.reshape(1, d))


def _fgate_kernel(z_ref, bf_ref, c_ref, carry):
    @pl.when(pl.program_id(1) == 0)
    def _():
        carry[...] = jnp.zeros_like(carry)

    tm = z_ref.shape[2]
    z = z_ref[0] + bf_ref[...]
    log_f = jnp.minimum(z, 0.0) - jnp.log1p(jnp.exp(-jnp.abs(z)))
    row = lax.broadcasted_iota(jnp.int32, (tm, tm), 0)
    col = lax.broadcasted_iota(jnp.int32, (tm, tm), 1)
    upper = jnp.where(row <= col, 1.0, 0.0).astype(F32)
    c = jnp.dot(log_f, upper, preferred_element_type=F32, precision=HI) + carry[...]
    c_ref[0] = c
    carry[...] = carry[...] + jnp.sum(log_f, axis=-1, keepdims=True)


def _fgate(f_logit, b_f, batch):
    tiles, h, tm = f_logit.shape
    nt = tiles // batch
    return pl.pallas_call(
        _fgate_kernel,
        grid=(batch, nt),
        in_specs=[pl.BlockSpec((1, h, tm), lambda b, i: (b * nt + i, 0, 0)), pl.BlockSpec((h, 1), lambda b, i: (0, 0))],
        out_specs=pl.BlockSpec((1, h, tm), lambda b, i: (b, 0, i)),
        out_shape=jax.ShapeDtypeStruct((batch, h, nt * tm), F32),
        scratch_shapes=[pltpu.VMEM((h, 1), F32)],
        compiler_params=_cparams(("parallel", "arbitrary")),
        name="fgate",
    )(f_logit, b_f.reshape(h, 1))


def _gelu_tanh(x):
    return 0.5 * x * (1.0 + jnp.tanh(0.7978845608028654 * (x + 0.044715 * (x * x * x))))


def _sgu_kernel(z_ref, ba_ref, ws_ref, bs_ref, gv_ref, bv_ref, o_ref, *, chunk, groups):
    aw = o_ref.shape[1]
    gd = aw // groups
    z = _gelu_tanh(z_ref[...].astype(F32) + ba_ref[...])
    u = z[:, :aw]
    v = _layer_norm(z[:, aw:], gv_ref[...], bv_ref[...], LN_EPS).astype(BF16)
    row = lax.broadcasted_iota(jnp.int32, (chunk, chunk), 0)
    col = lax.broadcasted_iota(jnp.int32, (chunk, chunk), 1)
    causal = row >= col
    bs = bs_ref[...]
    for g in range(groups):
        w_g = jnp.where(causal, ws_ref[g], 0.0).astype(BF16)
        for c in range(z.shape[0] // chunk):
            rs = slice(c * chunk, (c + 1) * chunk)
            cs = slice(g * gd, (g + 1) * gd)
            s = jnp.dot(w_g, v[rs, cs], preferred_element_type=F32) + bs[:, g:g + 1]
            o_ref[rs, cs] = (u[rs, cs] * s).astype(o_ref.dtype)


def _sgu(proj, b_a, w_s, b_s, g_v, b_v, tm=512):
    n = proj.shape[0]
    groups, chunk, _ = w_s.shape
    aw = g_v.shape[0]
    return pl.pallas_call(
        functools.partial(_sgu_kernel, chunk=chunk, groups=groups),
        grid=(n // tm,),
        in_specs=[pl.BlockSpec((tm, 2 * aw), lambda i: (i, 0)), pl.BlockSpec((1, 2 * aw), lambda i: (0, 0)),
                  pl.BlockSpec((groups, chunk, chunk), lambda i: (0, 0, 0)),
                  pl.BlockSpec((chunk, groups), lambda i: (0, 0)), pl.BlockSpec((1, aw), lambda i: (0, 0)),
                  pl.BlockSpec((1, aw), lambda i: (0, 0))],
        out_specs=pl.BlockSpec((tm, aw), lambda i: (i, 0)),
        out_shape=jax.ShapeDtypeStruct((n, aw), BF16),
        compiler_params=_cparams(("parallel",)),
        name="sgu",
    )(proj, b_a.reshape(1, -1), w_s, b_s.T, g_v.reshape(1, aw), b_v.reshape(1, aw))


def _vt_kernel(x_ref, w_ref, wf_ref, o_ref, f_ref):
    x = x_ref[...]
    x_hi = x.astype(BF16)
    x_lo = (x - x_hi.astype(F32)).astype(BF16)
    o_ref[0] = _nt(w_ref[...], x_hi).astype(o_ref.dtype)
    f_ref[0] = _nt(wf_ref[0], x_hi) + _nt(wf_ref[0], x_lo) + _nt(wf_ref[1], x_hi)


def _proj_transposed(x, w_t, wf_t, blk):
    n, d = x.shape
    rows, rows_f = w_t.shape[0], wf_t.shape[0]
    wf_hi = wf_t.astype(BF16)
    wf = jnp.stack([wf_hi, (wf_t - wf_hi.astype(F32)).astype(BF16)])
    return pl.pallas_call(
        _vt_kernel,
        grid=(n // blk,),
        in_specs=[pl.BlockSpec((blk, d), lambda i: (i, 0)), pl.BlockSpec((rows, d), lambda i: (0, 0)),
                  pl.BlockSpec((2, rows_f, d), lambda i: (0, 0, 0))],
        out_specs=[pl.BlockSpec((1, rows, blk), lambda i: (i, 0, 0)),
                   pl.BlockSpec((1, rows_f, blk), lambda i: (i, 0, 0))],
        out_shape=[jax.ShapeDtypeStruct((n // blk, rows, blk), BF16),
                   jax.ShapeDtypeStruct((n // blk, rows_f, blk), F32)],
        compiler_params=_cparams(("parallel",)),
        name="proj_transposed",
    )(x, w_t, wf)


def _fox_kernel(q_ref, k_ref, vt_ref, cq_ref, ck_ref, o_ref, m_scr, l_scr, acc_scr, *, blk, nh):
    qi = pl.program_id(2)
    heads = range(nh)
    hs = [slice(h * LANES, (h + 1) * LANES) for h in heads]
    m_scr[...] = jnp.full_like(m_scr, -jnp.inf)
    l_scr[...] = jnp.zeros_like(l_scr)
    acc_scr[...] = jnp.zeros_like(acc_scr)
    q = [q_ref[:, hs[h]] for h in heads]
    cq = [cq_ref[0, h, pl.ds(qi, 1), :] * LOG2E for h in heads]

    def step(ki, masked):
        ks = pl.multiple_of(ki * blk, blk)
        t = [_nt(k_ref[pl.ds(ks, blk), hs[h]], q[h]) - ck_ref[0, h, pl.ds(ks, blk), :] * LOG2E for h in heads]
        if masked:
            row = lax.broadcasted_iota(jnp.int32, (blk, blk), 0)
            col = lax.broadcasted_iota(jnp.int32, (blk, blk), 1)
            t = [jnp.where(row <= col, x, -jnp.inf) for x in t]
        m_prev = [m_scr[h] for h in heads]
        m_new = [jnp.maximum(m_prev[h], cq[h] + jnp.max(t[h], axis=0, keepdims=True)) for h in heads]
        p = [jnp.exp2(t[h] - (m_new[h] - cq[h])) for h in heads]
        corr = [jnp.exp2(m_prev[h] - m_new[h]) for h in heads]
        for h in heads:
            l_scr[h] = corr[h] * l_scr[h] + jnp.sum(p[h], axis=0, keepdims=True)
            acc_scr[h] = corr[h] * acc_scr[h] + jnp.dot(vt_ref[ki, hs[h], :], p[h].astype(BF16),
                                                         preferred_element_type=F32)
            m_scr[h] = m_new[h]

    def body(ki, carry):
        step(ki, False)
        return carry

    lax.fori_loop(0, qi, body, 0)
    step(qi, True)
    for h in heads:
        o_ref[:, hs[h]] = jnp.transpose(acc_scr[h] / l_scr[h]).astype(o_ref.dtype)


def _fox_attention(proj, vt, c, batch, heads, q_col, blk, nh=4):
    n = proj.shape[0]
    t = n // batch
    nb = t // blk
    dh = LANES
    nh = min(nh, heads)
    w = nh * dh
    q0, k0 = q_col // w, (q_col + heads * dh) // w
    c_col = c.reshape(batch, heads, t, 1)
    c_row = c.reshape(batch, heads, nb, blk)
    return pl.pallas_call(
        functools.partial(_fox_kernel, blk=blk, nh=nh),
        grid=(batch, heads // nh, nb),
        in_specs=[pl.BlockSpec((blk, w), lambda b, h, i: (b * nb + i, q0 + h)),
                  pl.BlockSpec((t, w), lambda b, h, i: (b, k0 + h)),
                  pl.BlockSpec((nb, w, blk), lambda b, h, i: (b, h, 0)),
                  pl.BlockSpec((1, nh, nb, blk), lambda b, h, i: (b, h, 0, 0)),
                  pl.BlockSpec((1, nh, t, 1), lambda b, h, i: (b, h, 0, 0))],
        out_specs=pl.BlockSpec((blk, w), lambda b, h, i: (b * nb + i, h)),
        out_shape=jax.ShapeDtypeStruct((n, heads * dh), BF16),
        scratch_shapes=[pltpu.VMEM((nh, 1, blk), F32), pltpu.VMEM((nh, 1, blk), F32),
                        pltpu.VMEM((nh, dh, blk), F32)],
        compiler_params=_cparams(("parallel", "parallel", "arbitrary")),
        name="fox_attention",
    )(proj, proj, vt, c_row, c_col)


def _first_max(p, idx, valid):
    pm = jnp.where(valid, p, -2.0)
    m = jnp.max(pm, axis=0, keepdims=True)
    first = jnp.min(jnp.where(pm == m, idx, float(p.shape[0])), axis=0, keepdims=True)
    return m, first


def _route_kernel(x_ref, wr_ref, br_ref, meta_ref, cnt_ref, carry):
    @pl.when(pl.program_id(0) == 0)
    def _():
        carry[...] = jnp.zeros_like(carry)

    tm = x_ref.shape[0]
    n_experts = br_ref.shape[0]
    per = n_experts // N_GROUPS
    x = x_ref[...]
    x_hi = x.astype(BF16)
    x_lo = (x - x_hi.astype(F32)).astype(BF16)
    logits = _nt(wr_ref[0], x_hi) + _nt(wr_ref[0], x_lo) + _nt(wr_ref[1], x_hi) + br_ref[...]
    e = jnp.exp(logits - jnp.max(logits, axis=0, keepdims=True))
    probs = e / jnp.sum(e, axis=0, keepdims=True)
    idx = lax.broadcasted_iota(jnp.int32, (n_experts, tm), 0).astype(F32)
    best_score = jnp.full((1, tm), -1.0, F32)
    best_group = jnp.zeros((1, tm), F32)
    for grp in range(N_GROUPS):
        in_g = (idx >= grp * per) & (idx < (grp + 1) * per)
        m1, i1 = _first_max(probs, idx, in_g)
        m2, _ = _first_max(probs, idx, in_g & (idx != i1))
        score = m1 + m2
        take = score > best_score
        best_score = jnp.where(take, score, best_score)
        best_group = jnp.where(take, float(grp), best_group)
    in_sel = (idx >= best_group * per) & (idx < (best_group + 1) * per)
    p1, i1 = _first_max(probs, idx, in_sel)
    p2, i2 = _first_max(probs, idx, in_sel & (idx != i1))
    tot = p1 + p2
    row = lax.broadcasted_iota(jnp.int32, (tm, tm), 0)
    col = lax.broadcasted_iota(jnp.int32, (tm, tm), 1)
    earlier = jnp.where(row < col, 1.0, 0.0).astype(BF16)
    onehot = jnp.where((idx == i1) | (idx == i2), 1.0, 0.0)
    seen = jnp.dot(onehot.astype(BF16), earlier, preferred_element_type=F32) + carry[...]
    r1 = jnp.sum(jnp.where(idx == i1, seen, 0.0), axis=0, keepdims=True)
    r2 = jnp.sum(jnp.where(idx == i2, seen, 0.0), axis=0, keepdims=True)
    zero = jnp.zeros_like(r1)
    meta_ref[...] = jnp.concatenate([i1, i2, p1 / tot, p2 / tot, r1, r2, zero, zero], axis=0)
    carry[...] = carry[...] + jnp.sum(onehot, axis=1, keepdims=True)
    cnt_ref[...] = jnp.broadcast_to(carry[...], cnt_ref.shape)


def _route(x, w_router, b_router, tm=512):
    n, d = x.shape
    n_experts = w_router.shape[1]
    wr = w_router.T
    wr_hi = wr.astype(BF16)
    wr = jnp.stack([wr_hi, (wr - wr_hi.astype(F32)).astype(BF16)])
    return pl.pallas_call(
        _route_kernel,
        grid=(n // tm,),
        in_specs=[pl.BlockSpec((tm, d), lambda i: (i, 0)), pl.BlockSpec((2, n_experts, d), lambda i: (0, 0, 0)),
                  pl.BlockSpec((n_experts, 1), lambda i: (0, 0))],
        out_specs=[pl.BlockSpec((8, tm), lambda i: (0, i)), pl.BlockSpec((n_experts, LANES), lambda i: (0, 0))],
        out_shape=[jax.ShapeDtypeStruct((8, n), F32), jax.ShapeDtypeStruct((n_experts, LANES), F32)],
        scratch_shapes=[pltpu.VMEM((n_experts, 1), F32)],
        compiler_params=_cparams(("arbitrary",)),
        name="route",
    )(x, wr, b_router.reshape(n_experts, 1))


def _row_copy(src_ref, src_row, dst_ref, dst_row, sem):
    return pltpu.make_async_copy(src_ref.at[pl.ds(src_row, 1)], dst_ref.at[pl.ds(dst_row, 1)], sem)


def _dispatch_kernel(zid_ref, pos_ref, x_ref, xs_ref, zbuf, sem, zsem, *, tm):
    def zero_tiles(fn):
        for i in range(zid_ref.shape[0]):
            new_id = zid_ref[i] != zid_ref[max(i - 1, 0)] if i else True

            @pl.when(new_id)
            def _():
                fn(pltpu.make_async_copy(zbuf, xs_ref.at[pl.ds(zid_ref[i] * tm, tm)], zsem))

    @pl.when(pl.program_id(0) == 0)
    def _():
        zbuf[...] = jnp.zeros_like(zbuf)
        zero_tiles(lambda cp: cp.start())
        zero_tiles(lambda cp: cp.wait())

    def rows(fn):
        for r in range(x_ref.shape[0]):
            for s in range(pos_ref.shape[0]):
                fn(_row_copy(x_ref, r, xs_ref, pos_ref[s, r], sem), s)

    rows(lambda cp, s: cp.start(priority=s % DMA_QUEUES))
    rows(lambda cp, s: cp.wait())


def _dispatch(x, pos, zero_ids, n_rows, tm):
    n, d = x.shape
    tok = 256
    grid_spec = pltpu.PrefetchScalarGridSpec(
        num_scalar_prefetch=1,
        grid=(n // tok,),
        in_specs=[pl.BlockSpec((pos.shape[0], tok), lambda i, z: (0, i), memory_space=pltpu.SMEM),
                  pl.BlockSpec((tok, d), lambda i, z: (i, 0))],
        out_specs=pl.BlockSpec(memory_space=pl.ANY),
        scratch_shapes=[pltpu.VMEM((tm, d), x.dtype), pltpu.SemaphoreType.DMA, pltpu.SemaphoreType.DMA],
    )
    return pl.pallas_call(
        functools.partial(_dispatch_kernel, tm=tm),
        grid_spec=grid_spec,
        out_shape=jax.ShapeDtypeStruct((n_rows, d), x.dtype),
        compiler_params=_cparams(("arbitrary",)),
        name="moe_dispatch",
    )(zero_ids, pos, x)


def _experts_kernel(te_ref, nx_ref, na_ref, xs_ref, wgu_hbm, wd_hbm, ys_ref, wgu_f, wd_f, wgu_b, wd_b, sems, *, layer):
    j = pl.program_id(0)
    live = j < na_ref[0]
    fresh = (j == 0) | (te_ref[j] != te_ref[jnp.maximum(j - 1, 0)])

    def fetch(e):
        return (pltpu.make_async_copy(wgu_hbm.at[layer, e], wgu_f, sems.at[0]),
                pltpu.make_async_copy(wd_hbm.at[layer, e], wd_f, sems.at[1]))

    @pl.when(live & (j == 0))
    def _():
        for cp in fetch(te_ref[0]):
            cp.start()

    @pl.when(live & fresh)
    def _():
        for cp in fetch(te_ref[j]):
            cp.wait()
        wgu_b[...] = wgu_f[...].astype(BF16)
        wd_b[...] = wd_f[...].astype(BF16)

    @pl.when(live & fresh & (nx_ref[j] >= 0))
    def _():
        for cp in fetch(nx_ref[j]):
            cp.start()

    @pl.when(live)
    def _():
        de = wd_b.shape[0]
        gu = jnp.dot(xs_ref[...].astype(BF16), wgu_b[...], preferred_element_type=F32)
        gpart, upart = gu[:, :de], gu[:, de:]
        h = (gpart / (1.0 + jnp.exp(-gpart))) * upart
        ys_ref[...] = jnp.dot(h.astype(BF16), wd_b[...], preferred_element_type=F32)

    @pl.when(jnp.logical_not(live))
    def _():
        ys_ref[...] = jnp.zeros_like(ys_ref)


def _experts(xs, w_gu, w_down, layer, tile_expert, next_expert, n_active, tm):
    p, d = xs.shape
    de = w_down.shape[2]
    live = lambda j, na: jnp.maximum(jnp.minimum(j, na[0] - 1), 0)
    grid_spec = pltpu.PrefetchScalarGridSpec(
        num_scalar_prefetch=3,
        grid=(p // tm,),
        in_specs=[pl.BlockSpec((tm, d), lambda j, te, nx, na: (live(j, na), 0)),
                  pl.BlockSpec(memory_space=pl.ANY), pl.BlockSpec(memory_space=pl.ANY)],
        out_specs=pl.BlockSpec((tm, d), lambda j, te, nx, na: (j, 0)),
        scratch_shapes=[pltpu.VMEM((d, 2 * de), F32), pltpu.VMEM((de, d), F32), pltpu.VMEM((d, 2 * de), BF16),
                        pltpu.VMEM((de, d), BF16), pltpu.SemaphoreType.DMA((2,))],
    )
    return pl.pallas_call(
        functools.partial(_experts_kernel, layer=layer),
        grid_spec=grid_spec,
        out_shape=jax.ShapeDtypeStruct((p, d), F32),
        compiler_params=_cparams(("arbitrary",)),
        name="moe_experts",
    )(tile_expert, next_expert, n_active, xs, w_gu, w_down)


def _combine_kernel(pos_ref, x_ref, gw_ref, ys_ref, g_ref, b_ref, o_ref, buf, sems, *, alpha):
    tm = x_ref.shape[0]
    n_slots = pos_ref.shape[0]
    half = tm // 2

    def rows(h, fn):
        for r in range(h * half, (h + 1) * half):
            for s in range(n_slots):
                fn(_row_copy(ys_ref, pos_ref[s, r], buf.at[s], r, sems.at[h]), s)

    for h in range(2):
        rows(h, lambda cp, s: cp.start(priority=s % DMA_QUEUES))
    for h in range(2):
        rows(h, lambda cp, s: cp.wait())
        rs = pl.ds(h * half, half)
        gw = gw_ref[rs, :]
        y = alpha * x_ref[rs, :]
        for s in range(n_slots):
            y = y + gw[:, s:s + 1] * buf[s, rs, :]
        o_ref[rs, :] = _layer_norm(y, g_ref[...], b_ref[...], LN_EPS)


def _combine_ln(x, ys, pos, gw, g, b, alpha, tm=256):
    n, d = x.shape
    n_slots = pos.shape[0]
    return pl.pallas_call(
        functools.partial(_combine_kernel, alpha=alpha),
        grid=(n // tm,),
        in_specs=[pl.BlockSpec((n_slots, tm), lambda i: (0, i), memory_space=pltpu.SMEM),
                  pl.BlockSpec((tm, d), lambda i: (i, 0)), pl.BlockSpec((tm, n_slots), lambda i: (i, 0)),
                  pl.BlockSpec(memory_space=pl.ANY), pl.BlockSpec((1, d), lambda i: (0, 0)),
                  pl.BlockSpec((1, d), lambda i: (0, 0))],
        out_specs=pl.BlockSpec((tm, d), lambda i: (i, 0)),
        out_shape=jax.ShapeDtypeStruct((n, d), F32),
        scratch_shapes=[pltpu.VMEM((n_slots, tm, d), F32), pltpu.SemaphoreType.DMA((2,))],
        compiler_params=_cparams(("arbitrary",)),
        name="moe_combine",
    )(pos, x, gw, ys, g.reshape(1, d), b.reshape(1, d))


def _moe_ln(x, w_router, b_router, w_gu, w_down, layer, g, b, alpha, tm=256):
    n, d = x.shape
    n_experts = w_down.shape[1]
    meta, counts = _route(x, w_router, b_router)
    eid = meta[0:2].astype(jnp.int32)
    gw = meta[2:4].T
    rank = meta[4:6].astype(jnp.int32)
    counts = counts[:, 0].astype(jnp.int32)
    tiles = (counts + tm - 1) // tm
    tile_end = jnp.cumsum(tiles)
    row_off = (tile_end - tiles) * tm
    n_tiles = (eid.size + tm - 1) // tm + n_experts
    tile_expert = jnp.sum(jnp.arange(n_tiles)[:, None] >= tile_end[None, :], axis=1).astype(jnp.int32)
    tile_expert = jnp.minimum(tile_expert, n_experts - 1)
    n_active = tile_end[-1:].astype(jnp.int32)
    ids = jnp.arange(n_experts)
    later = (ids[None, :] > ids[:, None]) & (tiles[None, :] > 0)
    next_of = jnp.min(jnp.where(later, ids[None, :], n_experts), axis=1)
    next_of = jnp.where(next_of < n_experts, next_of, -1).astype(jnp.int32)
    lookup = lambda table, ids: jnp.sum(jnp.where(ids[..., None] == jnp.arange(n_experts), table, 0), axis=-1)
    next_expert = lookup(next_of, tile_expert)
    pos = lookup(row_off, eid) + rank
    last_tile = jnp.maximum(tile_end - 1, 0)
    idle_tile = jnp.minimum(tile_end[-1] + jnp.arange(n_experts), n_tiles - 1)
    zero_ids = jnp.concatenate([last_tile, idle_tile]).astype(jnp.int32)
    xs = _dispatch(x, pos, zero_ids, n_tiles * tm, tm)
    ys = _experts(xs, w_gu, w_down, layer, tile_expert, next_expert, n_active, tm)
    return _combine_ln(x, ys, pos, gw, g, b, alpha)


def _token_shift(x_ref, prev_ref, first):
    x = x_ref[...]
    prev_row = jnp.where(first, 0.0, prev_ref[7:8, :])
    row = lax.broadcasted_iota(jnp.int32, x.shape, 0)
    return x, jnp.where(row == 0, prev_row, pltpu.roll(x, 1, 0))


def _rkv_kernel(x_ref, prev_ref, mu_ref, w_ref, o_ref, *, tiles_per_seq):
    first = pl.program_id(0) % tiles_per_seq == 0
    x, xp = _token_shift(x_ref, prev_ref, first)
    xm = x + (xp - x) * mu_ref[0]
    o_ref[0] = jnp.dot(xm.astype(BF16), w_ref[0], preferred_element_type=F32).astype(o_ref.dtype)


def _rkv_proj(x, mu3, w_rkv, t, tm=1024):
    n, d = x.shape
    sub = tm // 8
    return pl.pallas_call(
        functools.partial(_rkv_kernel, tiles_per_seq=t // tm),
        grid=(n // tm, 3),
        in_specs=[pl.BlockSpec((tm, d), lambda i, j: (i, 0)),
                  pl.BlockSpec((8, d), lambda i, j: (jnp.maximum(i * sub - 1, 0), 0)),
                  pl.BlockSpec((1, 1, d), lambda i, j: (j, 0, 0)), pl.BlockSpec((1, d, d), lambda i, j: (j, 0, 0))],
        out_specs=pl.BlockSpec((1, tm, d), lambda i, j: (j, i, 0)),
        out_shape=jax.ShapeDtypeStruct((3, n, d), BF16),
        compiler_params=_cparams(("parallel", "arbitrary")),
        name="rkv_proj",
    )(x, x, mu3.reshape(3, 1, d), w_rkv)


def _lora_kernel(x_ref, prev_ref, mu_ref, w0_ref, w1_ref, w2_ref, a0_ref, a1_ref, a2_ref, g1_ref, g2_ref, wl_ref,
                 a_ref, g_ref, *, tiles_per_seq):
    first = pl.program_id(0) % tiles_per_seq == 0
    x, xp = _token_shift(x_ref, prev_ref, first)
    dx = xp - x

    def mm(a, w_ref):
        return jnp.dot(a.astype(BF16), w_ref[...], preferred_element_type=F32)

    zw = w0_ref[...] + mm(jnp.tanh(mm(x + dx * mu_ref[0:1, :], w1_ref)), w2_ref)
    wl_ref[...] = -HALF_DECAY * _sigmoid(zw)
    za = a0_ref[...] + mm(mm(x + dx * mu_ref[1:2, :], a1_ref), a2_ref)
    a_ref[...] = _sigmoid(za).astype(a_ref.dtype)
    zg = mm(x + dx * mu_ref[2:3, :], g1_ref)
    g_ref[...] = mm(_sigmoid(zg), g2_ref).astype(g_ref.dtype)


def _lora(x, mu3, w0, w1, w2, a0, a1, a2, g1, g2, t, tm=512):
    n, d = x.shape
    sub = tm // 8
    const = lambda shape: pl.BlockSpec(shape, lambda i: (0, 0))
    row = pl.BlockSpec((tm, d), lambda i: (i, 0))
    return pl.pallas_call(
        functools.partial(_lora_kernel, tiles_per_seq=t // tm),
        grid=(n // tm,),
        in_specs=[row, pl.BlockSpec((8, d), lambda i: (jnp.maximum(i * sub - 1, 0), 0)), const((3, d)),
                  const((1, d)), const(w1.shape), const(w2.shape), const((1, d)), const(a1.shape), const(a2.shape),
                  const(g1.shape), const(g2.shape)],
        out_specs=[row, row, row],
        out_shape=[jax.ShapeDtypeStruct((n, d), F32), jax.ShapeDtypeStruct((n, d), BF16),
                   jax.ShapeDtypeStruct((n, d), BF16)],
        compiler_params=_cparams(("parallel",)),
        name="lora",
    )(x, x, mu3, w0.reshape(1, d), w1, w2, a0.reshape(1, d), a1, a2, g1, g2)


def _scan_groups(r, k, v, lw, a_gate, g, kk_w, ka_w, rk_w, gg, gb, s0, *, hd, gn_eps):
    c = r[0].shape[0]
    nh = LANES // hd
    hc = nh * c
    each = lambda f, *xs: [f(*x) for x in zip(*xs)]
    lane = lax.broadcasted_iota(jnp.int32, (1, LANES), 1)
    head_masks = [(lane >= h * hd) & (lane < (h + 1) * hd) for h in range(nh)]
    lr = lax.broadcasted_iota(jnp.int32, (LANES, LANES), 0) // hd
    lc = lax.broadcasted_iota(jnp.int32, (LANES, LANES), 1) // hd
    same_head = jnp.where(lr == lc, 1.0, 0.0).astype(BF16)
    trow = lax.broadcasted_iota(jnp.int32, (c, c), 0)
    tcol = lax.broadcasted_iota(jnp.int32, (c, c), 1)
    lower = jnp.where(trow >= tcol, 1.0, 0.0).astype(BF16)
    prow = lax.broadcasted_iota(jnp.int32, (hc, hc), 0)
    pcol = lax.broadcasted_iota(jnp.int32, (hc, hc), 1)
    same_blk = (prow // c) == (pcol // c)
    strict = same_blk & (prow > pcol)
    incl = same_blk & (prow >= pcol)
    eye = jnp.where(prow == pcol, 1.0, 0.0).astype(F32)

    def split(x):
        hi = x.astype(BF16)
        return hi, (x - hi.astype(F32)).astype(BF16)

    def head_sum(xs):
        return [jnp.dot(x.astype(BF16), same_head, preferred_element_type=F32) for x in xs]

    def per_head(x):
        return jnp.concatenate([jnp.where(hm, x, 0.0) for hm in head_masks], axis=0).astype(BF16)

    kk = each(lambda k_, w_: k_ * w_, k, kk_w)
    kk_n = head_sum(each(lambda x: x * x, kk))
    kk = each(lambda x, n_: x / jnp.maximum(jnp.sqrt(n_), 1e-12), kk, kk_n)
    k2 = each(lambda k_, a_, w_: k_ * (1.0 + (a_ - 1.0) * w_), k, a_gate, ka_w)
    lw_parts = each(split, lw)
    cum = [jnp.dot(lower, hi, preferred_element_type=F32) + jnp.dot(lower, lo, preferred_element_type=F32)
           for hi, lo in lw_parts]
    gam = each(jnp.exp, cum)
    inv_gam = each(lambda x: jnp.exp(-x), cum)
    gam_end = each(lambda x: x[c - 1:c, :], gam)
    a_t = each(lambda kk_, cum_, lw_: -kk_ * jnp.exp(cum_ - lw_), kk, cum, lw)
    b_t = each(lambda kk_, a_, ig: kk_ * a_ * ig, kk, a_gate, inv_gam)
    k_t = each(lambda k2_, ig: k2_ * ig, k2, inv_gam)
    r_t = each(lambda r_, gm: r_ * gm, r, gam)

    s0b = each(lambda x: x.astype(BF16), s0)
    ar_s0 = each(lambda a_, r_, s_: _nt(jnp.concatenate([a_, r_], axis=0).astype(BF16), s_), a_t, r_t, s0b)
    ar2 = each(lambda a_, r_: jnp.concatenate([per_head(a_), per_head(r_)], axis=0), a_t, r_t)
    bk2 = each(lambda b_, k_: jnp.concatenate([per_head(b_), per_head(k_)], axis=0), b_t, k_t)
    v2 = each(per_head, v)
    gmat = each(_nt, ar2, bk2)
    a_ab = each(lambda x: jnp.where(strict, x[:hc, :hc], 0.0), gmat)
    a_ak = each(lambda x: jnp.where(strict, x[:hc, hc:], 0.0), gmat)
    a_r = each(lambda x: jnp.concatenate([jnp.where(incl, x[hc:, :hc], 0.0), jnp.where(incl, x[hc:, hc:], 0.0)],
                                         axis=1).astype(BF16), gmat)
    rhs = each(lambda as0, ak, v_: jnp.concatenate([jnp.where(hm, as0[:c], 0.0) for hm in head_masks], axis=0)
               + _bdot(ak, v_), ar_s0, a_ak, v2)
    inv = each(lambda x: eye + x, a_ab)
    m = each(lambda x: _bdot(x, x), a_ab)
    n_pow = 2
    while 2 * n_pow < c:
        mp = each(lambda m_, p_: _bdot(jnp.concatenate([m_, p_], axis=0), m_), m, inv)
        inv = each(lambda p_, mp_: p_ + mp_[hc:], inv, mp)
        m = each(lambda mp_: mp_[:hc], mp)
        n_pow *= 2
    rhs = each(lambda rhs_, m_: rhs_ + _bdot(m_, rhs_), rhs, m)
    u2 = each(_bdot, inv, rhs)
    uv = each(lambda u_, v_: jnp.concatenate([u_.astype(BF16), v_], axis=0), u2, v2)
    y2 = each(lambda ar_, uv_: jnp.dot(ar_, uv_, preferred_element_type=F32), a_r, uv)
    y = each(lambda rs, y2_: rs[c:] + sum(y2_[h * c:(h + 1) * c, :] for h in range(nh)), ar_s0, y2)
    bkg = each(lambda b_, k_, ge: jnp.concatenate([per_head(b_ * ge), per_head(k_ * ge)], axis=0), b_t, k_t, gam_end)
    s_new = each(lambda s_, ge, uv_, bkg_: s_ * ge + _tn(uv_, bkg_), s0, gam_end, uv, bkg)

    inv_hd = 1.0 / hd
    mean = head_sum(y)
    yc = each(lambda y_, m_: y_ - m_ * inv_hd, y, mean)
    var = head_sum(each(lambda x: x * x, yc))
    yn = each(lambda yc_, var_, gg_, gb_: yc_ * lax.rsqrt(var_ * inv_hd + gn_eps) * gg_ + gb_, yc, var, gg, gb)
    rk_sum = head_sum(each(lambda r_, k2_, w_: r_ * k2_ * w_, r, k2, rk_w))
    out = each(lambda yn_, rk_, v_, g_: (yn_ + rk_ * v_) * g_, yn, rk_sum, v, g)
    return out, s_new


def _scan_kernel(r_ref, k_ref, v_ref, wl_ref, a_ref, g_ref, kk_ref, ka_ref, rk_ref, gg_ref, gb_ref, o_ref, state, *,
                 hd, gn_eps):
    @pl.when(pl.program_id(2) == 0)
    def _():
        state[...] = jnp.zeros_like(state)

    ng = state.shape[0]
    sls = [slice(p * LANES, (p + 1) * LANES) for p in range(ng)]
    tok3 = lambda ref: [ref[0, :, sl].astype(F32) for sl in sls]
    tok2 = lambda ref: [ref[:, sl].astype(F32) for sl in sls]
    out, s_new = _scan_groups(tok3(r_ref), tok3(k_ref), tok3(v_ref), tok2(wl_ref), tok2(a_ref), tok2(g_ref),
                              tok2(kk_ref), tok2(ka_ref), tok2(rk_ref), tok2(gg_ref), tok2(gb_ref),
                              [state[p] for p in range(ng)], hd=hd, gn_eps=gn_eps)
    for p in range(ng):
        state[p] = s_new[p]
        o_ref[:, sls[p]] = out[p].astype(o_ref.dtype)


def _rwkv_scan(rkv, wl, a, g, k_k, k_a, r_k, gn_g, gn_b, batch, hd, gn_eps, chunk=64, groups=16):
    _, n, d = rkv.shape
    t = n // batch
    nc = t // chunk
    groups = min(groups, d // LANES)
    w = groups * LANES
    tok = lambda j: pl.BlockSpec((1, chunk, w), functools.partial(lambda b, p, c, j: (j, b * nc + c, p), j=j))
    tok2 = pl.BlockSpec((chunk, w), lambda b, p, c: (b * nc + c, p))
    par = pl.BlockSpec((1, w), lambda b, p, c: (0, p))
    return pl.pallas_call(
        functools.partial(_scan_kernel, hd=hd, gn_eps=gn_eps),
        grid=(batch, d // w, nc),
        in_specs=[tok(0), tok(1), tok(2), tok2, tok2, tok2, par, par, par, par, par],
        out_specs=tok2,
        out_shape=jax.ShapeDtypeStruct((n, d), BF16),
        scratch_shapes=[pltpu.VMEM((groups, LANES, LANES), F32)],
        compiler_params=_cparams(("parallel", "parallel", "arbitrary")),
        name="rwkv_scan",
    )(rkv, rkv, rkv, wl, a, g, k_k.reshape(1, d), k_a.reshape(1, d), r_k.reshape(1, d), gn_g.reshape(1, d),
      gn_b.reshape(1, d))


def _pad_lora(w_in, w_out):
    r = w_in.shape[1]
    rp = -(-r // LANES) * LANES
    return (jnp.pad(w_in, ((0, 0), (0, rp - r))).astype(BF16), jnp.pad(w_out, ((0, rp - r), (0, 0))).astype(BF16))


def kernel(x, ev_w_in, ev_b_a, ev_w_s, ev_b_s, ev_g_v, ev_b_v, ev_b_f, ev_w_out, rw_mu, rw_w_rkv, rw_w0, rw_w1, rw_w2, rw_a0, rw_a1, rw_a2, rw_g1, rw_g2, rw_k_k, rw_k_a, rw_r_k, rw_gn_g, rw_gn_b, rw_w_o, ln_g, ln_b, w_router, b_router, w_gu, w_down):
    batch, t, d = x.shape
    depth = ln_g.shape[0]
    alpha = (2 * depth) ** 0.25
    h = x.reshape(batch * t, d)
    for layer in range(depth):
        i = layer // 2
        if layer % 2 == 0:
            aw = ev_g_v.shape[1]
            heads = ev_b_f.shape[1]
            q_col = 2 * aw
            bw = heads * LANES
            k_col, v_col, f_col = q_col + bw, q_col + 2 * bw, q_col + 3 * bw
            w_in = ev_w_in[i]
            col = jnp.arange(v_col)
            q_scale = jnp.where((col >= q_col) & (col < k_col), LANES ** -0.5 * LOG2E, 1.0)
            proj = _matmul(h, (w_in[:, :v_col] * q_scale).astype(BF16), v_col, BF16)
            vt, f_logit = _proj_transposed(h, w_in[:, v_col:f_col].T.astype(BF16), w_in[:, f_col:].T, FOX_BLOCK)
            c = _fgate(f_logit, ev_b_f[i], batch)
            y_a = _sgu(proj, ev_b_a[i], ev_w_s[i], ev_b_s[i], ev_g_v[i], ev_b_v[i])
            y_b = _fox_attention(proj, vt, c, batch, heads, q_col, FOX_BLOCK)
            h = _proj_ln([y_a, y_b], ev_w_out[i].astype(BF16), h, ln_g[layer, 0], ln_b[layer, 0], alpha)
        else:
            hd = rw_r_k.shape[2]
            mu = rw_mu[i]
            rkv = _rkv_proj(h, mu[:3], rw_w_rkv[i].astype(BF16), t)
            w1, w2 = _pad_lora(rw_w1[i], rw_w2[i])
            a1, a2 = _pad_lora(rw_a1[i], rw_a2[i])
            g1, g2 = _pad_lora(rw_g1[i], rw_g2[i])
            wl, a, g = _lora(h, mu[3:], rw_w0[i], w1, w2, rw_a0[i], a1, a2, g1, g2, t)
            y = _rwkv_scan(rkv, wl, a, g, rw_k_k[i], rw_k_a[i], rw_r_k[i].reshape(-1), rw_gn_g[i], rw_gn_b[i],
                           batch, hd, hd * 1e-5)
            h = _proj_ln([y], rw_w_o[i].astype(BF16), h, ln_g[layer, 0], ln_b[layer, 0], alpha)
        h = _moe_ln(h, w_router, b_router, w_gu, w_down, layer, ln_g[layer, 1], ln_b[layer, 1], alpha)
    return h.reshape(batch, t, d)
```

```python
import functools

import jax
import jax.numpy as jnp
from jax import lax
from jax.experimental import pallas as pl
from jax.experimental.pallas import tpu as pltpu

F32 = jnp.float32
BF16 = jnp.bfloat16
HI = lax.Precision.HIGHEST

LN_EPS = 1e-5
N_GROUPS = 4
LANES = 128
FOX_BLOCK = 512
LOG2E = 1.4426950408889634
HALF_DECAY = 0.6065306597126334
DMA_QUEUES = 2
VMEM_LIMIT = 56 * 1024 * 1024


def _cparams(sem):
    return pltpu.CompilerParams(dimension_semantics=sem, vmem_limit_bytes=VMEM_LIMIT)


def _layer_norm(x, g, b, eps):
    mu = jnp.mean(x, -1, keepdims=True)
    xc = x - mu
    var = jnp.mean(xc * xc, -1, keepdims=True)
    return xc * lax.rsqrt(var + eps) * g + b


def _sigmoid(x):
    return 0.5 + 0.5 * jnp.tanh(0.5 * x)


def _nt(a, b, **kw):
    return lax.dot_general(a, b, (((1,), (1,)), ((), ())), preferred_element_type=F32, **kw)


def _tn(a, b, **kw):
    return lax.dot_general(a, b, (((0,), (0,)), ((), ())), preferred_element_type=F32, **kw)


def _bdot(a, b):
    return jnp.dot(a.astype(BF16), b.astype(BF16), preferred_element_type=F32)


def _mm_kernel(a_ref, w_ref, o_ref, a_bf16):
    @pl.when(pl.program_id(1) == 0)
    def _():
        a_bf16[...] = a_ref[...].astype(BF16)

    o_ref[...] = jnp.dot(a_bf16[...], w_ref[...], preferred_element_type=F32).astype(o_ref.dtype)


def _matmul(a, w, n_cols, out_dtype, tm=1024, tn=1024):
    m, k = a.shape
    tm = min(tm, m)
    tn = next(c for c in (tn, 256, LANES) if n_cols % c == 0)
    return pl.pallas_call(
        _mm_kernel,
        grid=(m // tm, n_cols // tn),
        in_specs=[pl.BlockSpec((tm, k), lambda i, j: (i, 0)), pl.BlockSpec((k, tn), lambda i, j: (0, j))],
        out_specs=pl.BlockSpec((tm, tn), lambda i, j: (i, j)),
        out_shape=jax.ShapeDtypeStruct((m, n_cols), out_dtype),
        scratch_shapes=[pltpu.VMEM((tm, k), BF16)],
        compiler_params=_cparams(("parallel", "arbitrary")),
        name="matmul",
    )(a, w)


def _proj_ln_kernel(*refs, n_in, alpha):
    a_refs, w_refs = refs[:n_in], refs[n_in:2 * n_in]
    res_ref, g_ref, b_ref, wr_ref, br_ref, o_ref, meta_ref, cnt_ref, carry = refs[2 * n_in:]

    @pl.when(pl.program_id(0) == 0)
    def _():
        carry[...] = jnp.zeros_like(carry)

    acc = alpha * res_ref[...]
    for a_ref, w_ref in zip(a_refs, w_refs):
        acc = acc + jnp.dot(a_ref[...], w_ref[...], preferred_element_type=F32)
    y = _layer_norm(acc, g_ref[...], b_ref[...], LN_EPS)
    o_ref[...] = y
    meta_ref[...] = _route_tile(y, wr_ref, br_ref, carry)
    cnt_ref[...] = jnp.broadcast_to(carry[...], cnt_ref.shape)


def _proj_ln_route(a_list, w, res, g, b, alpha, w_router, b_router, tm=512):
    m, d = res.shape
    n_in = len(a_list)
    kc = a_list[0].shape[1]
    n_experts = w_router.shape[1]
    wr = w_router.T
    wr_hi = wr.astype(BF16)
    wr = jnp.stack([wr_hi, (wr - wr_hi.astype(F32)).astype(BF16)])
    in_specs = [pl.BlockSpec((tm, kc), lambda i: (i, 0)) for _ in a_list]
    in_specs += [pl.BlockSpec((kc, d), functools.partial(lambda i, r: (r, 0), r=r)) for r in range(n_in)]
    in_specs += [pl.BlockSpec((tm, d), lambda i: (i, 0)), pl.BlockSpec((1, d), lambda i: (0, 0)),
                 pl.BlockSpec((1, d), lambda i: (0, 0)), pl.BlockSpec((2, n_experts, d), lambda i: (0, 0, 0)),
                 pl.BlockSpec((n_experts, 1), lambda i: (0, 0))]
    return pl.pallas_call(
        functools.partial(_proj_ln_kernel, n_in=n_in, alpha=alpha),
        grid=(m // tm,),
        in_specs=in_specs,
        out_specs=[pl.BlockSpec((tm, d), lambda i: (i, 0)), pl.BlockSpec((8, tm), lambda i: (0, i)),
                   pl.BlockSpec((n_experts, LANES), lambda i: (0, 0))],
        out_shape=[jax.ShapeDtypeStruct((m, d), F32), jax.ShapeDtypeStruct((8, m), F32),
                   jax.ShapeDtypeStruct((n_experts, LANES), F32)],
        scratch_shapes=[pltpu.VMEM((n_experts, 1), F32)],
        compiler_params=_cparams(("arbitrary",)),
        name="proj_ln_route",
    )(*a_list, *([w] * n_in), res, g.reshape(1, d), b.reshape(1, d), wr, b_router.reshape(n_experts, 1))


def _fgate_kernel(z_ref, bf_ref, c_ref, carry):
    @pl.when(pl.program_id(1) == 0)
    def _():
        carry[...] = jnp.zeros_like(carry)

    tm = z_ref.shape[2]
    z = z_ref[0] + bf_ref[...]
    log_f = jnp.minimum(z, 0.0) - jnp.log1p(jnp.exp(-jnp.abs(z)))
    row = lax.broadcasted_iota(jnp.int32, (tm, tm), 0)
    col = lax.broadcasted_iota(jnp.int32, (tm, tm), 1)
    upper = jnp.where(row <= col, 1.0, 0.0).astype(F32)
    c = jnp.dot(log_f, upper, preferred_element_type=F32, precision=HI) + carry[...]
    c_ref[0] = c
    carry[...] = carry[...] + jnp.sum(log_f, axis=-1, keepdims=True)


def _fgate(f_logit, b_f, batch):
    tiles, h, tm = f_logit.shape
    nt = tiles // batch
    return pl.pallas_call(
        _fgate_kernel,
        grid=(batch, nt),
        in_specs=[pl.BlockSpec((1, h, tm), lambda b, i: (b * nt + i, 0, 0)), pl.BlockSpec((h, 1), lambda b, i: (0, 0))],
        out_specs=pl.BlockSpec((1, h, tm), lambda b, i: (b, 0, i)),
        out_shape=jax.ShapeDtypeStruct((batch, h, nt * tm), F32),
        scratch_shapes=[pltpu.VMEM((h, 1), F32)],
        compiler_params=_cparams(("parallel", "arbitrary")),
        name="fgate",
    )(f_logit, b_f.reshape(h, 1))


def _gelu_tanh(x):
    return 0.5 * x * (1.0 + jnp.tanh(0.7978845608028654 * (x + 0.044715 * (x * x * x))))


def _sgu_kernel(z_ref, ba_ref, ws_ref, bs_ref, gv_ref, bv_ref, o_ref, *, chunk, groups):
    aw = o_ref.shape[1]
    gd = aw // groups
    z = _gelu_tanh(z_ref[...].astype(F32) + ba_ref[...])
    u = z[:, :aw]
    v = _layer_norm(z[:, aw:], gv_ref[...], bv_ref[...], LN_EPS).astype(BF16)
    row = lax.broadcasted_iota(jnp.int32, (chunk, chunk), 0)
    col = lax.broadcasted_iota(jnp.int32, (chunk, chunk), 1)
    causal = row >= col
    bs = bs_ref[...]
    for g in range(groups):
        w_g = jnp.where(causal, ws_ref[g], 0.0).astype(BF16)
        for c in range(z.shape[0] // chunk):
            rs = slice(c * chunk, (c + 1) * chunk)
            cs = slice(g * gd, (g + 1) * gd)
            s = jnp.dot(w_g, v[rs, cs], preferred_element_type=F32) + bs[:, g:g + 1]
            o_ref[rs, cs] = (u[rs, cs] * s).astype(o_ref.dtype)


def _sgu(proj, b_a, w_s, b_s, g_v, b_v, tm=512):
    n = proj.shape[0]
    groups, chunk, _ = w_s.shape
    aw = g_v.shape[0]
    return pl.pallas_call(
        functools.partial(_sgu_kernel, chunk=chunk, groups=groups),
        grid=(n // tm,),
        in_specs=[pl.BlockSpec((tm, 2 * aw), lambda i: (i, 0)), pl.BlockSpec((1, 2 * aw), lambda i: (0, 0)),
                  pl.BlockSpec((groups, chunk, chunk), lambda i: (0, 0, 0)),
                  pl.BlockSpec((chunk, groups), lambda i: (0, 0)), pl.BlockSpec((1, aw), lambda i: (0, 0)),
                  pl.BlockSpec((1, aw), lambda i: (0, 0))],
        out_specs=pl.BlockSpec((tm, aw), lambda i: (i, 0)),
        out_shape=jax.ShapeDtypeStruct((n, aw), BF16),
        compiler_params=_cparams(("parallel",)),
        name="sgu",
    )(proj, b_a.reshape(1, -1), w_s, b_s.T, g_v.reshape(1, aw), b_v.reshape(1, aw))


def _vt_kernel(x_ref, w_ref, wf_ref, o_ref, f_ref):
    x = x_ref[...]
    x_hi = x.astype(BF16)
    x_lo = (x - x_hi.astype(F32)).astype(BF16)
    o_ref[0] = _nt(w_ref[...], x_hi).astype(o_ref.dtype)
    f_ref[0] = _nt(wf_ref[0], x_hi) + _nt(wf_ref[0], x_lo) + _nt(wf_ref[1], x_hi)


def _proj_transposed(x, w_t, wf_t, blk):
    n, d = x.shape
    rows, rows_f = w_t.shape[0], wf_t.shape[0]
    wf_hi = wf_t.astype(BF16)
    wf = jnp.stack([wf_hi, (wf_t - wf_hi.astype(F32)).astype(BF16)])
    return pl.pallas_call(
        _vt_kernel,
        grid=(n // blk,),
        in_specs=[pl.BlockSpec((blk, d), lambda i: (i, 0)), pl.BlockSpec((rows, d), lambda i: (0, 0)),
                  pl.BlockSpec((2, rows_f, d), lambda i: (0, 0, 0))],
        out_specs=[pl.BlockSpec((1, rows, blk), lambda i: (i, 0, 0)),
                   pl.BlockSpec((1, rows_f, blk), lambda i: (i, 0, 0))],
        out_shape=[jax.ShapeDtypeStruct((n // blk, rows, blk), BF16),
                   jax.ShapeDtypeStruct((n // blk, rows_f, blk), F32)],
        compiler_params=_cparams(("parallel",)),
        name="proj_transposed",
    )(x, w_t, wf)


def _fox_kernel(q_ref, k_ref, vt_ref, cq_ref, ck_ref, o_ref, m_scr, l_scr, acc_scr, *, blk, nh):
    qi = pl.program_id(2)
    heads = range(nh)
    hs = [slice(h * LANES, (h + 1) * LANES) for h in heads]
    m_scr[...] = jnp.full_like(m_scr, -jnp.inf)
    l_scr[...] = jnp.zeros_like(l_scr)
    acc_scr[...] = jnp.zeros_like(acc_scr)
    q = [q_ref[:, hs[h]] for h in heads]
    cq = [cq_ref[0, h, pl.ds(qi, 1), :] * LOG2E for h in heads]

    def step(ki, masked):
        ks = pl.multiple_of(ki * blk, blk)
        t = [_nt(k_ref[pl.ds(ks, blk), hs[h]], q[h]) - ck_ref[0, h, pl.ds(ks, blk), :] * LOG2E for h in heads]
        if masked:
            row = lax.broadcasted_iota(jnp.int32, (blk, blk), 0)
            col = lax.broadcasted_iota(jnp.int32, (blk, blk), 1)
            t = [jnp.where(row <= col, x, -jnp.inf) for x in t]
        m_prev = [m_scr[h] for h in heads]
        m_new = [jnp.maximum(m_prev[h], cq[h] + jnp.max(t[h], axis=0, keepdims=True)) for h in heads]
        p = [jnp.exp2(t[h] - (m_new[h] - cq[h])) for h in heads]
        corr = [jnp.exp2(m_prev[h] - m_new[h]) for h in heads]
        for h in heads:
            l_scr[h] = corr[h] * l_scr[h] + jnp.sum(p[h], axis=0, keepdims=True)
            acc_scr[h] = corr[h] * acc_scr[h] + jnp.dot(vt_ref[ki, hs[h], :], p[h].astype(BF16),
                                                         preferred_element_type=F32)
            m_scr[h] = m_new[h]

    def body(ki, carry):
        step(ki, False)
        return carry

    lax.fori_loop(0, qi, body, 0)
    step(qi, True)
    for h in heads:
        o_ref[:, hs[h]] = jnp.transpose(acc_scr[h] / l_scr[h]).astype(o_ref.dtype)


def _fox_attention(proj, vt, c, batch, heads, q_col, blk, nh=4):
    n = proj.shape[0]
    t = n // batch
    nb = t // blk
    dh = LANES
    nh = min(nh, heads)
    w = nh * dh
    q0, k0 = q_col // w, (q_col + heads * dh) // w
    c_col = c.reshape(batch, heads, t, 1)
    c_row = c.reshape(batch, heads, nb, blk)
    return pl.pallas_call(
        functools.partial(_fox_kernel, blk=blk, nh=nh),
        grid=(batch, heads // nh, nb),
        in_specs=[pl.BlockSpec((blk, w), lambda b, h, i: (b * nb + i, q0 + h)),
                  pl.BlockSpec((t, w), lambda b, h, i: (b, k0 + h)),
                  pl.BlockSpec((nb, w, blk), lambda b, h, i: (b, h, 0)),
                  pl.BlockSpec((1, nh, nb, blk), lambda b, h, i: (b, h, 0, 0)),
                  pl.BlockSpec((1, nh, t, 1), lambda b, h, i: (b, h, 0, 0))],
        out_specs=pl.BlockSpec((blk, w), lambda b, h, i: (b * nb + i, h)),
        out_shape=jax.ShapeDtypeStruct((n, heads * dh), BF16),
        scratch_shapes=[pltpu.VMEM((nh, 1, blk), F32), pltpu.VMEM((nh, 1, blk), F32),
                        pltpu.VMEM((nh, dh, blk), F32)],
        compiler_params=_cparams(("parallel", "parallel", "arbitrary")),
        name="fox_attention",
    )(proj, proj, vt, c_row, c_col)


def _first_max(p, idx, valid):
    pm = jnp.where(valid, p, -2.0)
    m = jnp.max(pm, axis=0, keepdims=True)
    first = jnp.min(jnp.where(pm == m, idx, float(p.shape[0])), axis=0, keepdims=True)
    return m, first


def _route_tile(x, wr_ref, br_ref, carry):
    tm = x.shape[0]
    n_experts = br_ref.shape[0]
    per = n_experts // N_GROUPS
    x_hi = x.astype(BF16)
    x_lo = (x - x_hi.astype(F32)).astype(BF16)
    logits = _nt(wr_ref[0], x_hi) + _nt(wr_ref[0], x_lo) + _nt(wr_ref[1], x_hi) + br_ref[...]
    e = jnp.exp(logits - jnp.max(logits, axis=0, keepdims=True))
    probs = e / jnp.sum(e, axis=0, keepdims=True)
    idx = lax.broadcasted_iota(jnp.int32, (n_experts, tm), 0).astype(F32)
    best_score = jnp.full((1, tm), -1.0, F32)
    best_group = jnp.zeros((1, tm), F32)
    for grp in range(N_GROUPS):
        in_g = (idx >= grp * per) & (idx < (grp + 1) * per)
        m1, i1 = _first_max(probs, idx, in_g)
        m2, _ = _first_max(probs, idx, in_g & (idx != i1))
        score = m1 + m2
        take = score > best_score
        best_score = jnp.where(take, score, best_score)
        best_group = jnp.where(take, float(grp), best_group)
    in_sel = (idx >= best_group * per) & (idx < (best_group + 1) * per)
    p1, i1 = _first_max(probs, idx, in_sel)
    p2, i2 = _first_max(probs, idx, in_sel & (idx != i1))
    tot = p1 + p2
    row = lax.broadcasted_iota(jnp.int32, (tm, tm), 0)
    col = lax.broadcasted_iota(jnp.int32, (tm, tm), 1)
    earlier = jnp.where(row < col, 1.0, 0.0).astype(BF16)
    onehot = jnp.where((idx == i1) | (idx == i2), 1.0, 0.0)
    seen = jnp.dot(onehot.astype(BF16), earlier, preferred_element_type=F32) + carry[...]
    r1 = jnp.sum(jnp.where(idx == i1, seen, 0.0), axis=0, keepdims=True)
    r2 = jnp.sum(jnp.where(idx == i2, seen, 0.0), axis=0, keepdims=True)
    zero = jnp.zeros_like(r1)
    carry[...] = carry[...] + jnp.sum(onehot, axis=1, keepdims=True)
    return jnp.concatenate([i1, i2, p1 / tot, p2 / tot, r1, r2, zero, zero], axis=0)


def _row_copy(src_ref, src_row, dst_ref, dst_row, sem):
    return pltpu.make_async_copy(src_ref.at[pl.ds(src_row, 1)], dst_ref.at[pl.ds(dst_row, 1)], sem)


def _dispatch_kernel(zid_ref, pos_ref, x_ref, xs_ref, zbuf, sem, zsem, *, tm):
    def zero_tiles(fn):
        for i in range(zid_ref.shape[0]):
            new_id = zid_ref[i] != zid_ref[max(i - 1, 0)] if i else True

            @pl.when(new_id)
            def _():
                fn(pltpu.make_async_copy(zbuf, xs_ref.at[pl.ds(zid_ref[i] * tm, tm)], zsem))

    @pl.when(pl.program_id(0) == 0)
    def _():
        zbuf[...] = jnp.zeros_like(zbuf)
        zero_tiles(lambda cp: cp.start())
        zero_tiles(lambda cp: cp.wait())

    def rows(fn):
        for r in range(x_ref.shape[0]):
            for s in range(pos_ref.shape[0]):
                fn(_row_copy(x_ref, r, xs_ref, pos_ref[s, r], sem), s)

    rows(lambda cp, s: cp.start(priority=s % DMA_QUEUES))
    rows(lambda cp, s: cp.wait())


def _dispatch(x, pos, zero_ids, n_rows, tm):
    n, d = x.shape
    tok = 256
    grid_spec = pltpu.PrefetchScalarGridSpec(
        num_scalar_prefetch=1,
        grid=(n // tok,),
        in_specs=[pl.BlockSpec((pos.shape[0], tok), lambda i, z: (0, i), memory_space=pltpu.SMEM),
                  pl.BlockSpec((tok, d), lambda i, z: (i, 0))],
        out_specs=pl.BlockSpec(memory_space=pl.ANY),
        scratch_shapes=[pltpu.VMEM((tm, d), x.dtype), pltpu.SemaphoreType.DMA, pltpu.SemaphoreType.DMA],
    )
    return pl.pallas_call(
        functools.partial(_dispatch_kernel, tm=tm),
        grid_spec=grid_spec,
        out_shape=jax.ShapeDtypeStruct((n_rows, d), x.dtype),
        compiler_params=_cparams(("arbitrary",)),
        name="moe_dispatch",
    )(zero_ids, pos, x)


def _experts_kernel(te_ref, nx_ref, na_ref, xs_ref, wgu_hbm, wd_hbm, ys_ref, wgu_f, wd_f, wgu_b, wd_b, sems, *, layer):
    j = pl.program_id(0)
    live = j < na_ref[0]
    fresh = (j == 0) | (te_ref[j] != te_ref[jnp.maximum(j - 1, 0)])

    def fetch(e):
        return (pltpu.make_async_copy(wgu_hbm.at[layer, e], wgu_f, sems.at[0]),
                pltpu.make_async_copy(wd_hbm.at[layer, e], wd_f, sems.at[1]))

    @pl.when(live & (j == 0))
    def _():
        for cp in fetch(te_ref[0]):
            cp.start()

    @pl.when(live & fresh)
    def _():
        for cp in fetch(te_ref[j]):
            cp.wait()
        wgu_b[...] = wgu_f[...].astype(BF16)
        wd_b[...] = wd_f[...].astype(BF16)

    @pl.when(live & fresh & (nx_ref[j] >= 0))
    def _():
        for cp in fetch(nx_ref[j]):
            cp.start()

    @pl.when(live)
    def _():
        de = wd_b.shape[0]
        gu = jnp.dot(xs_ref[...].astype(BF16), wgu_b[...], preferred_element_type=F32)
        gpart, upart = gu[:, :de], gu[:, de:]
        h = (gpart / (1.0 + jnp.exp(-gpart))) * upart
        ys_ref[...] = jnp.dot(h.astype(BF16), wd_b[...], preferred_element_type=F32)

    @pl.when(jnp.logical_not(live))
    def _():
        ys_ref[...] = jnp.zeros_like(ys_ref)


def _experts(xs, w_gu, w_down, layer, tile_expert, next_expert, n_active, tm):
    p, d = xs.shape
    de = w_down.shape[2]
    live = lambda j, na: jnp.maximum(jnp.minimum(j, na[0] - 1), 0)
    grid_spec = pltpu.PrefetchScalarGridSpec(
        num_scalar_prefetch=3,
        grid=(p // tm,),
        in_specs=[pl.BlockSpec((tm, d), lambda j, te, nx, na: (live(j, na), 0)),
                  pl.BlockSpec(memory_space=pl.ANY), pl.BlockSpec(memory_space=pl.ANY)],
        out_specs=pl.BlockSpec((tm, d), lambda j, te, nx, na: (j, 0)),
        scratch_shapes=[pltpu.VMEM((d, 2 * de), F32), pltpu.VMEM((de, d), F32), pltpu.VMEM((d, 2 * de), BF16),
                        pltpu.VMEM((de, d), BF16), pltpu.SemaphoreType.DMA((2,))],
    )
    return pl.pallas_call(
        functools.partial(_experts_kernel, layer=layer),
        grid_spec=grid_spec,
        out_shape=jax.ShapeDtypeStruct((p, d), F32),
        compiler_params=_cparams(("arbitrary",)),
        name="moe_experts",
    )(tile_expert, next_expert, n_active, xs, w_gu, w_down)


def _combine_kernel(pos_ref, pos_next_ref, x_ref, gw_ref, ys_ref, g_ref, b_ref, o_ref, buf, sems, *, alpha):
    i = pl.program_id(0)
    slot = i % 2
    tm = x_ref.shape[0]
    n_slots = pos_ref.shape[0]

    def rows(p_ref, sl, fn):
        for r in range(tm):
            for s in range(n_slots):
                fn(_row_copy(ys_ref, p_ref[s, r], buf.at[sl, s], r, sems.at[sl]), s)

    start = lambda cp, s: cp.start(priority=s % DMA_QUEUES)

    @pl.when(i == 0)
    def _():
        rows(pos_ref, 0, start)

    @pl.when(i + 1 < pl.num_programs(0))
    def _():
        rows(pos_next_ref, 1 - slot, start)

    rows(pos_ref, slot, lambda cp, s: cp.wait())
    gw = gw_ref[...]
    y = alpha * x_ref[...]
    for s in range(n_slots):
        y = y + gw[:, s:s + 1] * buf[slot, s]
    o_ref[...] = _layer_norm(y, g_ref[...], b_ref[...], LN_EPS)


def _combine_ln(x, ys, pos, gw, g, b, alpha, tm=256):
    n, d = x.shape
    n_slots = pos.shape[0]
    nt = n // tm
    return pl.pallas_call(
        functools.partial(_combine_kernel, alpha=alpha),
        grid=(nt,),
        in_specs=[pl.BlockSpec((n_slots, tm), lambda i: (0, i), memory_space=pltpu.SMEM),
                  pl.BlockSpec((n_slots, tm), lambda i: (0, jnp.minimum(i + 1, nt - 1)), memory_space=pltpu.SMEM),
                  pl.BlockSpec((tm, d), lambda i: (i, 0)), pl.BlockSpec((tm, n_slots), lambda i: (i, 0)),
                  pl.BlockSpec(memory_space=pl.ANY), pl.BlockSpec((1, d), lambda i: (0, 0)),
                  pl.BlockSpec((1, d), lambda i: (0, 0))],
        out_specs=pl.BlockSpec((tm, d), lambda i: (i, 0)),
        out_shape=jax.ShapeDtypeStruct((n, d), F32),
        scratch_shapes=[pltpu.VMEM((2, n_slots, tm, d), F32), pltpu.SemaphoreType.DMA((2,))],
        compiler_params=_cparams(("arbitrary",)),
        name="moe_combine",
    )(pos, pos, x, gw, ys, g.reshape(1, d), b.reshape(1, d))


def _moe_ln(x, meta, counts, w_gu, w_down, layer, g, b, alpha, tm=256):
    n, d = x.shape
    n_experts = w_down.shape[1]
    eid = meta[0:2].astype(jnp.int32)
    gw = meta[2:4].T
    rank = meta[4:6].astype(jnp.int32)
    counts = counts[:, 0].astype(jnp.int32)
    tiles = (counts + tm - 1) // tm
    tile_end = jnp.cumsum(tiles)
    row_off = (tile_end - tiles) * tm
    n_tiles = (eid.size + tm - 1) // tm + n_experts
    tile_expert = jnp.sum(jnp.arange(n_tiles)[:, None] >= tile_end[None, :], axis=1).astype(jnp.int32)
    tile_expert = jnp.minimum(tile_expert, n_experts - 1)
    n_active = tile_end[-1:].astype(jnp.int32)
    ids = jnp.arange(n_experts)
    later = (ids[None, :] > ids[:, None]) & (tiles[None, :] > 0)
    next_of = jnp.min(jnp.where(later, ids[None, :], n_experts), axis=1)
    next_of = jnp.where(next_of < n_experts, next_of, -1).astype(jnp.int32)
    lookup = lambda table, ids: jnp.sum(jnp.where(ids[..., None] == jnp.arange(n_experts), table, 0), axis=-1)
    next_expert = lookup(next_of, tile_expert)
    pos = lookup(row_off, eid) + rank
    last_tile = jnp.maximum(tile_end - 1, 0)
    idle_tile = jnp.minimum(tile_end[-1] + jnp.arange(n_experts), n_tiles - 1)
    zero_ids = jnp.concatenate([last_tile, idle_tile]).astype(jnp.int32)
    xs = _dispatch(x, pos, zero_ids, n_tiles * tm, tm)
    ys = _experts(xs, w_gu, w_down, layer, tile_expert, next_expert, n_active, tm)
    return _combine_ln(x, ys, pos, gw, g, b, alpha)


def _token_shift(x_ref, prev_ref, first):
    x = x_ref[...]
    prev_row = jnp.where(first, 0.0, prev_ref[7:8, :])
    row = lax.broadcasted_iota(jnp.int32, x.shape, 0)
    return x, jnp.where(row == 0, prev_row, pltpu.roll(x, 1, 0))


def _rkv_kernel(x_ref, prev_ref, mu_ref, w_ref, o_ref, *, tiles_per_seq):
    first = pl.program_id(0) % tiles_per_seq == 0
    x, xp = _token_shift(x_ref, prev_ref, first)
    xm = x + (xp - x) * mu_ref[0]
    o_ref[0] = jnp.dot(xm.astype(BF16), w_ref[0], preferred_element_type=F32).astype(o_ref.dtype)


def _rkv_proj(x, mu3, w_rkv, t, tm=1024):
    n, d = x.shape
    sub = tm // 8
    return pl.pallas_call(
        functools.partial(_rkv_kernel, tiles_per_seq=t // tm),
        grid=(n // tm, 3),
        in_specs=[pl.BlockSpec((tm, d), lambda i, j: (i, 0)),
                  pl.BlockSpec((8, d), lambda i, j: (jnp.maximum(i * sub - 1, 0), 0)),
                  pl.BlockSpec((1, 1, d), lambda i, j: (j, 0, 0)), pl.BlockSpec((1, d, d), lambda i, j: (j, 0, 0))],
        out_specs=pl.BlockSpec((1, tm, d), lambda i, j: (j, i, 0)),
        out_shape=jax.ShapeDtypeStruct((3, n, d), BF16),
        compiler_params=_cparams(("parallel", "arbitrary")),
        name="rkv_proj",
    )(x, x, mu3.reshape(3, 1, d), w_rkv)


def _lora_kernel(x_ref, prev_ref, mu_ref, w0_ref, w1_ref, w2_ref, a0_ref, a1_ref, a2_ref, g1_ref, g2_ref, wl_ref,
                 a_ref, g_ref, *, tiles_per_seq):
    first = pl.program_id(0) % tiles_per_seq == 0
    x, xp = _token_shift(x_ref, prev_ref, first)
    dx = xp - x

    def mm(a, w_ref):
        return jnp.dot(a.astype(BF16), w_ref[...], preferred_element_type=F32)

    zw = w0_ref[...] + mm(jnp.tanh(mm(x + dx * mu_ref[0:1, :], w1_ref)), w2_ref)
    wl_ref[...] = -HALF_DECAY * _sigmoid(zw)
    za = a0_ref[...] + mm(mm(x + dx * mu_ref[1:2, :], a1_ref), a2_ref)
    a_ref[...] = _sigmoid(za).astype(a_ref.dtype)
    zg = mm(x + dx * mu_ref[2:3, :], g1_ref)
    g_ref[...] = mm(_sigmoid(zg), g2_ref).astype(g_ref.dtype)


def _lora(x, mu3, w0, w1, w2, a0, a1, a2, g1, g2, t, tm=512):
    n, d = x.shape
    sub = tm // 8
    const = lambda shape: pl.BlockSpec(shape, lambda i: (0, 0))
    row = pl.BlockSpec((tm, d), lambda i: (i, 0))
    return pl.pallas_call(
        functools.partial(_lora_kernel, tiles_per_seq=t // tm),
        grid=(n // tm,),
        in_specs=[row, pl.BlockSpec((8, d), lambda i: (jnp.maximum(i * sub - 1, 0), 0)), const((3, d)),
                  const((1, d)), const(w1.shape), const(w2.shape), const((1, d)), const(a1.shape), const(a2.shape),
                  const(g1.shape), const(g2.shape)],
        out_specs=[row, row, row],
        out_shape=[jax.ShapeDtypeStruct((n, d), F32), jax.ShapeDtypeStruct((n, d), BF16),
                   jax.ShapeDtypeStruct((n, d), BF16)],
        compiler_params=_cparams(("parallel",)),
        name="lora",
    )(x, x, mu3, w0.reshape(1, d), w1, w2, a0.reshape(1, d), a1, a2, g1, g2)


def _scan_groups(r, k, v, lw, a_gate, g, kk_w, ka_w, rk_w, gg, gb, s0, *, hd, gn_eps):
    c = r[0].shape[0]
    nh = LANES // hd
    hc = nh * c
    each = lambda f, *xs: [f(*x) for x in zip(*xs)]
    lane = lax.broadcasted_iota(jnp.int32, (1, LANES), 1)
    head_masks = [(lane >= h * hd) & (lane < (h + 1) * hd) for h in range(nh)]
    lr = lax.broadcasted_iota(jnp.int32, (LANES, LANES), 0) // hd
    lc = lax.broadcasted_iota(jnp.int32, (LANES, LANES), 1) // hd
    same_head = jnp.where(lr == lc, 1.0, 0.0).astype(BF16)
    trow = lax.broadcasted_iota(jnp.int32, (c, c), 0)
    tcol = lax.broadcasted_iota(jnp.int32, (c, c), 1)
    lower = jnp.where(trow >= tcol, 1.0, 0.0).astype(BF16)
    prow = lax.broadcasted_iota(jnp.int32, (hc, hc), 0)
    pcol = lax.broadcasted_iota(jnp.int32, (hc, hc), 1)
    same_blk = (prow // c) == (pcol // c)
    strict = same_blk & (prow > pcol)
    incl = same_blk & (prow >= pcol)
    eye = jnp.where(prow == pcol, 1.0, 0.0).astype(F32)

    def split(x):
        hi = x.astype(BF16)
        return hi, (x - hi.astype(F32)).astype(BF16)

    def head_sum(xs):
        return [jnp.dot(x.astype(BF16), same_head, preferred_element_type=F32) for x in xs]

    def per_head(x):
        return jnp.concatenate([jnp.where(hm, x, 0.0) for hm in head_masks], axis=0).astype(BF16)

    kk = each(lambda k_, w_: k_ * w_, k, kk_w)
    kk_n = head_sum(each(lambda x: x * x, kk))
    kk = each(lambda x, n_: x / jnp.maximum(jnp.sqrt(n_), 1e-12), kk, kk_n)
    k2 = each(lambda k_, a_, w_: k_ * (1.0 + (a_ - 1.0) * w_), k, a_gate, ka_w)
    lw_parts = each(split, lw)
    cum = [jnp.dot(lower, hi, preferred_element_type=F32) + jnp.dot(lower, lo, preferred_element_type=F32)
           for hi, lo in lw_parts]
    gam = each(jnp.exp, cum)
    inv_gam = each(lambda x: jnp.exp(-x), cum)
    gam_end = each(lambda x: x[c - 1:c, :], gam)
    a_t = each(lambda kk_, cum_, lw_: -kk_ * jnp.exp(cum_ - lw_), kk, cum, lw)
    b_t = each(lambda kk_, a_, ig: kk_ * a_ * ig, kk, a_gate, inv_gam)
    k_t = each(lambda k2_, ig: k2_ * ig, k2, inv_gam)
    r_t = each(lambda r_, gm: r_ * gm, r, gam)

    s0b = each(lambda x: x.astype(BF16), s0)
    ar_s0 = each(lambda a_, r_, s_: _nt(jnp.concatenate([a_, r_], axis=0).astype(BF16), s_), a_t, r_t, s0b)
    ar2 = each(lambda a_, r_: jnp.concatenate([per_head(a_), per_head(r_)], axis=0), a_t, r_t)
    bk2 = each(lambda b_, k_: jnp.concatenate([per_head(b_), per_head(k_)], axis=0), b_t, k_t)
    v2 = each(per_head, v)
    gmat = each(_nt, ar2, bk2)
    a_ab = each(lambda x: jnp.where(strict, x[:hc, :hc], 0.0), gmat)
    a_ak = each(lambda x: jnp.where(strict, x[:hc, hc:], 0.0), gmat)
    a_r = each(lambda x: jnp.concatenate([jnp.where(incl, x[hc:, :hc], 0.0), jnp.where(incl, x[hc:, hc:], 0.0)],
                                         axis=1).astype(BF16), gmat)
    rhs = each(lambda as0, ak, v_: jnp.concatenate([jnp.where(hm, as0[:c], 0.0) for hm in head_masks], axis=0)
               + _bdot(ak, v_), ar_s0, a_ak, v2)
    inv = each(lambda x: eye + x, a_ab)
    m = each(lambda x: _bdot(x, x), a_ab)
    n_pow = 2
    while 2 * n_pow < c:
        mp = each(lambda m_, p_: _bdot(jnp.concatenate([m_, p_], axis=0), m_), m, inv)
        inv = each(lambda p_, mp_: p_ + mp_[hc:], inv, mp)
        m = each(lambda mp_: mp_[:hc], mp)
        n_pow *= 2
    rhs = each(lambda rhs_, m_: rhs_ + _bdot(m_, rhs_), rhs, m)
    u2 = each(_bdot, inv, rhs)
    uv = each(lambda u_, v_: jnp.concatenate([u_.astype(BF16), v_], axis=0), u2, v2)
    y2 = each(lambda ar_, uv_: jnp.dot(ar_, uv_, preferred_element_type=F32), a_r, uv)
    y = each(lambda rs, y2_: rs[c:] + sum(y2_[h * c:(h + 1) * c, :] for h in range(nh)), ar_s0, y2)
    bkg = each(lambda b_, k_, ge: jnp.concatenate([per_head(b_ * ge), per_head(k_ * ge)], axis=0), b_t, k_t, gam_end)
    s_new = each(lambda s_, ge, uv_, bkg_: s_ * ge + _tn(uv_, bkg_), s0, gam_end, uv, bkg)

    inv_hd = 1.0 / hd
    mean = head_sum(y)
    yc = each(lambda y_, m_: y_ - m_ * inv_hd, y, mean)
    var = head_sum(each(lambda x: x * x, yc))
    yn = each(lambda yc_, var_, gg_, gb_: yc_ * lax.rsqrt(var_ * inv_hd + gn_eps) * gg_ + gb_, yc, var, gg, gb)
    rk_sum = head_sum(each(lambda r_, k2_, w_: r_ * k2_ * w_, r, k2, rk_w))
    out = each(lambda yn_, rk_, v_, g_: (yn_ + rk_ * v_) * g_, yn, rk_sum, v, g)
    return out, s_new


def _scan_kernel(r_ref, k_ref, v_ref, wl_ref, a_ref, g_ref, kk_ref, ka_ref, rk_ref, gg_ref, gb_ref, o_ref, state, *,
                 hd, gn_eps):
    @pl.when(pl.program_id(2) == 0)
    def _():
        state[...] = jnp.zeros_like(state)

    ng = state.shape[0]
    sls = [slice(p * LANES, (p + 1) * LANES) for p in range(ng)]
    tok3 = lambda ref: [ref[0, :, sl].astype(F32) for sl in sls]
    tok2 = lambda ref: [ref[:, sl].astype(F32) for sl in sls]
    out, s_new = _scan_groups(tok3(r_ref), tok3(k_ref), tok3(v_ref), tok2(wl_ref), tok2(a_ref), tok2(g_ref),
                              tok2(kk_ref), tok2(ka_ref), tok2(rk_ref), tok2(gg_ref), tok2(gb_ref),
                              [state[p] for p in range(ng)], hd=hd, gn_eps=gn_eps)
    for p in range(ng):
        state[p] = s_new[p]
        o_ref[:, sls[p]] = out[p].astype(o_ref.dtype)


def _rwkv_scan(rkv, wl, a, g, k_k, k_a, r_k, gn_g, gn_b, batch, hd, gn_eps, chunk=64, groups=16):
    _, n, d = rkv.shape
    t = n // batch
    nc = t // chunk
    groups = min(groups, d // LANES)
    w = groups * LANES
    tok = lambda j: pl.BlockSpec((1, chunk, w), functools.partial(lambda b, p, c, j: (j, b * nc + c, p), j=j))
    tok2 = pl.BlockSpec((chunk, w), lambda b, p, c: (b * nc + c, p))
    par = pl.BlockSpec((1, w), lambda b, p, c: (0, p))
    return pl.pallas_call(
        functools.partial(_scan_kernel, hd=hd, gn_eps=gn_eps),
        grid=(batch, d // w, nc),
        in_specs=[tok(0), tok(1), tok(2), tok2, tok2, tok2, par, par, par, par, par],
        out_specs=tok2,
        out_shape=jax.ShapeDtypeStruct((n, d), BF16),
        scratch_shapes=[pltpu.VMEM((groups, LANES, LANES), F32)],
        compiler_params=_cparams(("parallel", "parallel", "arbitrary")),
        name="rwkv_scan",
    )(rkv, rkv, rkv, wl, a, g, k_k.reshape(1, d), k_a.reshape(1, d), r_k.reshape(1, d), gn_g.reshape(1, d),
      gn_b.reshape(1, d))


def _pad_lora(w_in, w_out):
    r = w_in.shape[1]
    rp = -(-r // LANES) * LANES
    return (jnp.pad(w_in, ((0, 0), (0, rp - r))).astype(BF16), jnp.pad(w_out, ((0, rp - r), (0, 0))).astype(BF16))


def kernel(x, ev_w_in, ev_b_a, ev_w_s, ev_b_s, ev_g_v, ev_b_v, ev_b_f, ev_w_out, rw_mu, rw_w_rkv, rw_w0, rw_w1, rw_w2, rw_a0, rw_a1, rw_a2, rw_g1, rw_g2, rw_k_k, rw_k_a, rw_r_k, rw_gn_g, rw_gn_b, rw_w_o, ln_g, ln_b, w_router, b_router, w_gu, w_down):
    batch, t, d = x.shape
    depth = ln_g.shape[0]
    alpha = (2 * depth) ** 0.25
    h = x.reshape(batch * t, d)
    for layer in range(depth):
        i = layer // 2
        if layer % 2 == 0:
            aw = ev_g_v.shape[1]
            heads = ev_b_f.shape[1]
            q_col = 2 * aw
            bw = heads * LANES
            k_col, v_col, f_col = q_col + bw, q_col + 2 * bw, q_col + 3 * bw
            w_in = ev_w_in[i]
            col = jnp.arange(v_col)
            q_scale = jnp.where((col >= q_col) & (col < k_col), LANES ** -0.5 * LOG2E, 1.0)
            proj = _matmul(h, (w_in[:, :v_col] * q_scale).astype(BF16), v_col, BF16)
            vt, f_logit = _proj_transposed(h, w_in[:, v_col:f_col].T.astype(BF16), w_in[:, f_col:].T, FOX_BLOCK)
            c = _fgate(f_logit, ev_b_f[i], batch)
            y_a = _sgu(proj, ev_b_a[i], ev_w_s[i], ev_b_s[i], ev_g_v[i], ev_b_v[i])
            y_b = _fox_attention(proj, vt, c, batch, heads, q_col, FOX_BLOCK)
            h, meta, counts = _proj_ln_route([y_a, y_b], ev_w_out[i].astype(BF16), h, ln_g[layer, 0],
                                             ln_b[layer, 0], alpha, w_router, b_router)
        else:
            hd = rw_r_k.shape[2]
            mu = rw_mu[i]
            rkv = _rkv_proj(h, mu[:3], rw_w_rkv[i].astype(BF16), t)
            w1, w2 = _pad_lora(rw_w1[i], rw_w2[i])
            a1, a2 = _pad_lora(rw_a1[i], rw_a2[i])
            g1, g2 = _pad_lora(rw_g1[i], rw_g2[i])
            wl, a, g = _lora(h, mu[3:], rw_w0[i], w1, w2, rw_a0[i], a1, a2, g1, g2, t)
            y = _rwkv_scan(rkv, wl, a, g, rw_k_k[i], rw_k_a[i], rw_r_k[i].reshape(-1), rw_gn_g[i], rw_gn_b[i],
                           batch, hd, hd * 1e-5)
            h, meta, counts = _proj_ln_route([y], rw_w_o[i].astype(BF16), h, ln_g[layer, 0], ln_b[layer, 0], alpha,
                                             w_router, b_router)
        h = _moe_ln(h, meta, counts, w_gu, w_down, layer, ln_g[layer, 1], ln_b[layer, 1], alpha)
    return h.reshape(batch, t, d)
```

```python
import functools

import jax
import jax.numpy as jnp
from jax import lax
from jax.experimental import pallas as pl
from jax.experimental.pallas import tpu as pltpu

F32 = jnp.float32
BF16 = jnp.bfloat16
HI = lax.Precision.HIGHEST

LN_EPS = 1e-5
N_GROUPS = 4
LANES = 128
FOX_BLOCK = 512
LOG2E = 1.4426950408889634
HALF_DECAY = 0.6065306597126334
DMA_QUEUES = 2
VMEM_LIMIT = 56 * 1024 * 1024


def _cparams(sem):
    return pltpu.CompilerParams(dimension_semantics=sem, vmem_limit_bytes=VMEM_LIMIT)


def _layer_norm(x, g, b, eps):
    mu = jnp.mean(x, -1, keepdims=True)
    xc = x - mu
    var = jnp.mean(xc * xc, -1, keepdims=True)
    return xc * lax.rsqrt(var + eps) * g + b


def _sigmoid(x):
    return 0.5 + 0.5 * jnp.tanh(0.5 * x)


def _nt(a, b, **kw):
    return lax.dot_general(a, b, (((1,), (1,)), ((), ())), preferred_element_type=F32, **kw)


def _tn(a, b, **kw):
    return lax.dot_general(a, b, (((0,), (0,)), ((), ())), preferred_element_type=F32, **kw)


def _bdot(a, b):
    return jnp.dot(a.astype(BF16), b.astype(BF16), preferred_element_type=F32)


def _mm_kernel(a_ref, w_ref, o_ref, a_bf16):
    @pl.when(pl.program_id(1) == 0)
    def _():
        a_bf16[...] = a_ref[...].astype(BF16)

    o_ref[...] = jnp.dot(a_bf16[...], w_ref[...], preferred_element_type=F32).astype(o_ref.dtype)


def _matmul(a, w, n_cols, out_dtype, tm=1024, tn=1024):
    m, k = a.shape
    tm = min(tm, m)
    tn = next(c for c in (tn, 256, LANES) if n_cols % c == 0)
    return pl.pallas_call(
        _mm_kernel,
        grid=(m // tm, n_cols // tn),
        in_specs=[pl.BlockSpec((tm, k), lambda i, j: (i, 0)), pl.BlockSpec((k, tn), lambda i, j: (0, j))],
        out_specs=pl.BlockSpec((tm, tn), lambda i, j: (i, j)),
        out_shape=jax.ShapeDtypeStruct((m, n_cols), out_dtype),
        scratch_shapes=[pltpu.VMEM((tm, k), BF16)],
        compiler_params=_cparams(("parallel", "arbitrary")),
        name="matmul",
    )(a, w)


def _proj_ln_kernel(*refs, n_in, alpha):
    a_refs, w_refs = refs[:n_in], refs[n_in:2 * n_in]
    res_ref, g_ref, b_ref, wr_ref, br_ref, o_ref, meta_ref, cnt_ref, carry = refs[2 * n_in:]

    @pl.when(pl.program_id(0) == 0)
    def _():
        carry[...] = jnp.zeros_like(carry)

    acc = alpha * res_ref[...]
    for a_ref, w_ref in zip(a_refs, w_refs):
        acc = acc + jnp.dot(a_ref[...], w_ref[...], preferred_element_type=F32)
    y = _layer_norm(acc, g_ref[...], b_ref[...], LN_EPS)
    o_ref[...] = y
    meta_ref[...] = _route_tile(y, wr_ref, br_ref, carry)
    cnt_ref[...] = jnp.broadcast_to(carry[...], cnt_ref.shape)


def _proj_ln_route(a_list, w, res, g, b, alpha, w_router, b_router, tm=512):
    m, d = res.shape
    n_in = len(a_list)
    kc = a_list[0].shape[1]
    n_experts = w_router.shape[1]
    wr = w_router.T
    wr_hi = wr.astype(BF16)
    wr = jnp.stack([wr_hi, (wr - wr_hi.astype(F32)).astype(BF16)])
    in_specs = [pl.BlockSpec((tm, kc), lambda i: (i, 0)) for _ in a_list]
    in_specs += [pl.BlockSpec((kc, d), functools.partial(lambda i, r: (r, 0), r=r)) for r in range(n_in)]
    in_specs += [pl.BlockSpec((tm, d), lambda i: (i, 0)), pl.BlockSpec((1, d), lambda i: (0, 0)),
                 pl.BlockSpec((1, d), lambda i: (0, 0)), pl.BlockSpec((2, n_experts, d), lambda i: (0, 0, 0)),
                 pl.BlockSpec((n_experts, 1), lambda i: (0, 0))]
    return pl.pallas_call(
        functools.partial(_proj_ln_kernel, n_in=n_in, alpha=alpha),
        grid=(m // tm,),
        in_specs=in_specs,
        out_specs=[pl.BlockSpec((tm, d), lambda i: (i, 0)), pl.BlockSpec((8, tm), lambda i: (0, i)),
                   pl.BlockSpec((n_experts, LANES), lambda i: (0, 0))],
        out_shape=[jax.ShapeDtypeStruct((m, d), F32), jax.ShapeDtypeStruct((8, m), F32),
                   jax.ShapeDtypeStruct((n_experts, LANES), F32)],
        scratch_shapes=[pltpu.VMEM((n_experts, 1), F32)],
        compiler_params=_cparams(("arbitrary",)),
        name="proj_ln_route",
    )(*a_list, *([w] * n_in), res, g.reshape(1, d), b.reshape(1, d), wr, b_router.reshape(n_experts, 1))


def _fgate_kernel(z_ref, bf_ref, c_ref, carry):
    @pl.when(pl.program_id(1) == 0)
    def _():
        carry[...] = jnp.zeros_like(carry)

    tm = z_ref.shape[2]
    z = z_ref[0] + bf_ref[...]
    log_f = jnp.minimum(z, 0.0) - jnp.log1p(jnp.exp(-jnp.abs(z)))
    row = lax.broadcasted_iota(jnp.int32, (tm, tm), 0)
    col = lax.broadcasted_iota(jnp.int32, (tm, tm), 1)
    upper = jnp.where(row <= col, 1.0, 0.0).astype(F32)
    c = jnp.dot(log_f, upper, preferred_element_type=F32, precision=HI) + carry[...]
    c_ref[0] = c
    carry[...] = carry[...] + jnp.sum(log_f, axis=-1, keepdims=True)


def _fgate(f_logit, b_f, batch):
    tiles, h, tm = f_logit.shape
    nt = tiles // batch
    return pl.pallas_call(
        _fgate_kernel,
        grid=(batch, nt),
        in_specs=[pl.BlockSpec((1, h, tm), lambda b, i: (b * nt + i, 0, 0)), pl.BlockSpec((h, 1), lambda b, i: (0, 0))],
        out_specs=pl.BlockSpec((1, h, tm), lambda b, i: (b, 0, i)),
        out_shape=jax.ShapeDtypeStruct((batch, h, nt * tm), F32),
        scratch_shapes=[pltpu.VMEM((h, 1), F32)],
        compiler_params=_cparams(("parallel", "arbitrary")),
        name="fgate",
    )(f_logit, b_f.reshape(h, 1))


def _gelu_tanh(x):
    return 0.5 * x * (1.0 + jnp.tanh(0.7978845608028654 * (x + 0.044715 * (x * x * x))))


def _sgu_kernel(z_ref, ba_ref, ws_ref, bs_ref, gv_ref, bv_ref, o_ref, *, chunk, groups):
    aw = o_ref.shape[1]
    gd = aw // groups
    z = _gelu_tanh(z_ref[...].astype(F32) + ba_ref[...])
    u = z[:, :aw]
    v = _layer_norm(z[:, aw:], gv_ref[...], bv_ref[...], LN_EPS).astype(BF16)
    row = lax.broadcasted_iota(jnp.int32, (chunk, chunk), 0)
    col = lax.broadcasted_iota(jnp.int32, (chunk, chunk), 1)
    causal = row >= col
    bs = bs_ref[...]
    for g in range(groups):
        w_g = jnp.where(causal, ws_ref[g], 0.0).astype(BF16)
        for c in range(z.shape[0] // chunk):
            rs = slice(c * chunk, (c + 1) * chunk)
            cs = slice(g * gd, (g + 1) * gd)
            s = jnp.dot(w_g, v[rs, cs], preferred_element_type=F32) + bs[:, g:g + 1]
            o_ref[rs, cs] = (u[rs, cs] * s).astype(o_ref.dtype)


def _sgu(proj, b_a, w_s, b_s, g_v, b_v, tm=512):
    n = proj.shape[0]
    groups, chunk, _ = w_s.shape
    aw = g_v.shape[0]
    return pl.pallas_call(
        functools.partial(_sgu_kernel, chunk=chunk, groups=groups),
        grid=(n // tm,),
        in_specs=[pl.BlockSpec((tm, 2 * aw), lambda i: (i, 0)), pl.BlockSpec((1, 2 * aw), lambda i: (0, 0)),
                  pl.BlockSpec((groups, chunk, chunk), lambda i: (0, 0, 0)),
                  pl.BlockSpec((chunk, groups), lambda i: (0, 0)), pl.BlockSpec((1, aw), lambda i: (0, 0)),
                  pl.BlockSpec((1, aw), lambda i: (0, 0))],
        out_specs=pl.BlockSpec((tm, aw), lambda i: (i, 0)),
        out_shape=jax.ShapeDtypeStruct((n, aw), BF16),
        compiler_params=_cparams(("parallel",)),
        name="sgu",
    )(proj, b_a.reshape(1, -1), w_s, b_s.T, g_v.reshape(1, aw), b_v.reshape(1, aw))


def _vt_kernel(x_ref, w_ref, wf_ref, o_ref, f_ref):
    x = x_ref[...]
    x_hi = x.astype(BF16)
    x_lo = (x - x_hi.astype(F32)).astype(BF16)
    o_ref[0] = _nt(w_ref[...], x_hi).astype(o_ref.dtype)
    f_ref[0] = _nt(wf_ref[0], x_hi) + _nt(wf_ref[0], x_lo) + _nt(wf_ref[1], x_hi)


def _proj_transposed(x, w_t, wf_t, blk):
    n, d = x.shape
    rows, rows_f = w_t.shape[0], wf_t.shape[0]
    wf_hi = wf_t.astype(BF16)
    wf = jnp.stack([wf_hi, (wf_t - wf_hi.astype(F32)).astype(BF16)])
    return pl.pallas_call(
        _vt_kernel,
        grid=(n // blk,),
        in_specs=[pl.BlockSpec((blk, d), lambda i: (i, 0)), pl.BlockSpec((rows, d), lambda i: (0, 0)),
                  pl.BlockSpec((2, rows_f, d), lambda i: (0, 0, 0))],
        out_specs=[pl.BlockSpec((1, rows, blk), lambda i: (i, 0, 0)),
                   pl.BlockSpec((1, rows_f, blk), lambda i: (i, 0, 0))],
        out_shape=[jax.ShapeDtypeStruct((n // blk, rows, blk), BF16),
                   jax.ShapeDtypeStruct((n // blk, rows_f, blk), F32)],
        compiler_params=_cparams(("parallel",)),
        name="proj_transposed",
    )(x, w_t, wf)


def _fox_kernel(q_ref, k_ref, vt_ref, cq_ref, ck_ref, o_ref, m_scr, l_scr, acc_scr, *, blk, nh):
    qi = pl.program_id(2)
    heads = range(nh)
    hs = [slice(h * LANES, (h + 1) * LANES) for h in heads]
    m_scr[...] = jnp.full_like(m_scr, -jnp.inf)
    l_scr[...] = jnp.zeros_like(l_scr)
    acc_scr[...] = jnp.zeros_like(acc_scr)
    q = [q_ref[:, hs[h]] for h in heads]
    cq = [cq_ref[0, h, pl.ds(qi, 1), :] * LOG2E for h in heads]

    def step(ki, masked):
        ks = pl.multiple_of(ki * blk, blk)
        t = [_nt(k_ref[pl.ds(ks, blk), hs[h]], q[h]) - ck_ref[0, h, pl.ds(ks, blk), :] * LOG2E for h in heads]
        if masked:
            row = lax.broadcasted_iota(jnp.int32, (blk, blk), 0)
            col = lax.broadcasted_iota(jnp.int32, (blk, blk), 1)
            t = [jnp.where(row <= col, x, -jnp.inf) for x in t]
        m_prev = [m_scr[h] for h in heads]
        m_new = [jnp.maximum(m_prev[h], cq[h] + jnp.max(t[h], axis=0, keepdims=True)) for h in heads]
        p = [jnp.exp2(t[h] - (m_new[h] - cq[h])) for h in heads]
        corr = [jnp.exp2(m_prev[h] - m_new[h]) for h in heads]
        for h in heads:
            l_scr[h] = corr[h] * l_scr[h] + jnp.sum(p[h], axis=0, keepdims=True)
            acc_scr[h] = corr[h] * acc_scr[h] + jnp.dot(vt_ref[ki, hs[h], :], p[h].astype(BF16),
                                                         preferred_element_type=F32)
            m_scr[h] = m_new[h]

    def body(ki, carry):
        step(ki, False)
        return carry

    lax.fori_loop(0, qi, body, 0)
    step(qi, True)
    for h in heads:
        o_ref[:, hs[h]] = jnp.transpose(acc_scr[h] / l_scr[h]).astype(o_ref.dtype)


def _fox_attention(proj, vt, c, batch, heads, q_col, blk, nh=4):
    n = proj.shape[0]
    t = n // batch
    nb = t // blk
    dh = LANES
    nh = min(nh, heads)
    w = nh * dh
    q0, k0 = q_col // w, (q_col + heads * dh) // w
    c_col = c.reshape(batch, heads, t, 1)
    c_row = c.reshape(batch, heads, nb, blk)
    return pl.pallas_call(
        functools.partial(_fox_kernel, blk=blk, nh=nh),
        grid=(batch, heads // nh, nb),
        in_specs=[pl.BlockSpec((blk, w), lambda b, h, i: (b * nb + i, q0 + h)),
                  pl.BlockSpec((t, w), lambda b, h, i: (b, k0 + h)),
                  pl.BlockSpec((nb, w, blk), lambda b, h, i: (b, h, 0)),
                  pl.BlockSpec((1, nh, nb, blk), lambda b, h, i: (b, h, 0, 0)),
                  pl.BlockSpec((1, nh, t, 1), lambda b, h, i: (b, h, 0, 0))],
        out_specs=pl.BlockSpec((blk, w), lambda b, h, i: (b * nb + i, h)),
        out_shape=jax.ShapeDtypeStruct((n, heads * dh), BF16),
        scratch_shapes=[pltpu.VMEM((nh, 1, blk), F32), pltpu.VMEM((nh, 1, blk), F32),
                        pltpu.VMEM((nh, dh, blk), F32)],
        compiler_params=_cparams(("parallel", "parallel", "arbitrary")),
        name="fox_attention",
    )(proj, proj, vt, c_row, c_col)


def _first_max(p, idx, valid):
    pm = jnp.where(valid, p, -2.0)
    m = jnp.max(pm, axis=0, keepdims=True)
    first = jnp.min(jnp.where(pm == m, idx, float(p.shape[0])), axis=0, keepdims=True)
    return m, first


def _route_tile(x, wr_ref, br_ref, carry):
    tm = x.shape[0]
    n_experts = br_ref.shape[0]
    per = n_experts // N_GROUPS
    x_hi = x.astype(BF16)
    x_lo = (x - x_hi.astype(F32)).astype(BF16)
    logits = _nt(wr_ref[0], x_hi) + _nt(wr_ref[0], x_lo) + _nt(wr_ref[1], x_hi) + br_ref[...]
    e = jnp.exp(logits - jnp.max(logits, axis=0, keepdims=True))
    probs = e / jnp.sum(e, axis=0, keepdims=True)
    idx = lax.broadcasted_iota(jnp.int32, (n_experts, tm), 0).astype(F32)
    best_score = jnp.full((1, tm), -1.0, F32)
    best_group = jnp.zeros((1, tm), F32)
    for grp in range(N_GROUPS):
        in_g = (idx >= grp * per) & (idx < (grp + 1) * per)
        m1, i1 = _first_max(probs, idx, in_g)
        m2, _ = _first_max(probs, idx, in_g & (idx != i1))
        score = m1 + m2
        take = score > best_score
        best_score = jnp.where(take, score, best_score)
        best_group = jnp.where(take, float(grp), best_group)
    in_sel = (idx >= best_group * per) & (idx < (best_group + 1) * per)
    p1, i1 = _first_max(probs, idx, in_sel)
    p2, i2 = _first_max(probs, idx, in_sel & (idx != i1))
    tot = p1 + p2
    row = lax.broadcasted_iota(jnp.int32, (tm, tm), 0)
    col = lax.broadcasted_iota(jnp.int32, (tm, tm), 1)
    earlier = jnp.where(row < col, 1.0, 0.0).astype(BF16)
    onehot = jnp.where((idx == i1) | (idx == i2), 1.0, 0.0)
    seen = jnp.dot(onehot.astype(BF16), earlier, preferred_element_type=F32) + carry[...]
    r1 = jnp.sum(jnp.where(idx == i1, seen, 0.0), axis=0, keepdims=True)
    r2 = jnp.sum(jnp.where(idx == i2, seen, 0.0), axis=0, keepdims=True)
    zero = jnp.zeros_like(r1)
    carry[...] = carry[...] + jnp.sum(onehot, axis=1, keepdims=True)
    return jnp.concatenate([i1, i2, p1 / tot, p2 / tot, r1, r2, zero, zero], axis=0)


def _row_copy(src_ref, src_row, dst_ref, dst_row, sem):
    return pltpu.make_async_copy(src_ref.at[pl.ds(src_row, 1)], dst_ref.at[pl.ds(dst_row, 1)], sem)


def _dispatch_kernel(zid_ref, pos_ref, x_hbm, xs_ref, zbuf, sems, zsem, *, tm, tok):
    i = pl.program_id(0)
    n_slots = pos_ref.shape[0]

    def zero_tiles(fn):
        for j in range(zid_ref.shape[0]):
            new_id = zid_ref[j] != zid_ref[max(j - 1, 0)] if j else True

            @pl.when(new_id)
            def _():
                fn(pltpu.make_async_copy(zbuf, xs_ref.at[pl.ds(zid_ref[j] * tm, tm)], zsem))

    @pl.when(i == 0)
    def _():
        zbuf[...] = jnp.zeros_like(zbuf)
        zero_tiles(lambda cp: cp.start())
        zero_tiles(lambda cp: cp.wait())

    base = i * tok
    for r in range(tok):
        for s in range(n_slots):
            _row_copy(x_hbm, base + r, xs_ref, pos_ref[s, r], sems.at[i % 2]).start(priority=s % DMA_QUEUES)

    def drain(sem):
        for _ in range(tok * n_slots):
            _row_copy(x_hbm, 0, xs_ref, 0, sem).wait()

    @pl.when(i > 0)
    def _():
        drain(sems.at[1 - i % 2])

    @pl.when(i == pl.num_programs(0) - 1)
    def _():
        drain(sems.at[i % 2])


def _dispatch(x, pos, zero_ids, n_rows, tm, tok=256):
    n, d = x.shape
    grid_spec = pltpu.PrefetchScalarGridSpec(
        num_scalar_prefetch=1,
        grid=(n // tok,),
        in_specs=[pl.BlockSpec((pos.shape[0], tok), lambda i, z: (0, i), memory_space=pltpu.SMEM),
                  pl.BlockSpec(memory_space=pl.ANY)],
        out_specs=pl.BlockSpec(memory_space=pl.ANY),
        scratch_shapes=[pltpu.VMEM((tm, d), x.dtype), pltpu.SemaphoreType.DMA((2,)), pltpu.SemaphoreType.DMA],
    )
    return pl.pallas_call(
        functools.partial(_dispatch_kernel, tm=tm, tok=tok),
        grid_spec=grid_spec,
        out_shape=jax.ShapeDtypeStruct((n_rows, d), x.dtype),
        compiler_params=_cparams(("arbitrary",)),
        name="moe_dispatch",
    )(zero_ids, pos, x)


def _experts_kernel(te_ref, nx_ref, na_ref, xs_ref, wgu_hbm, wd_hbm, ys_ref, wgu_f, wd_f, wgu_b, wd_b, sems, *, layer):
    j = pl.program_id(0)
    live = j < na_ref[0]
    fresh = (j == 0) | (te_ref[j] != te_ref[jnp.maximum(j - 1, 0)])

    def fetch(e):
        return (pltpu.make_async_copy(wgu_hbm.at[layer, e], wgu_f, sems.at[0]),
                pltpu.make_async_copy(wd_hbm.at[layer, e], wd_f, sems.at[1]))

    @pl.when(live & (j == 0))
    def _():
        for cp in fetch(te_ref[0]):
            cp.start()

    @pl.when(live & fresh)
    def _():
        for cp in fetch(te_ref[j]):
            cp.wait()
        wgu_b[...] = wgu_f[...].astype(BF16)
        wd_b[...] = wd_f[...].astype(BF16)

    @pl.when(live & fresh & (nx_ref[j] >= 0))
    def _():
        for cp in fetch(nx_ref[j]):
            cp.start()

    @pl.when(live)
    def _():
        de = wd_b.shape[0]
        gu = jnp.dot(xs_ref[...].astype(BF16), wgu_b[...], preferred_element_type=F32)
        gpart, upart = gu[:, :de], gu[:, de:]
        h = (gpart / (1.0 + jnp.exp(-gpart))) * upart
        ys_ref[...] = jnp.dot(h.astype(BF16), wd_b[...], preferred_element_type=F32)

    @pl.when(jnp.logical_not(live))
    def _():
        ys_ref[...] = jnp.zeros_like(ys_ref)


def _experts(xs, w_gu, w_down, layer, tile_expert, next_expert, n_active, tm):
    p, d = xs.shape
    de = w_down.shape[2]
    live = lambda j, na: jnp.maximum(jnp.minimum(j, na[0] - 1), 0)
    grid_spec = pltpu.PrefetchScalarGridSpec(
        num_scalar_prefetch=3,
        grid=(p // tm,),
        in_specs=[pl.BlockSpec((tm, d), lambda j, te, nx, na: (live(j, na), 0)),
                  pl.BlockSpec(memory_space=pl.ANY), pl.BlockSpec(memory_space=pl.ANY)],
        out_specs=pl.BlockSpec((tm, d), lambda j, te, nx, na: (j, 0)),
        scratch_shapes=[pltpu.VMEM((d, 2 * de), F32), pltpu.VMEM((de, d), F32), pltpu.VMEM((d, 2 * de), BF16),
                        pltpu.VMEM((de, d), BF16), pltpu.SemaphoreType.DMA((2,))],
    )
    return pl.pallas_call(
        functools.partial(_experts_kernel, layer=layer),
        grid_spec=grid_spec,
        out_shape=jax.ShapeDtypeStruct((p, d), F32),
        compiler_params=_cparams(("arbitrary",)),
        name="moe_experts",
    )(tile_expert, next_expert, n_active, xs, w_gu, w_down)


def _combine_kernel(pos_ref, pos_next_ref, x_ref, gw_ref, ys_ref, g_ref, b_ref, o_ref, buf, sems, *, alpha):
    i = pl.program_id(0)
    slot = i % 2
    tm = x_ref.shape[0]
    n_slots = pos_ref.shape[0]

    def rows(p_ref, sl, fn):
        for r in range(tm):
            for s in range(n_slots):
                fn(_row_copy(ys_ref, p_ref[s, r], buf.at[sl, s], r, sems.at[sl]), s)

    start = lambda cp, s: cp.start(priority=s % DMA_QUEUES)

    @pl.when(i == 0)
    def _():
        rows(pos_ref, 0, start)

    @pl.when(i + 1 < pl.num_programs(0))
    def _():
        rows(pos_next_ref, 1 - slot, start)

    rows(pos_ref, slot, lambda cp, s: cp.wait())
    gw = gw_ref[...]
    y = alpha * x_ref[...]
    for s in range(n_slots):
        y = y + gw[:, s:s + 1] * buf[slot, s]
    o_ref[...] = _layer_norm(y, g_ref[...], b_ref[...], LN_EPS)


def _combine_ln(x, ys, pos, gw, g, b, alpha, tm=256):
    n, d = x.shape
    n_slots = pos.shape[0]
    nt = n // tm
    return pl.pallas_call(
        functools.partial(_combine_kernel, alpha=alpha),
        grid=(nt,),
        in_specs=[pl.BlockSpec((n_slots, tm), lambda i: (0, i), memory_space=pltpu.SMEM),
                  pl.BlockSpec((n_slots, tm), lambda i: (0, jnp.minimum(i + 1, nt - 1)), memory_space=pltpu.SMEM),
                  pl.BlockSpec((tm, d), lambda i: (i, 0)), pl.BlockSpec((tm, n_slots), lambda i: (i, 0)),
                  pl.BlockSpec(memory_space=pl.ANY), pl.BlockSpec((1, d), lambda i: (0, 0)),
                  pl.BlockSpec((1, d), lambda i: (0, 0))],
        out_specs=pl.BlockSpec((tm, d), lambda i: (i, 0)),
        out_shape=jax.ShapeDtypeStruct((n, d), F32),
        scratch_shapes=[pltpu.VMEM((2, n_slots, tm, d), F32), pltpu.SemaphoreType.DMA((2,))],
        compiler_params=_cparams(("arbitrary",)),
        name="moe_combine",
    )(pos, pos, x, gw, ys, g.reshape(1, d), b.reshape(1, d))


def _moe_ln(x, meta, counts, w_gu, w_down, layer, g, b, alpha, tm=256):
    n, d = x.shape
    n_experts = w_down.shape[1]
    eid = meta[0:2].astype(jnp.int32)
    gw = meta[2:4].T
    rank = meta[4:6].astype(jnp.int32)
    counts = counts[:, 0].astype(jnp.int32)
    tiles = (counts + tm - 1) // tm
    tile_end = jnp.cumsum(tiles)
    row_off = (tile_end - tiles) * tm
    n_tiles = (eid.size + tm - 1) // tm + n_experts
    tile_expert = jnp.sum(jnp.arange(n_tiles)[:, None] >= tile_end[None, :], axis=1).astype(jnp.int32)
    tile_expert = jnp.minimum(tile_expert, n_experts - 1)
    n_active = tile_end[-1:].astype(jnp.int32)
    ids = jnp.arange(n_experts)
    later = (ids[None, :] > ids[:, None]) & (tiles[None, :] > 0)
    next_of = jnp.min(jnp.where(later, ids[None, :], n_experts), axis=1)
    next_of = jnp.where(next_of < n_experts, next_of, -1).astype(jnp.int32)
    lookup = lambda table, ids: jnp.sum(jnp.where(ids[..., None] == jnp.arange(n_experts), table, 0), axis=-1)
    next_expert = lookup(next_of, tile_expert)
    pos = lookup(row_off, eid) + rank
    last_tile = jnp.maximum(tile_end - 1, 0)
    idle_tile = jnp.minimum(tile_end[-1] + jnp.arange(n_experts), n_tiles - 1)
    zero_ids = jnp.concatenate([last_tile, idle_tile]).astype(jnp.int32)
    xs = _dispatch(x, pos, zero_ids, n_tiles * tm, tm)
    ys = _experts(xs, w_gu, w_down, layer, tile_expert, next_expert, n_active, tm)
    return _combine_ln(x, ys, pos, gw, g, b, alpha)


def _token_shift(x_ref, prev_ref, first):
    x = x_ref[...]
    prev_row = jnp.where(first, 0.0, prev_ref[7:8, :])
    row = lax.broadcasted_iota(jnp.int32, x.shape, 0)
    return x, jnp.where(row == 0, prev_row, pltpu.roll(x, 1, 0))


def _rkv_kernel(x_ref, prev_ref, mu_ref, w_ref, o_ref, *, tiles_per_seq):
    first = pl.program_id(0) % tiles_per_seq == 0
    x, xp = _token_shift(x_ref, prev_ref, first)
    xm = x + (xp - x) * mu_ref[0]
    o_ref[0] = jnp.dot(xm.astype(BF16), w_ref[0], preferred_element_type=F32).astype(o_ref.dtype)


def _rkv_proj(x, mu3, w_rkv, t, tm=1024):
    n, d = x.shape
    sub = tm // 8
    return pl.pallas_call(
        functools.partial(_rkv_kernel, tiles_per_seq=t // tm),
        grid=(n // tm, 3),
        in_specs=[pl.BlockSpec((tm, d), lambda i, j: (i, 0)),
                  pl.BlockSpec((8, d), lambda i, j: (jnp.maximum(i * sub - 1, 0), 0)),
                  pl.BlockSpec((1, 1, d), lambda i, j: (j, 0, 0)), pl.BlockSpec((1, d, d), lambda i, j: (j, 0, 0))],
        out_specs=pl.BlockSpec((1, tm, d), lambda i, j: (j, i, 0)),
        out_shape=jax.ShapeDtypeStruct((3, n, d), BF16),
        compiler_params=_cparams(("parallel", "arbitrary")),
        name="rkv_proj",
    )(x, x, mu3.reshape(3, 1, d), w_rkv)


def _lora_kernel(x_ref, prev_ref, mu_ref, w0_ref, w1_ref, w2_ref, a0_ref, a1_ref, a2_ref, g1_ref, g2_ref, wl_ref,
                 a_ref, g_ref, *, tiles_per_seq):
    first = pl.program_id(0) % tiles_per_seq == 0
    x, xp = _token_shift(x_ref, prev_ref, first)
    dx = xp - x

    def mm(a, w_ref):
        return jnp.dot(a.astype(BF16), w_ref[...], preferred_element_type=F32)

    zw = w0_ref[...] + mm(jnp.tanh(mm(x + dx * mu_ref[0:1, :], w1_ref)), w2_ref)
    wl_ref[...] = -HALF_DECAY * _sigmoid(zw)
    za = a0_ref[...] + mm(mm(x + dx * mu_ref[1:2, :], a1_ref), a2_ref)
    a_ref[...] = _sigmoid(za).astype(a_ref.dtype)
    zg = mm(x + dx * mu_ref[2:3, :], g1_ref)
    g_ref[...] = mm(_sigmoid(zg), g2_ref).astype(g_ref.dtype)


def _lora(x, mu3, w0, w1, w2, a0, a1, a2, g1, g2, t, tm=512):
    n, d = x.shape
    sub = tm // 8
    const = lambda shape: pl.BlockSpec(shape, lambda i: (0, 0))
    row = pl.BlockSpec((tm, d), lambda i: (i, 0))
    return pl.pallas_call(
        functools.partial(_lora_kernel, tiles_per_seq=t // tm),
        grid=(n // tm,),
        in_specs=[row, pl.BlockSpec((8, d), lambda i: (jnp.maximum(i * sub - 1, 0), 0)), const((3, d)),
                  const((1, d)), const(w1.shape), const(w2.shape), const((1, d)), const(a1.shape), const(a2.shape),
                  const(g1.shape), const(g2.shape)],
        out_specs=[row, row, row],
        out_shape=[jax.ShapeDtypeStruct((n, d), F32), jax.ShapeDtypeStruct((n, d), BF16),
                   jax.ShapeDtypeStruct((n, d), BF16)],
        compiler_params=_cparams(("parallel",)),
        name="lora",
    )(x, x, mu3, w0.reshape(1, d), w1, w2, a0.reshape(1, d), a1, a2, g1, g2)


def _scan_groups(r, k, v, lw, a_gate, g, kk_w, ka_w, rk_w, gg, gb, s0, *, hd, gn_eps):
    c = r[0].shape[0]
    nh = LANES // hd
    hc = nh * c
    each = lambda f, *xs: [f(*x) for x in zip(*xs)]
    lane = lax.broadcasted_iota(jnp.int32, (1, LANES), 1)
    head_masks = [(lane >= h * hd) & (lane < (h + 1) * hd) for h in range(nh)]
    lr = lax.broadcasted_iota(jnp.int32, (LANES, LANES), 0) // hd
    lc = lax.broadcasted_iota(jnp.int32, (LANES, LANES), 1) // hd
    same_head = jnp.where(lr == lc, 1.0, 0.0).astype(BF16)
    trow = lax.broadcasted_iota(jnp.int32, (c, c), 0)
    tcol = lax.broadcasted_iota(jnp.int32, (c, c), 1)
    lower = jnp.where(trow >= tcol, 1.0, 0.0).astype(BF16)
    prow = lax.broadcasted_iota(jnp.int32, (hc, hc), 0)
    pcol = lax.broadcasted_iota(jnp.int32, (hc, hc), 1)
    same_blk = (prow // c) == (pcol // c)
    strict = same_blk & (prow > pcol)
    incl = same_blk & (prow >= pcol)
    eye = jnp.where(prow == pcol, 1.0, 0.0).astype(F32)

    def split(x):
        hi = x.astype(BF16)
        return hi, (x - hi.astype(F32)).astype(BF16)

    def head_sum(xs):
        return [jnp.dot(x.astype(BF16), same_head, preferred_element_type=F32) for x in xs]

    def per_head(x):
        return jnp.concatenate([jnp.where(hm, x, 0.0) for hm in head_masks], axis=0).astype(BF16)

    kk = each(lambda k_, w_: k_ * w_, k, kk_w)
    kk_n = head_sum(each(lambda x: x * x, kk))
    kk = each(lambda x, n_: x / jnp.maximum(jnp.sqrt(n_), 1e-12), kk, kk_n)
    k2 = each(lambda k_, a_, w_: k_ * (1.0 + (a_ - 1.0) * w_), k, a_gate, ka_w)
    lw_parts = each(split, lw)
    cum = [jnp.dot(lower, hi, preferred_element_type=F32) + jnp.dot(lower, lo, preferred_element_type=F32)
           for hi, lo in lw_parts]
    gam = each(jnp.exp, cum)
    inv_gam = each(lambda x: jnp.exp(-x), cum)
    gam_end = each(lambda x: x[c - 1:c, :], gam)
    a_t = each(lambda kk_, cum_, lw_: -kk_ * jnp.exp(cum_ - lw_), kk, cum, lw)
    b_t = each(lambda kk_, a_, ig: kk_ * a_ * ig, kk, a_gate, inv_gam)
    k_t = each(lambda k2_, ig: k2_ * ig, k2, inv_gam)
    r_t = each(lambda r_, gm: r_ * gm, r, gam)

    s0b = each(lambda x: x.astype(BF16), s0)
    ar_s0 = each(lambda a_, r_, s_: _nt(jnp.concatenate([a_, r_], axis=0).astype(BF16), s_), a_t, r_t, s0b)
    ar2 = each(lambda a_, r_: jnp.concatenate([per_head(a_), per_head(r_)], axis=0), a_t, r_t)
    bk2 = each(lambda b_, k_: jnp.concatenate([per_head(b_), per_head(k_)], axis=0), b_t, k_t)
    v2 = each(per_head, v)
    gmat = each(_nt, ar2, bk2)
    a_ab = each(lambda x: jnp.where(strict, x[:hc, :hc], 0.0), gmat)
    a_ak = each(lambda x: jnp.where(strict, x[:hc, hc:], 0.0), gmat)
    a_r = each(lambda x: jnp.concatenate([jnp.where(incl, x[hc:, :hc], 0.0), jnp.where(incl, x[hc:, hc:], 0.0)],
                                         axis=1).astype(BF16), gmat)
    rhs = each(lambda as0, ak, v_: jnp.concatenate([jnp.where(hm, as0[:c], 0.0) for hm in head_masks], axis=0)
               + _bdot(ak, v_), ar_s0, a_ak, v2)
    inv = each(lambda x: eye + x, a_ab)
    m = each(lambda x: _bdot(x, x), a_ab)
    n_pow = 2
    while 2 * n_pow < c:
        mp = each(lambda m_, p_: _bdot(jnp.concatenate([m_, p_], axis=0), m_), m, inv)
        inv = each(lambda p_, mp_: p_ + mp_[hc:], inv, mp)
        m = each(lambda mp_: mp_[:hc], mp)
        n_pow *= 2
    rhs = each(lambda rhs_, m_: rhs_ + _bdot(m_, rhs_), rhs, m)
    u2 = each(_bdot, inv, rhs)
    uv = each(lambda u_, v_: jnp.concatenate([u_.astype(BF16), v_], axis=0), u2, v2)
    y2 = each(lambda ar_, uv_: jnp.dot(ar_, uv_, preferred_element_type=F32), a_r, uv)
    y = each(lambda rs, y2_: rs[c:] + sum(y2_[h * c:(h + 1) * c, :] for h in range(nh)), ar_s0, y2)
    bkg = each(lambda b_, k_, ge: jnp.concatenate([per_head(b_ * ge), per_head(k_ * ge)], axis=0), b_t, k_t, gam_end)
    s_new = each(lambda s_, ge, uv_, bkg_: s_ * ge + _tn(uv_, bkg_), s0, gam_end, uv, bkg)

    inv_hd = 1.0 / hd
    mean = head_sum(y)
    yc = each(lambda y_, m_: y_ - m_ * inv_hd, y, mean)
    var = head_sum(each(lambda x: x * x, yc))
    yn = each(lambda yc_, var_, gg_, gb_: yc_ * lax.rsqrt(var_ * inv_hd + gn_eps) * gg_ + gb_, yc, var, gg, gb)
    rk_sum = head_sum(each(lambda r_, k2_, w_: r_ * k2_ * w_, r, k2, rk_w))
    out = each(lambda yn_, rk_, v_, g_: (yn_ + rk_ * v_) * g_, yn, rk_sum, v, g)
    return out, s_new


def _scan_kernel(r_ref, k_ref, v_ref, wl_ref, a_ref, g_ref, kk_ref, ka_ref, rk_ref, gg_ref, gb_ref, o_ref, state, *,
                 hd, gn_eps):
    @pl.when(pl.program_id(2) == 0)
    def _():
        state[...] = jnp.zeros_like(state)

    ng = state.shape[0]
    sls = [slice(p * LANES, (p + 1) * LANES) for p in range(ng)]
    tok3 = lambda ref: [ref[0, :, sl].astype(F32) for sl in sls]
    tok2 = lambda ref: [ref[:, sl].astype(F32) for sl in sls]
    out, s_new = _scan_groups(tok3(r_ref), tok3(k_ref), tok3(v_ref), tok2(wl_ref), tok2(a_ref), tok2(g_ref),
                              tok2(kk_ref), tok2(ka_ref), tok2(rk_ref), tok2(gg_ref), tok2(gb_ref),
                              [state[p] for p in range(ng)], hd=hd, gn_eps=gn_eps)
    for p in range(ng):
        state[p] = s_new[p]
        o_ref[:, sls[p]] = out[p].astype(o_ref.dtype)


def _rwkv_scan(rkv, wl, a, g, k_k, k_a, r_k, gn_g, gn_b, batch, hd, gn_eps, chunk=64, groups=16):
    _, n, d = rkv.shape
    t = n // batch
    nc = t // chunk
    groups = min(groups, d // LANES)
    w = groups * LANES
    tok = lambda j: pl.BlockSpec((1, chunk, w), functools.partial(lambda b, p, c, j: (j, b * nc + c, p), j=j))
    tok2 = pl.BlockSpec((chunk, w), lambda b, p, c: (b * nc + c, p))
    par = pl.BlockSpec((1, w), lambda b, p, c: (0, p))
    return pl.pallas_call(
        functools.partial(_scan_kernel, hd=hd, gn_eps=gn_eps),
        grid=(batch, d // w, nc),
        in_specs=[tok(0), tok(1), tok(2), tok2, tok2, tok2, par, par, par, par, par],
        out_specs=tok2,
        out_shape=jax.ShapeDtypeStruct((n, d), BF16),
        scratch_shapes=[pltpu.VMEM((groups, LANES, LANES), F32)],
        compiler_params=_cparams(("parallel", "parallel", "arbitrary")),
        name="rwkv_scan",
    )(rkv, rkv, rkv, wl, a, g, k_k.reshape(1, d), k_a.reshape(1, d), r_k.reshape(1, d), gn_g.reshape(1, d),
      gn_b.reshape(1, d))


def _pad_lora(w_in, w_out):
    r = w_in.shape[1]
    rp = -(-r // LANES) * LANES
    return (jnp.pad(w_in, ((0, 0), (0, rp - r))).astype(BF16), jnp.pad(w_out, ((0, rp - r), (0, 0))).astype(BF16))


def kernel(x, ev_w_in, ev_b_a, ev_w_s, ev_b_s, ev_g_v, ev_b_v, ev_b_f, ev_w_out, rw_mu, rw_w_rkv, rw_w0, rw_w1, rw_w2, rw_a0, rw_a1, rw_a2, rw_g1, rw_g2, rw_k_k, rw_k_a, rw_r_k, rw_gn_g, rw_gn_b, rw_w_o, ln_g, ln_b, w_router, b_router, w_gu, w_down):
    batch, t, d = x.shape
    depth = ln_g.shape[0]
    alpha = (2 * depth) ** 0.25
    h = x.reshape(batch * t, d)
    for layer in range(depth):
        i = layer // 2
        if layer % 2 == 0:
            aw = ev_g_v.shape[1]
            heads = ev_b_f.shape[1]
            q_col = 2 * aw
            bw = heads * LANES
            k_col, v_col, f_col = q_col + bw, q_col + 2 * bw, q_col + 3 * bw
            w_in = ev_w_in[i]
            col = jnp.arange(v_col)
            q_scale = jnp.where((col >= q_col) & (col < k_col), LANES ** -0.5 * LOG2E, 1.0)
            proj = _matmul(h, (w_in[:, :v_col] * q_scale).astype(BF16), v_col, BF16)
            vt, f_logit = _proj_transposed(h, w_in[:, v_col:f_col].T.astype(BF16), w_in[:, f_col:].T, FOX_BLOCK)
            c = _fgate(f_logit, ev_b_f[i], batch)
            y_a = _sgu(proj, ev_b_a[i], ev_w_s[i], ev_b_s[i], ev_g_v[i], ev_b_v[i])
            y_b = _fox_attention(proj, vt, c, batch, heads, q_col, FOX_BLOCK)
            h, meta, counts = _proj_ln_route([y_a, y_b], ev_w_out[i].astype(BF16), h, ln_g[layer, 0],
                                             ln_b[layer, 0], alpha, w_router, b_router)
        else:
            hd = rw_r_k.shape[2]
            mu = rw_mu[i]
            rkv = _rkv_proj(h, mu[:3], rw_w_rkv[i].astype(BF16), t)
            w1, w2 = _pad_lora(rw_w1[i], rw_w2[i])
            a1, a2 = _pad_lora(rw_a1[i], rw_a2[i])
            g1, g2 = _pad_lora(rw_g1[i], rw_g2[i])
            wl, a, g = _lora(h, mu[3:], rw_w0[i], w1, w2, rw_a0[i], a1, a2, g1, g2, t)
            y = _rwkv_scan(rkv, wl, a, g, rw_k_k[i], rw_k_a[i], rw_r_k[i].reshape(-1), rw_gn_g[i], rw_gn_b[i],
                           batch, hd, hd * 1e-5)
            h, meta, counts = _proj_ln_route([y], rw_w_o[i].astype(BF16), h, ln_g[layer, 0], ln_b[layer, 0], alpha,
                                             w_router, b_router)
        h = _moe_ln(h, meta, counts, w_gu, w_down, layer, ln_g[layer, 1], ln_b[layer, 1], alpha)
    return h.reshape(batch, t, d)
```

```python
import functools

import jax
import jax.numpy as jnp
from jax import lax
from jax.experimental import pallas as pl
from jax.experimental.pallas import tpu as pltpu

F32 = jnp.float32
BF16 = jnp.bfloat16
HI = lax.Precision.HIGHEST

LN_EPS = 1e-5
N_GROUPS = 4
LANES = 128
FOX_BLOCK = 512
LOG2E = 1.4426950408889634
HALF_DECAY = 0.6065306597126334
DMA_QUEUES = 2
VMEM_LIMIT = 56 * 1024 * 1024


def _cparams(sem):
    return pltpu.CompilerParams(dimension_semantics=sem, vmem_limit_bytes=VMEM_LIMIT)


def _layer_norm(x, g, b, eps):
    mu = jnp.mean(x, -1, keepdims=True)
    xc = x - mu
    var = jnp.mean(xc * xc, -1, keepdims=True)
    return xc * lax.rsqrt(var + eps) * g + b


def _sigmoid(x):
    return 0.5 + 0.5 * jnp.tanh(0.5 * x)


def _nt(a, b, **kw):
    return lax.dot_general(a, b, (((1,), (1,)), ((), ())), preferred_element_type=F32, **kw)


def _tn(a, b, **kw):
    return lax.dot_general(a, b, (((0,), (0,)), ((), ())), preferred_element_type=F32, **kw)


def _bdot(a, b):
    return jnp.dot(a.astype(BF16), b.astype(BF16), preferred_element_type=F32)


def _mm_kernel(a_ref, w_ref, o_ref, a_bf16):
    @pl.when(pl.program_id(1) == 0)
    def _():
        a_bf16[...] = a_ref[...].astype(BF16)

    o_ref[...] = jnp.dot(a_bf16[...], w_ref[...], preferred_element_type=F32).astype(o_ref.dtype)


def _matmul(a, w, n_cols, out_dtype, tm=1024, tn=1024):
    m, k = a.shape
    tm = min(tm, m)
    tn = next(c for c in (tn, 256, LANES) if n_cols % c == 0)
    return pl.pallas_call(
        _mm_kernel,
        grid=(m // tm, n_cols // tn),
        in_specs=[pl.BlockSpec((tm, k), lambda i, j: (i, 0)), pl.BlockSpec((k, tn), lambda i, j: (0, j))],
        out_specs=pl.BlockSpec((tm, tn), lambda i, j: (i, j)),
        out_shape=jax.ShapeDtypeStruct((m, n_cols), out_dtype),
        scratch_shapes=[pltpu.VMEM((tm, k), BF16)],
        compiler_params=_cparams(("parallel", "arbitrary")),
        name="matmul",
    )(a, w)


def _proj_ln_kernel(*refs, n_in, alpha):
    a_refs, w_refs = refs[:n_in], refs[n_in:2 * n_in]
    res_ref, g_ref, b_ref, wr_ref, br_ref, o_ref, meta_ref, cnt_ref, carry = refs[2 * n_in:]

    @pl.when(pl.program_id(0) == 0)
    def _():
        carry[...] = jnp.zeros_like(carry)

    acc = alpha * res_ref[...]
    for a_ref, w_ref in zip(a_refs, w_refs):
        acc = acc + jnp.dot(a_ref[...], w_ref[...], preferred_element_type=F32)
    y = _layer_norm(acc, g_ref[...], b_ref[...], LN_EPS)
    o_ref[...] = y
    meta_ref[...] = _route_tile(y, wr_ref, br_ref, carry)
    cnt_ref[...] = jnp.broadcast_to(carry[...], cnt_ref.shape)


def _proj_ln_route(a_list, w, res, g, b, alpha, w_router, b_router, tm=512):
    m, d = res.shape
    n_in = len(a_list)
    kc = a_list[0].shape[1]
    n_experts = w_router.shape[1]
    wr = w_router.T
    wr_hi = wr.astype(BF16)
    wr = jnp.stack([wr_hi, (wr - wr_hi.astype(F32)).astype(BF16)])
    in_specs = [pl.BlockSpec((tm, kc), lambda i: (i, 0)) for _ in a_list]
    in_specs += [pl.BlockSpec((kc, d), functools.partial(lambda i, r: (r, 0), r=r)) for r in range(n_in)]
    in_specs += [pl.BlockSpec((tm, d), lambda i: (i, 0)), pl.BlockSpec((1, d), lambda i: (0, 0)),
                 pl.BlockSpec((1, d), lambda i: (0, 0)), pl.BlockSpec((2, n_experts, d), lambda i: (0, 0, 0)),
                 pl.BlockSpec((n_experts, 1), lambda i: (0, 0))]
    return pl.pallas_call(
        functools.partial(_proj_ln_kernel, n_in=n_in, alpha=alpha),
        grid=(m // tm,),
        in_specs=in_specs,
        out_specs=[pl.BlockSpec((tm, d), lambda i: (i, 0)), pl.BlockSpec((8, tm), lambda i: (0, i)),
                   pl.BlockSpec((n_experts, LANES), lambda i: (0, 0))],
        out_shape=[jax.ShapeDtypeStruct((m, d), F32), jax.ShapeDtypeStruct((8, m), F32),
                   jax.ShapeDtypeStruct((n_experts, LANES), F32)],
        scratch_shapes=[pltpu.VMEM((n_experts, 1), F32)],
        compiler_params=_cparams(("arbitrary",)),
        name="proj_ln_route",
    )(*a_list, *([w] * n_in), res, g.reshape(1, d), b.reshape(1, d), wr, b_router.reshape(n_experts, 1))


def _fgate_kernel(z_ref, bf_ref, c_ref, carry):
    @pl.when(pl.program_id(1) == 0)
    def _():
        carry[...] = jnp.zeros_like(carry)

    tm = z_ref.shape[2]
    z = z_ref[0] + bf_ref[...]
    log_f = jnp.minimum(z, 0.0) - jnp.log1p(jnp.exp(-jnp.abs(z)))
    row = lax.broadcasted_iota(jnp.int32, (tm, tm), 0)
    col = lax.broadcasted_iota(jnp.int32, (tm, tm), 1)
    upper = jnp.where(row <= col, 1.0, 0.0).astype(F32)
    c = jnp.dot(log_f, upper, preferred_element_type=F32, precision=HI) + carry[...]
    c_ref[0] = c
    carry[...] = carry[...] + jnp.sum(log_f, axis=-1, keepdims=True)


def _fgate(f_logit, b_f, batch):
    tiles, h, tm = f_logit.shape
    nt = tiles // batch
    return pl.pallas_call(
        _fgate_kernel,
        grid=(batch, nt),
        in_specs=[pl.BlockSpec((1, h, tm), lambda b, i: (b * nt + i, 0, 0)), pl.BlockSpec((h, 1), lambda b, i: (0, 0))],
        out_specs=pl.BlockSpec((1, h, tm), lambda b, i: (b, 0, i)),
        out_shape=jax.ShapeDtypeStruct((batch, h, nt * tm), F32),
        scratch_shapes=[pltpu.VMEM((h, 1), F32)],
        compiler_params=_cparams(("parallel", "arbitrary")),
        name="fgate",
    )(f_logit, b_f.reshape(h, 1))


def _gelu_tanh(x):
    return 0.5 * x * (1.0 + jnp.tanh(0.7978845608028654 * (x + 0.044715 * (x * x * x))))


def _sgu_kernel(z_ref, ba_ref, ws_ref, bs_ref, gv_ref, bv_ref, o_ref, *, chunk, groups):
    aw = o_ref.shape[1]
    gd = aw // groups
    z = _gelu_tanh(z_ref[...].astype(F32) + ba_ref[...])
    u = z[:, :aw]
    v = _layer_norm(z[:, aw:], gv_ref[...], bv_ref[...], LN_EPS).astype(BF16)
    row = lax.broadcasted_iota(jnp.int32, (chunk, chunk), 0)
    col = lax.broadcasted_iota(jnp.int32, (chunk, chunk), 1)
    causal = row >= col
    bs = bs_ref[...]
    for g in range(groups):
        w_g = jnp.where(causal, ws_ref[g], 0.0).astype(BF16)
        for c in range(z.shape[0] // chunk):
            rs = slice(c * chunk, (c + 1) * chunk)
            cs = slice(g * gd, (g + 1) * gd)
            s = jnp.dot(w_g, v[rs, cs], preferred_element_type=F32) + bs[:, g:g + 1]
            o_ref[rs, cs] = (u[rs, cs] * s).astype(o_ref.dtype)


def _sgu(proj, b_a, w_s, b_s, g_v, b_v, tm=512):
    n = proj.shape[0]
    groups, chunk, _ = w_s.shape
    aw = g_v.shape[0]
    return pl.pallas_call(
        functools.partial(_sgu_kernel, chunk=chunk, groups=groups),
        grid=(n // tm,),
        in_specs=[pl.BlockSpec((tm, 2 * aw), lambda i: (i, 0)), pl.BlockSpec((1, 2 * aw), lambda i: (0, 0)),
                  pl.BlockSpec((groups, chunk, chunk), lambda i: (0, 0, 0)),
                  pl.BlockSpec((chunk, groups), lambda i: (0, 0)), pl.BlockSpec((1, aw), lambda i: (0, 0)),
                  pl.BlockSpec((1, aw), lambda i: (0, 0))],
        out_specs=pl.BlockSpec((tm, aw), lambda i: (i, 0)),
        out_shape=jax.ShapeDtypeStruct((n, aw), BF16),
        compiler_params=_cparams(("parallel",)),
        name="sgu",
    )(proj, b_a.reshape(1, -1), w_s, b_s.T, g_v.reshape(1, aw), b_v.reshape(1, aw))


def _vt_kernel(x_ref, w_ref, wf_ref, o_ref, f_ref):
    x = x_ref[...]
    x_hi = x.astype(BF16)
    x_lo = (x - x_hi.astype(F32)).astype(BF16)
    o_ref[0] = _nt(w_ref[...], x_hi).astype(o_ref.dtype)
    f_ref[0] = _nt(wf_ref[0], x_hi) + _nt(wf_ref[0], x_lo) + _nt(wf_ref[1], x_hi)


def _proj_transposed(x, w_t, wf_t, blk):
    n, d = x.shape
    rows, rows_f = w_t.shape[0], wf_t.shape[0]
    wf_hi = wf_t.astype(BF16)
    wf = jnp.stack([wf_hi, (wf_t - wf_hi.astype(F32)).astype(BF16)])
    return pl.pallas_call(
        _vt_kernel,
        grid=(n // blk,),
        in_specs=[pl.BlockSpec((blk, d), lambda i: (i, 0)), pl.BlockSpec((rows, d), lambda i: (0, 0)),
                  pl.BlockSpec((2, rows_f, d), lambda i: (0, 0, 0))],
        out_specs=[pl.BlockSpec((1, rows, blk), lambda i: (i, 0, 0)),
                   pl.BlockSpec((1, rows_f, blk), lambda i: (i, 0, 0))],
        out_shape=[jax.ShapeDtypeStruct((n // blk, rows, blk), BF16),
                   jax.ShapeDtypeStruct((n // blk, rows_f, blk), F32)],
        compiler_params=_cparams(("parallel",)),
        name="proj_transposed",
    )(x, w_t, wf)


def _fox_kernel(q_ref, k_ref, vt_ref, cq_ref, ck_ref, o_ref, m_scr, l_scr, acc_scr, *, blk, nh):
    qi = pl.program_id(2)
    heads = range(nh)
    hs = [slice(h * LANES, (h + 1) * LANES) for h in heads]
    m_scr[...] = jnp.full_like(m_scr, -jnp.inf)
    l_scr[...] = jnp.zeros_like(l_scr)
    acc_scr[...] = jnp.zeros_like(acc_scr)
    q = [q_ref[:, hs[h]] for h in heads]
    cq = [cq_ref[0, h, pl.ds(qi, 1), :] * LOG2E for h in heads]

    def step(ki, masked):
        ks = pl.multiple_of(ki * blk, blk)
        t = [_nt(k_ref[pl.ds(ks, blk), hs[h]], q[h]) - ck_ref[0, h, pl.ds(ks, blk), :] * LOG2E for h in heads]
        if masked:
            row = lax.broadcasted_iota(jnp.int32, (blk, blk), 0)
            col = lax.broadcasted_iota(jnp.int32, (blk, blk), 1)
            t = [jnp.where(row <= col, x, -jnp.inf) for x in t]
        m_prev = [m_scr[h] for h in heads]
        m_new = [jnp.maximum(m_prev[h], cq[h] + jnp.max(t[h], axis=0, keepdims=True)) for h in heads]
        p = [jnp.exp2(t[h] - (m_new[h] - cq[h])) for h in heads]
        corr = [jnp.exp2(m_prev[h] - m_new[h]) for h in heads]
        for h in heads:
            l_scr[h] = corr[h] * l_scr[h] + jnp.sum(p[h], axis=0, keepdims=True)
            acc_scr[h] = corr[h] * acc_scr[h] + jnp.dot(vt_ref[ki, hs[h], :], p[h].astype(BF16),
                                                         preferred_element_type=F32)
            m_scr[h] = m_new[h]

    def body(ki, carry):
        step(ki, False)
        return carry

    lax.fori_loop(0, qi, body, 0)
    step(qi, True)
    for h in heads:
        o_ref[:, hs[h]] = jnp.transpose(acc_scr[h] / l_scr[h]).astype(o_ref.dtype)


def _fox_attention(proj, vt, c, batch, heads, q_col, blk, nh=4):
    n = proj.shape[0]
    t = n // batch
    nb = t // blk
    dh = LANES
    nh = min(nh, heads)
    w = nh * dh
    q0, k0 = q_col // w, (q_col + heads * dh) // w
    c_col = c.reshape(batch, heads, t, 1)
    c_row = c.reshape(batch, heads, nb, blk)
    return pl.pallas_call(
        functools.partial(_fox_kernel, blk=blk, nh=nh),
        grid=(batch, heads // nh, nb),
        in_specs=[pl.BlockSpec((blk, w), lambda b, h, i: (b * nb + i, q0 + h)),
                  pl.BlockSpec((t, w), lambda b, h, i: (b, k0 + h)),
                  pl.BlockSpec((nb, w, blk), lambda b, h, i: (b, h, 0)),
                  pl.BlockSpec((1, nh, nb, blk), lambda b, h, i: (b, h, 0, 0)),
                  pl.BlockSpec((1, nh, t, 1), lambda b, h, i: (b, h, 0, 0))],
        out_specs=pl.BlockSpec((blk, w), lambda b, h, i: (b * nb + i, h)),
        out_shape=jax.ShapeDtypeStruct((n, heads * dh), BF16),
        scratch_shapes=[pltpu.VMEM((nh, 1, blk), F32), pltpu.VMEM((nh, 1, blk), F32),
                        pltpu.VMEM((nh, dh, blk), F32)],
        compiler_params=_cparams(("parallel", "parallel", "arbitrary")),
        name="fox_attention",
    )(proj, proj, vt, c_row, c_col)


def _first_max(p, idx, valid):
    pm = jnp.where(valid, p, -2.0)
    m = jnp.max(pm, axis=0, keepdims=True)
    first = jnp.min(jnp.where(pm == m, idx, float(p.shape[0])), axis=0, keepdims=True)
    return m, first


def _route_tile(x, wr_ref, br_ref, carry):
    tm = x.shape[0]
    n_experts = br_ref.shape[0]
    per = n_experts // N_GROUPS
    x_hi = x.astype(BF16)
    x_lo = (x - x_hi.astype(F32)).astype(BF16)
    logits = _nt(wr_ref[0], x_hi) + _nt(wr_ref[0], x_lo) + _nt(wr_ref[1], x_hi) + br_ref[...]
    e = jnp.exp(logits - jnp.max(logits, axis=0, keepdims=True))
    probs = e / jnp.sum(e, axis=0, keepdims=True)
    idx = lax.broadcasted_iota(jnp.int32, (n_experts, tm), 0).astype(F32)
    best_score = jnp.full((1, tm), -1.0, F32)
    best_group = jnp.zeros((1, tm), F32)
    for grp in range(N_GROUPS):
        in_g = (idx >= grp * per) & (idx < (grp + 1) * per)
        m1, i1 = _first_max(probs, idx, in_g)
        m2, _ = _first_max(probs, idx, in_g & (idx != i1))
        score = m1 + m2
        take = score > best_score
        best_score = jnp.where(take, score, best_score)
        best_group = jnp.where(take, float(grp), best_group)
    in_sel = (idx >= best_group * per) & (idx < (best_group + 1) * per)
    p1, i1 = _first_max(probs, idx, in_sel)
    p2, i2 = _first_max(probs, idx, in_sel & (idx != i1))
    tot = p1 + p2
    row = lax.broadcasted_iota(jnp.int32, (tm, tm), 0)
    col = lax.broadcasted_iota(jnp.int32, (tm, tm), 1)
    earlier = jnp.where(row < col, 1.0, 0.0).astype(BF16)
    onehot = jnp.where((idx == i1) | (idx == i2), 1.0, 0.0)
    seen = jnp.dot(onehot.astype(BF16), earlier, preferred_element_type=F32) + carry[...]
    r1 = jnp.sum(jnp.where(idx == i1, seen, 0.0), axis=0, keepdims=True)
    r2 = jnp.sum(jnp.where(idx == i2, seen, 0.0), axis=0, keepdims=True)
    zero = jnp.zeros_like(r1)
    carry[...] = carry[...] + jnp.sum(onehot, axis=1, keepdims=True)
    return jnp.concatenate([i1, i2, p1 / tot, p2 / tot, r1, r2, zero, zero], axis=0)


def _row_copy(src_ref, src_row, dst_ref, dst_row, sem):
    return pltpu.make_async_copy(src_ref.at[pl.ds(src_row, 1)], dst_ref.at[pl.ds(dst_row, 1)], sem)


def _dispatch_kernel(zid_ref, pos_ref, x_hbm, xs_ref, zbuf, stage, row_sems, load_sems, zsem, *, tm, tok):
    i = pl.program_id(0)
    nt = pl.num_programs(0)
    n_slots = pos_ref.shape[0]
    n_stage = stage.shape[0]

    def zero_tiles(fn):
        for j in range(zid_ref.shape[0]):
            new_id = zid_ref[j] != zid_ref[max(j - 1, 0)] if j else True

            @pl.when(new_id)
            def _():
                fn(pltpu.make_async_copy(zbuf, xs_ref.at[pl.ds(zid_ref[j] * tm, tm)], zsem))

    def load(t, slot):
        return pltpu.make_async_copy(x_hbm.at[pl.ds(t * tok, tok)], stage.at[slot], load_sems.at[slot])

    def drain(slot):
        for _ in range(tok * n_slots):
            _row_copy(stage.at[slot], 0, xs_ref, 0, row_sems.at[slot]).wait()

    @pl.when(i == 0)
    def _():
        zbuf[...] = jnp.zeros_like(zbuf)
        zero_tiles(lambda cp: cp.start())
        load(0, 0).start()
        zero_tiles(lambda cp: cp.wait())

    @pl.when((i == 0) & (nt > 1))
    def _():
        load(1, 1).start()

    cur = i % n_stage
    prev = (i + n_stage - 1) % n_stage
    load(i, cur).wait()
    for r in range(tok):
        for s in range(n_slots):
            _row_copy(stage.at[cur], r, xs_ref, pos_ref[s, r], row_sems.at[cur]).start(priority=s % DMA_QUEUES)

    @pl.when(i > 0)
    def _():
        drain(prev)

    @pl.when(i + 2 < nt)
    def _():
        load(i + 2, prev).start()

    @pl.when(i == nt - 1)
    def _():
        drain(cur)


def _dispatch(x, pos, zero_ids, n_rows, tm, tok=256):
    n, d = x.shape
    grid_spec = pltpu.PrefetchScalarGridSpec(
        num_scalar_prefetch=1,
        grid=(n // tok,),
        in_specs=[pl.BlockSpec((pos.shape[0], tok), lambda i, z: (0, i), memory_space=pltpu.SMEM),
                  pl.BlockSpec(memory_space=pl.ANY)],
        out_specs=pl.BlockSpec(memory_space=pl.ANY),
        scratch_shapes=[pltpu.VMEM((tm, d), x.dtype), pltpu.VMEM((3, tok, d), x.dtype),
                        pltpu.SemaphoreType.DMA((3,)), pltpu.SemaphoreType.DMA((3,)), pltpu.SemaphoreType.DMA],
    )
    return pl.pallas_call(
        functools.partial(_dispatch_kernel, tm=tm, tok=tok),
        grid_spec=grid_spec,
        out_shape=jax.ShapeDtypeStruct((n_rows, d), x.dtype),
        compiler_params=_cparams(("arbitrary",)),
        name="moe_dispatch",
    )(zero_ids, pos, x)


def _experts_kernel(te_ref, nx_ref, na_ref, xs_ref, wgu_hbm, wd_hbm, ys_ref, wgu_f, wd_f, wgu_b, wd_b, sems, *, layer):
    j = pl.program_id(0)
    live = j < na_ref[0]
    fresh = (j == 0) | (te_ref[j] != te_ref[jnp.maximum(j - 1, 0)])

    def fetch(e):
        return (pltpu.make_async_copy(wgu_hbm.at[layer, e], wgu_f, sems.at[0]),
                pltpu.make_async_copy(wd_hbm.at[layer, e], wd_f, sems.at[1]))

    @pl.when(live & (j == 0))
    def _():
        for cp in fetch(te_ref[0]):
            cp.start()

    @pl.when(live & fresh)
    def _():
        for cp in fetch(te_ref[j]):
            cp.wait()
        wgu_b[...] = wgu_f[...].astype(BF16)
        wd_b[...] = wd_f[...].astype(BF16)

    @pl.when(live & fresh & (nx_ref[j] >= 0))
    def _():
        for cp in fetch(nx_ref[j]):
            cp.start()

    @pl.when(live)
    def _():
        de = wd_b.shape[0]
        gu = jnp.dot(xs_ref[...].astype(BF16), wgu_b[...], preferred_element_type=F32)
        gpart, upart = gu[:, :de], gu[:, de:]
        h = (gpart / (1.0 + jnp.exp(-gpart))) * upart
        ys_ref[...] = jnp.dot(h.astype(BF16), wd_b[...], preferred_element_type=F32)

    @pl.when(jnp.logical_not(live))
    def _():
        ys_ref[...] = jnp.zeros_like(ys_ref)


def _experts(xs, w_gu, w_down, layer, tile_expert, next_expert, n_active, tm):
    p, d = xs.shape
    de = w_down.shape[2]
    live = lambda j, na: jnp.maximum(jnp.minimum(j, na[0] - 1), 0)
    grid_spec = pltpu.PrefetchScalarGridSpec(
        num_scalar_prefetch=3,
        grid=(p // tm,),
        in_specs=[pl.BlockSpec((tm, d), lambda j, te, nx, na: (live(j, na), 0)),
                  pl.BlockSpec(memory_space=pl.ANY), pl.BlockSpec(memory_space=pl.ANY)],
        out_specs=pl.BlockSpec((tm, d), lambda j, te, nx, na: (j, 0)),
        scratch_shapes=[pltpu.VMEM((d, 2 * de), F32), pltpu.VMEM((de, d), F32), pltpu.VMEM((d, 2 * de), BF16),
                        pltpu.VMEM((de, d), BF16), pltpu.SemaphoreType.DMA((2,))],
    )
    return pl.pallas_call(
        functools.partial(_experts_kernel, layer=layer),
        grid_spec=grid_spec,
        out_shape=jax.ShapeDtypeStruct((p, d), F32),
        compiler_params=_cparams(("arbitrary",)),
        name="moe_experts",
    )(tile_expert, next_expert, n_active, xs, w_gu, w_down)


def _combine_kernel(pos_ref, pos_next_ref, x_ref, gw_ref, ys_ref, g_ref, b_ref, o_ref, buf, sems, *, alpha):
    i = pl.program_id(0)
    slot = i % 2
    tm = x_ref.shape[0]
    n_slots = pos_ref.shape[0]

    def rows(p_ref, sl, fn):
        for r in range(tm):
            for s in range(n_slots):
                fn(_row_copy(ys_ref, p_ref[s, r], buf.at[sl, s], r, sems.at[sl]), s)

    start = lambda cp, s: cp.start(priority=s % DMA_QUEUES)

    @pl.when(i == 0)
    def _():
        rows(pos_ref, 0, start)

    @pl.when(i + 1 < pl.num_programs(0))
    def _():
        rows(pos_next_ref, 1 - slot, start)

    rows(pos_ref, slot, lambda cp, s: cp.wait())
    gw = gw_ref[...]
    y = alpha * x_ref[...]
    for s in range(n_slots):
        y = y + gw[:, s:s + 1] * buf[slot, s]
    o_ref[...] = _layer_norm(y, g_ref[...], b_ref[...], LN_EPS)


def _combine_ln(x, ys, pos, gw, g, b, alpha, tm=256):
    n, d = x.shape
    n_slots = pos.shape[0]
    nt = n // tm
    return pl.pallas_call(
        functools.partial(_combine_kernel, alpha=alpha),
        grid=(nt,),
        in_specs=[pl.BlockSpec((n_slots, tm), lambda i: (0, i), memory_space=pltpu.SMEM),
                  pl.BlockSpec((n_slots, tm), lambda i: (0, jnp.minimum(i + 1, nt - 1)), memory_space=pltpu.SMEM),
                  pl.BlockSpec((tm, d), lambda i: (i, 0)), pl.BlockSpec((tm, n_slots), lambda i: (i, 0)),
                  pl.BlockSpec(memory_space=pl.ANY), pl.BlockSpec((1, d), lambda i: (0, 0)),
                  pl.BlockSpec((1, d), lambda i: (0, 0))],
        out_specs=pl.BlockSpec((tm, d), lambda i: (i, 0)),
        out_shape=jax.ShapeDtypeStruct((n, d), F32),
        scratch_shapes=[pltpu.VMEM((2, n_slots, tm, d), F32), pltpu.SemaphoreType.DMA((2,))],
        compiler_params=_cparams(("arbitrary",)),
        name="moe_combine",
    )(pos, pos, x, gw, ys, g.reshape(1, d), b.reshape(1, d))


def _moe_ln(x, meta, counts, w_gu, w_down, layer, g, b, alpha, tm=256):
    n, d = x.shape
    n_experts = w_down.shape[1]
    eid = meta[0:2].astype(jnp.int32)
    gw = meta[2:4].T
    rank = meta[4:6].astype(jnp.int32)
    counts = counts[:, 0].astype(jnp.int32)
    tiles = (counts + tm - 1) // tm
    tile_end = jnp.cumsum(tiles)
    row_off = (tile_end - tiles) * tm
    n_tiles = (eid.size + tm - 1) // tm + n_experts
    tile_expert = jnp.sum(jnp.arange(n_tiles)[:, None] >= tile_end[None, :], axis=1).astype(jnp.int32)
    tile_expert = jnp.minimum(tile_expert, n_experts - 1)
    n_active = tile_end[-1:].astype(jnp.int32)
    ids = jnp.arange(n_experts)
    later = (ids[None, :] > ids[:, None]) & (tiles[None, :] > 0)
    next_of = jnp.min(jnp.where(later, ids[None, :], n_experts), axis=1)
    next_of = jnp.where(next_of < n_experts, next_of, -1).astype(jnp.int32)
    lookup = lambda table, ids: jnp.sum(jnp.where(ids[..., None] == jnp.arange(n_experts), table, 0), axis=-1)
    next_expert = lookup(next_of, tile_expert)
    pos = lookup(row_off, eid) + rank
    last_tile = jnp.maximum(tile_end - 1, 0)
    idle_tile = jnp.minimum(tile_end[-1] + jnp.arange(n_experts), n_tiles - 1)
    zero_ids = jnp.concatenate([last_tile, idle_tile]).astype(jnp.int32)
    xs = _dispatch(x, pos, zero_ids, n_tiles * tm, tm)
    ys = _experts(xs, w_gu, w_down, layer, tile_expert, next_expert, n_active, tm)
    return _combine_ln(x, ys, pos, gw, g, b, alpha)


def _token_shift(x_ref, prev_ref, first):
    x = x_ref[...]
    prev_row = jnp.where(first, 0.0, prev_ref[7:8, :])
    row = lax.broadcasted_iota(jnp.int32, x.shape, 0)
    return x, jnp.where(row == 0, prev_row, pltpu.roll(x, 1, 0))


def _rkv_kernel(x_ref, prev_ref, mu_ref, w_ref, o_ref, *, tiles_per_seq):
    first = pl.program_id(0) % tiles_per_seq == 0
    x, xp = _token_shift(x_ref, prev_ref, first)
    xm = x + (xp - x) * mu_ref[0]
    o_ref[0] = jnp.dot(xm.astype(BF16), w_ref[0], preferred_element_type=F32).astype(o_ref.dtype)


def _rkv_proj(x, mu3, w_rkv, t, tm=1024):
    n, d = x.shape
    sub = tm // 8
    return pl.pallas_call(
        functools.partial(_rkv_kernel, tiles_per_seq=t // tm),
        grid=(n // tm, 3),
        in_specs=[pl.BlockSpec((tm, d), lambda i, j: (i, 0)),
                  pl.BlockSpec((8, d), lambda i, j: (jnp.maximum(i * sub - 1, 0), 0)),
                  pl.BlockSpec((1, 1, d), lambda i, j: (j, 0, 0)), pl.BlockSpec((1, d, d), lambda i, j: (j, 0, 0))],
        out_specs=pl.BlockSpec((1, tm, d), lambda i, j: (j, i, 0)),
        out_shape=jax.ShapeDtypeStruct((3, n, d), BF16),
        compiler_params=_cparams(("parallel", "arbitrary")),
        name="rkv_proj",
    )(x, x, mu3.reshape(3, 1, d), w_rkv)


def _lora_kernel(x_ref, prev_ref, mu_ref, w0_ref, w1_ref, w2_ref, a0_ref, a1_ref, a2_ref, g1_ref, g2_ref, wl_ref,
                 a_ref, g_ref, *, tiles_per_seq):
    first = pl.program_id(0) % tiles_per_seq == 0
    x, xp = _token_shift(x_ref, prev_ref, first)
    dx = xp - x

    def mm(a, w_ref):
        return jnp.dot(a.astype(BF16), w_ref[...], preferred_element_type=F32)

    zw = w0_ref[...] + mm(jnp.tanh(mm(x + dx * mu_ref[0:1, :], w1_ref)), w2_ref)
    wl_ref[...] = -HALF_DECAY * _sigmoid(zw)
    za = a0_ref[...] + mm(mm(x + dx * mu_ref[1:2, :], a1_ref), a2_ref)
    a_ref[...] = _sigmoid(za).astype(a_ref.dtype)
    zg = mm(x + dx * mu_ref[2:3, :], g1_ref)
    g_ref[...] = mm(_sigmoid(zg), g2_ref).astype(g_ref.dtype)


def _lora(x, mu3, w0, w1, w2, a0, a1, a2, g1, g2, t, tm=512):
    n, d = x.shape
    sub = tm // 8
    const = lambda shape: pl.BlockSpec(shape, lambda i: (0, 0))
    row = pl.BlockSpec((tm, d), lambda i: (i, 0))
    return pl.pallas_call(
        functools.partial(_lora_kernel, tiles_per_seq=t // tm),
        grid=(n // tm,),
        in_specs=[row, pl.BlockSpec((8, d), lambda i: (jnp.maximum(i * sub - 1, 0), 0)), const((3, d)),
                  const((1, d)), const(w1.shape), const(w2.shape), const((1, d)), const(a1.shape), const(a2.shape),
                  const(g1.shape), const(g2.shape)],
        out_specs=[row, row, row],
        out_shape=[jax.ShapeDtypeStruct((n, d), F32), jax.ShapeDtypeStruct((n, d), BF16),
                   jax.ShapeDtypeStruct((n, d), BF16)],
        compiler_params=_cparams(("parallel",)),
        name="lora",
    )(x, x, mu3, w0.reshape(1, d), w1, w2, a0.reshape(1, d), a1, a2, g1, g2)


def _scan_groups(r, k, v, lw, a_gate, g, kk_w, ka_w, rk_w, gg, gb, s0, *, hd, gn_eps):
    c = r[0].shape[0]
    nh = LANES // hd
    hc = nh * c
    each = lambda f, *xs: [f(*x) for x in zip(*xs)]
    lane = lax.broadcasted_iota(jnp.int32, (1, LANES), 1)
    head_masks = [(lane >= h * hd) & (lane < (h + 1) * hd) for h in range(nh)]
    lr = lax.broadcasted_iota(jnp.int32, (LANES, LANES), 0) // hd
    lc = lax.broadcasted_iota(jnp.int32, (LANES, LANES), 1) // hd
    same_head = jnp.where(lr == lc, 1.0, 0.0).astype(BF16)
    trow = lax.broadcasted_iota(jnp.int32, (c, c), 0)
    tcol = lax.broadcasted_iota(jnp.int32, (c, c), 1)
    lower = jnp.where(trow >= tcol, 1.0, 0.0).astype(BF16)
    prow = lax.broadcasted_iota(jnp.int32, (hc, hc), 0)
    pcol = lax.broadcasted_iota(jnp.int32, (hc, hc), 1)
    same_blk = (prow // c) == (pcol // c)
    strict = same_blk & (prow > pcol)
    incl = same_blk & (prow >= pcol)
    eye = jnp.where(prow == pcol, 1.0, 0.0).astype(F32)

    def split(x):
        hi = x.astype(BF16)
        return hi, (x - hi.astype(F32)).astype(BF16)

    def head_sum(xs):
        return [jnp.dot(x.astype(BF16), same_head, preferred_element_type=F32) for x in xs]

    def per_head(x):
        return jnp.concatenate([jnp.where(hm, x, 0.0) for hm in head_masks], axis=0).astype(BF16)

    kk = each(lambda k_, w_: k_ * w_, k, kk_w)
    kk_n = head_sum(each(lambda x: x * x, kk))
    kk = each(lambda x, n_: x / jnp.maximum(jnp.sqrt(n_), 1e-12), kk, kk_n)
    k2 = each(lambda k_, a_, w_: k_ * (1.0 + (a_ - 1.0) * w_), k, a_gate, ka_w)
    lw_parts = each(split, lw)
    cum = [jnp.dot(lower, hi, preferred_element_type=F32) + jnp.dot(lower, lo, preferred_element_type=F32)
           for hi, lo in lw_parts]
    gam = each(jnp.exp, cum)
    inv_gam = each(lambda x: jnp.exp(-x), cum)
    gam_end = each(lambda x: x[c - 1:c, :], gam)
    a_t = each(lambda kk_, cum_, lw_: -kk_ * jnp.exp(cum_ - lw_), kk, cum, lw)
    b_t = each(lambda kk_, a_, ig: kk_ * a_ * ig, kk, a_gate, inv_gam)
    k_t = each(lambda k2_, ig: k2_ * ig, k2, inv_gam)
    r_t = each(lambda r_, gm: r_ * gm, r, gam)

    s0b = each(lambda x: x.astype(BF16), s0)
    ar_s0 = each(lambda a_, r_, s_: _nt(jnp.concatenate([a_, r_], axis=0).astype(BF16), s_), a_t, r_t, s0b)
    ar2 = each(lambda a_, r_: jnp.concatenate([per_head(a_), per_head(r_)], axis=0), a_t, r_t)
    bk2 = each(lambda b_, k_: jnp.concatenate([per_head(b_), per_head(k_)], axis=0), b_t, k_t)
    v2 = each(per_head, v)
    gmat = each(_nt, ar2, bk2)
    a_ab = each(lambda x: jnp.where(strict, x[:hc, :hc], 0.0), gmat)
    a_ak = each(lambda x: jnp.where(strict, x[:hc, hc:], 0.0), gmat)
    a_r = each(lambda x: jnp.concatenate([jnp.where(incl, x[hc:, :hc], 0.0), jnp.where(incl, x[hc:, hc:], 0.0)],
                                         axis=1).astype(BF16), gmat)
    rhs = each(lambda as0, ak, v_: jnp.concatenate([jnp.where(hm, as0[:c], 0.0) for hm in head_masks], axis=0)
               + _bdot(ak, v_), ar_s0, a_ak, v2)
    inv = each(lambda x: eye + x, a_ab)
    m = each(lambda x: _bdot(x, x), a_ab)
    n_pow = 2
    while 2 * n_pow < c:
        mp = each(lambda m_, p_: _bdot(jnp.concatenate([m_, p_], axis=0), m_), m, inv)
        inv = each(lambda p_, mp_: p_ + mp_[hc:], inv, mp)
        m = each(lambda mp_: mp_[:hc], mp)
        n_pow *= 2
    rhs = each(lambda rhs_, m_: rhs_ + _bdot(m_, rhs_), rhs, m)
    u2 = each(_bdot, inv, rhs)
    uv = each(lambda u_, v_: jnp.concatenate([u_.astype(BF16), v_], axis=0), u2, v2)
    y2 = each(lambda ar_, uv_: jnp.dot(ar_, uv_, preferred_element_type=F32), a_r, uv)
    y = each(lambda rs, y2_: rs[c:] + sum(y2_[h * c:(h + 1) * c, :] for h in range(nh)), ar_s0, y2)
    bkg = each(lambda b_, k_, ge: jnp.concatenate([per_head(b_ * ge), per_head(k_ * ge)], axis=0), b_t, k_t, gam_end)
    s_new = each(lambda s_, ge, uv_, bkg_: s_ * ge + _tn(uv_, bkg_), s0, gam_end, uv, bkg)

    inv_hd = 1.0 / hd
    mean = head_sum(y)
    yc = each(lambda y_, m_: y_ - m_ * inv_hd, y, mean)
    var = head_sum(each(lambda x: x * x, yc))
    yn = each(lambda yc_, var_, gg_, gb_: yc_ * lax.rsqrt(var_ * inv_hd + gn_eps) * gg_ + gb_, yc, var, gg, gb)
    rk_sum = head_sum(each(lambda r_, k2_, w_: r_ * k2_ * w_, r, k2, rk_w))
    out = each(lambda yn_, rk_, v_, g_: (yn_ + rk_ * v_) * g_, yn, rk_sum, v, g)
    return out, s_new


def _scan_kernel(r_ref, k_ref, v_ref, wl_ref, a_ref, g_ref, kk_ref, ka_ref, rk_ref, gg_ref, gb_ref, o_ref, state, *,
                 hd, gn_eps):
    @pl.when(pl.program_id(2) == 0)
    def _():
        state[...] = jnp.zeros_like(state)

    ng = state.shape[0]
    sls = [slice(p * LANES, (p + 1) * LANES) for p in range(ng)]
    tok3 = lambda ref: [ref[0, :, sl].astype(F32) for sl in sls]
    tok2 = lambda ref: [ref[:, sl].astype(F32) for sl in sls]
    out, s_new = _scan_groups(tok3(r_ref), tok3(k_ref), tok3(v_ref), tok2(wl_ref), tok2(a_ref), tok2(g_ref),
                              tok2(kk_ref), tok2(ka_ref), tok2(rk_ref), tok2(gg_ref), tok2(gb_ref),
                              [state[p] for p in range(ng)], hd=hd, gn_eps=gn_eps)
    for p in range(ng):
        state[p] = s_new[p]
        o_ref[:, sls[p]] = out[p].astype(o_ref.dtype)


def _rwkv_scan(rkv, wl, a, g, k_k, k_a, r_k, gn_g, gn_b, batch, hd, gn_eps, chunk=64, groups=16):
    _, n, d = rkv.shape
    t = n // batch
    nc = t // chunk
    groups = min(groups, d // LANES)
    w = groups * LANES
    tok = lambda j: pl.BlockSpec((1, chunk, w), functools.partial(lambda b, p, c, j: (j, b * nc + c, p), j=j))
    tok2 = pl.BlockSpec((chunk, w), lambda b, p, c: (b * nc + c, p))
    par = pl.BlockSpec((1, w), lambda b, p, c: (0, p))
    return pl.pallas_call(
        functools.partial(_scan_kernel, hd=hd, gn_eps=gn_eps),
        grid=(batch, d // w, nc),
        in_specs=[tok(0), tok(1), tok(2), tok2, tok2, tok2, par, par, par, par, par],
        out_specs=tok2,
        out_shape=jax.ShapeDtypeStruct((n, d), BF16),
        scratch_shapes=[pltpu.VMEM((groups, LANES, LANES), F32)],
        compiler_params=_cparams(("parallel", "parallel", "arbitrary")),
        name="rwkv_scan",
    )(rkv, rkv, rkv, wl, a, g, k_k.reshape(1, d), k_a.reshape(1, d), r_k.reshape(1, d), gn_g.reshape(1, d),
      gn_b.reshape(1, d))


def _pad_lora(w_in, w_out):
    r = w_in.shape[1]
    rp = -(-r // LANES) * LANES
    return (jnp.pad(w_in, ((0, 0), (0, rp - r))).astype(BF16), jnp.pad(w_out, ((0, rp - r), (0, 0))).astype(BF16))


def kernel(x, ev_w_in, ev_b_a, ev_w_s, ev_b_s, ev_g_v, ev_b_v, ev_b_f, ev_w_out, rw_mu, rw_w_rkv, rw_w0, rw_w1, rw_w2, rw_a0, rw_a1, rw_a2, rw_g1, rw_g2, rw_k_k, rw_k_a, rw_r_k, rw_gn_g, rw_gn_b, rw_w_o, ln_g, ln_b, w_router, b_router, w_gu, w_down):
    batch, t, d = x.shape
    depth = ln_g.shape[0]
    alpha = (2 * depth) ** 0.25
    h = x.reshape(batch * t, d)
    for layer in range(depth):
        i = layer // 2
        if layer % 2 == 0:
            aw = ev_g_v.shape[1]
            heads = ev_b_f.shape[1]
            q_col = 2 * aw
            bw = heads * LANES
            k_col, v_col, f_col = q_col + bw, q_col + 2 * bw, q_col + 3 * bw
            w_in = ev_w_in[i]
            col = jnp.arange(v_col)
            q_scale = jnp.where((col >= q_col) & (col < k_col), LANES ** -0.5 * LOG2E, 1.0)
            proj = _matmul(h, (w_in[:, :v_col] * q_scale).astype(BF16), v_col, BF16)
            vt, f_logit = _proj_transposed(h, w_in[:, v_col:f_col].T.astype(BF16), w_in[:, f_col:].T, FOX_BLOCK)
            c = _fgate(f_logit, ev_b_f[i], batch)
            y_a = _sgu(proj, ev_b_a[i], ev_w_s[i], ev_b_s[i], ev_g_v[i], ev_b_v[i])
            y_b = _fox_attention(proj, vt, c, batch, heads, q_col, FOX_BLOCK)
            h, meta, counts = _proj_ln_route([y_a, y_b], ev_w_out[i].astype(BF16), h, ln_g[layer, 0],
                                             ln_b[layer, 0], alpha, w_router, b_router)
        else:
            hd = rw_r_k.shape[2]
            mu = rw_mu[i]
            rkv = _rkv_proj(h, mu[:3], rw_w_rkv[i].astype(BF16), t)
            w1, w2 = _pad_lora(rw_w1[i], rw_w2[i])
            a1, a2 = _pad_lora(rw_a1[i], rw_a2[i])
            g1, g2 = _pad_lora(rw_g1[i], rw_g2[i])
            wl, a, g = _lora(h, mu[3:], rw_w0[i], w1, w2, rw_a0[i], a1, a2, g1, g2, t)
            y = _rwkv_scan(rkv, wl, a, g, rw_k_k[i], rw_k_a[i], rw_r_k[i].reshape(-1), rw_gn_g[i], rw_gn_b[i],
                           batch, hd, hd * 1e-5)
            h, meta, counts = _proj_ln_route([y], rw_w_o[i].astype(BF16), h, ln_g[layer, 0], ln_b[layer, 0], alpha,
                                             w_router, b_router)
        h = _moe_ln(h, meta, counts, w_gu, w_down, layer, ln_g[layer, 1], ln_b[layer, 1], alpha)
    return h.reshape(batch, t, d)
```

```python
import functools

import jax
import jax.numpy as jnp
from jax import lax
from jax.experimental import pallas as pl
from jax.experimental.pallas import tpu as pltpu

F32 = jnp.float32
BF16 = jnp.bfloat16
HI = lax.Precision.HIGHEST

LN_EPS = 1e-5
N_GROUPS = 4
LANES = 128
FOX_BLOCK = 512
LOG2E = 1.4426950408889634
HALF_DECAY = 0.6065306597126334
DMA_QUEUES = 2
VMEM_LIMIT = 56 * 1024 * 1024


def _cparams(sem):
    return pltpu.CompilerParams(dimension_semantics=sem, vmem_limit_bytes=VMEM_LIMIT)


def _layer_norm(x, g, b, eps):
    mu = jnp.mean(x, -1, keepdims=True)
    xc = x - mu
    var = jnp.mean(xc * xc, -1, keepdims=True)
    return xc * lax.rsqrt(var + eps) * g + b


def _sigmoid(x):
    return 0.5 + 0.5 * jnp.tanh(0.5 * x)


def _nt(a, b, **kw):
    return lax.dot_general(a, b, (((1,), (1,)), ((), ())), preferred_element_type=F32, **kw)


def _tn(a, b, **kw):
    return lax.dot_general(a, b, (((0,), (0,)), ((), ())), preferred_element_type=F32, **kw)


def _bdot(a, b):
    return jnp.dot(a.astype(BF16), b.astype(BF16), preferred_element_type=F32)


def _mm_kernel(a_ref, w_ref, o_ref, a_bf16):
    @pl.when(pl.program_id(1) == 0)
    def _():
        a_bf16[...] = a_ref[...].astype(BF16)

    o_ref[...] = jnp.dot(a_bf16[...], w_ref[...], preferred_element_type=F32).astype(o_ref.dtype)


def _matmul(a, w, n_cols, out_dtype, tm=1024, tn=1024):
    m, k = a.shape
    tm = min(tm, m)
    tn = next(c for c in (tn, 256, LANES) if n_cols % c == 0)
    return pl.pallas_call(
        _mm_kernel,
        grid=(m // tm, n_cols // tn),
        in_specs=[pl.BlockSpec((tm, k), lambda i, j: (i, 0)), pl.BlockSpec((k, tn), lambda i, j: (0, j))],
        out_specs=pl.BlockSpec((tm, tn), lambda i, j: (i, j)),
        out_shape=jax.ShapeDtypeStruct((m, n_cols), out_dtype),
        scratch_shapes=[pltpu.VMEM((tm, k), BF16)],
        compiler_params=_cparams(("parallel", "arbitrary")),
        name="matmul",
    )(a, w)


def _proj_ln_kernel(*refs, n_in, alpha):
    a_refs, w_refs = refs[:n_in], refs[n_in:2 * n_in]
    res_ref, g_ref, b_ref, wr_ref, br_ref, o_ref, meta_ref, cnt_ref, carry = refs[2 * n_in:]

    @pl.when(pl.program_id(0) == 0)
    def _():
        carry[...] = jnp.zeros_like(carry)

    acc = alpha * res_ref[...]
    for a_ref, w_ref in zip(a_refs, w_refs):
        acc = acc + jnp.dot(a_ref[...], w_ref[...], preferred_element_type=F32)
    y = _layer_norm(acc, g_ref[...], b_ref[...], LN_EPS)
    o_ref[...] = y
    meta_ref[...] = _route_tile(y, wr_ref, br_ref, carry)
    cnt_ref[...] = jnp.broadcast_to(carry[...], cnt_ref.shape)


def _proj_ln_route(a_list, w, res, g, b, alpha, w_router, b_router, tm=512):
    m, d = res.shape
    n_in = len(a_list)
    kc = a_list[0].shape[1]
    n_experts = w_router.shape[1]
    wr = w_router.T
    wr_hi = wr.astype(BF16)
    wr = jnp.stack([wr_hi, (wr - wr_hi.astype(F32)).astype(BF16)])
    in_specs = [pl.BlockSpec((tm, kc), lambda i: (i, 0)) for _ in a_list]
    in_specs += [pl.BlockSpec((kc, d), functools.partial(lambda i, r: (r, 0), r=r)) for r in range(n_in)]
    in_specs += [pl.BlockSpec((tm, d), lambda i: (i, 0)), pl.BlockSpec((1, d), lambda i: (0, 0)),
                 pl.BlockSpec((1, d), lambda i: (0, 0)), pl.BlockSpec((2, n_experts, d), lambda i: (0, 0, 0)),
                 pl.BlockSpec((n_experts, 1), lambda i: (0, 0))]
    return pl.pallas_call(
        functools.partial(_proj_ln_kernel, n_in=n_in, alpha=alpha),
        grid=(m // tm,),
        in_specs=in_specs,
        out_specs=[pl.BlockSpec((tm, d), lambda i: (i, 0)), pl.BlockSpec((8, tm), lambda i: (0, i)),
                   pl.BlockSpec((n_experts, LANES), lambda i: (0, 0))],
        out_shape=[jax.ShapeDtypeStruct((m, d), F32), jax.ShapeDtypeStruct((8, m), F32),
                   jax.ShapeDtypeStruct((n_experts, LANES), F32)],
        scratch_shapes=[pltpu.VMEM((n_experts, 1), F32)],
        compiler_params=_cparams(("arbitrary",)),
        name="proj_ln_route",
    )(*a_list, *([w] * n_in), res, g.reshape(1, d), b.reshape(1, d), wr, b_router.reshape(n_experts, 1))


def _fgate_kernel(z_ref, bf_ref, c_ref, carry):
    @pl.when(pl.program_id(1) == 0)
    def _():
        carry[...] = jnp.zeros_like(carry)

    tm = z_ref.shape[2]
    z = z_ref[0] + bf_ref[...]
    log_f = jnp.minimum(z, 0.0) - jnp.log1p(jnp.exp(-jnp.abs(z)))
    row = lax.broadcasted_iota(jnp.int32, (tm, tm), 0)
    col = lax.broadcasted_iota(jnp.int32, (tm, tm), 1)
    upper = jnp.where(row <= col, 1.0, 0.0).astype(F32)
    c = jnp.dot(log_f, upper, preferred_element_type=F32, precision=HI) + carry[...]
    c_ref[0] = c
    carry[...] = carry[...] + jnp.sum(log_f, axis=-1, keepdims=True)


def _fgate(f_logit, b_f, batch):
    tiles, h, tm = f_logit.shape
    nt = tiles // batch
    return pl.pallas_call(
        _fgate_kernel,
        grid=(batch, nt),
        in_specs=[pl.BlockSpec((1, h, tm), lambda b, i: (b * nt + i, 0, 0)), pl.BlockSpec((h, 1), lambda b, i: (0, 0))],
        out_specs=pl.BlockSpec((1, h, tm), lambda b, i: (b, 0, i)),
        out_shape=jax.ShapeDtypeStruct((batch, h, nt * tm), F32),
        scratch_shapes=[pltpu.VMEM((h, 1), F32)],
        compiler_params=_cparams(("parallel", "arbitrary")),
        name="fgate",
    )(f_logit, b_f.reshape(h, 1))


def _gelu_tanh(x):
    return 0.5 * x * (1.0 + jnp.tanh(0.7978845608028654 * (x + 0.044715 * (x * x * x))))


def _sgu_kernel(z_ref, ba_ref, ws_ref, bs_ref, gv_ref, bv_ref, o_ref, *, chunk, groups):
    aw = o_ref.shape[1]
    gd = aw // groups
    z = _gelu_tanh(z_ref[...].astype(F32) + ba_ref[...])
    u = z[:, :aw]
    v = _layer_norm(z[:, aw:], gv_ref[...], bv_ref[...], LN_EPS).astype(BF16)
    row = lax.broadcasted_iota(jnp.int32, (chunk, chunk), 0)
    col = lax.broadcasted_iota(jnp.int32, (chunk, chunk), 1)
    causal = row >= col
    bs = bs_ref[...]
    for g in range(groups):
        w_g = jnp.where(causal, ws_ref[g], 0.0).astype(BF16)
        for c in range(z.shape[0] // chunk):
            rs = slice(c * chunk, (c + 1) * chunk)
            cs = slice(g * gd, (g + 1) * gd)
            s = jnp.dot(w_g, v[rs, cs], preferred_element_type=F32) + bs[:, g:g + 1]
            o_ref[rs, cs] = (u[rs, cs] * s).astype(o_ref.dtype)


def _sgu(proj, b_a, w_s, b_s, g_v, b_v, tm=512):
    n = proj.shape[0]
    groups, chunk, _ = w_s.shape
    aw = g_v.shape[0]
    return pl.pallas_call(
        functools.partial(_sgu_kernel, chunk=chunk, groups=groups),
        grid=(n // tm,),
        in_specs=[pl.BlockSpec((tm, 2 * aw), lambda i: (i, 0)), pl.BlockSpec((1, 2 * aw), lambda i: (0, 0)),
                  pl.BlockSpec((groups, chunk, chunk), lambda i: (0, 0, 0)),
                  pl.BlockSpec((chunk, groups), lambda i: (0, 0)), pl.BlockSpec((1, aw), lambda i: (0, 0)),
                  pl.BlockSpec((1, aw), lambda i: (0, 0))],
        out_specs=pl.BlockSpec((tm, aw), lambda i: (i, 0)),
        out_shape=jax.ShapeDtypeStruct((n, aw), BF16),
        compiler_params=_cparams(("parallel",)),
        name="sgu",
    )(proj, b_a.reshape(1, -1), w_s, b_s.T, g_v.reshape(1, aw), b_v.reshape(1, aw))


def _vt_kernel(x_ref, w_ref, wf_ref, o_ref, f_ref):
    x = x_ref[...]
    x_hi = x.astype(BF16)
    x_lo = (x - x_hi.astype(F32)).astype(BF16)
    o_ref[0] = _nt(w_ref[...], x_hi).astype(o_ref.dtype)
    f_ref[0] = _nt(wf_ref[0], x_hi) + _nt(wf_ref[0], x_lo) + _nt(wf_ref[1], x_hi)


def _proj_transposed(x, w_t, wf_t, blk):
    n, d = x.shape
    rows, rows_f = w_t.shape[0], wf_t.shape[0]
    wf_hi = wf_t.astype(BF16)
    wf = jnp.stack([wf_hi, (wf_t - wf_hi.astype(F32)).astype(BF16)])
    return pl.pallas_call(
        _vt_kernel,
        grid=(n // blk,),
        in_specs=[pl.BlockSpec((blk, d), lambda i: (i, 0)), pl.BlockSpec((rows, d), lambda i: (0, 0)),
                  pl.BlockSpec((2, rows_f, d), lambda i: (0, 0, 0))],
        out_specs=[pl.BlockSpec((1, rows, blk), lambda i: (i, 0, 0)),
                   pl.BlockSpec((1, rows_f, blk), lambda i: (i, 0, 0))],
        out_shape=[jax.ShapeDtypeStruct((n // blk, rows, blk), BF16),
                   jax.ShapeDtypeStruct((n // blk, rows_f, blk), F32)],
        compiler_params=_cparams(("parallel",)),
        name="proj_transposed",
    )(x, w_t, wf)


def _fox_kernel(q_ref, k_ref, vt_ref, cq_ref, ck_ref, o_ref, m_scr, l_scr, acc_scr, *, blk, nh):
    qi = pl.program_id(2)
    heads = range(nh)
    hs = [slice(h * LANES, (h + 1) * LANES) for h in heads]
    m_scr[...] = jnp.full_like(m_scr, -jnp.inf)
    l_scr[...] = jnp.zeros_like(l_scr)
    acc_scr[...] = jnp.zeros_like(acc_scr)
    q = [q_ref[:, hs[h]] for h in heads]
    cq = [cq_ref[0, h, pl.ds(qi, 1), :] * LOG2E for h in heads]

    def step(ki, masked):
        ks = pl.multiple_of(ki * blk, blk)
        t = [_nt(k_ref[pl.ds(ks, blk), hs[h]], q[h]) - ck_ref[0, h, pl.ds(ks, blk), :] * LOG2E for h in heads]
        if masked:
            row = lax.broadcasted_iota(jnp.int32, (blk, blk), 0)
            col = lax.broadcasted_iota(jnp.int32, (blk, blk), 1)
            t = [jnp.where(row <= col, x, -jnp.inf) for x in t]
        m_prev = [m_scr[h] for h in heads]
        m_new = [jnp.maximum(m_prev[h], cq[h] + jnp.max(t[h], axis=0, keepdims=True)) for h in heads]
        p = [jnp.exp2(t[h] - (m_new[h] - cq[h])) for h in heads]
        corr = [jnp.exp2(m_prev[h] - m_new[h]) for h in heads]
        for h in heads:
            l_scr[h] = corr[h] * l_scr[h] + jnp.sum(p[h], axis=0, keepdims=True)
            acc_scr[h] = corr[h] * acc_scr[h] + jnp.dot(vt_ref[ki, hs[h], :], p[h].astype(BF16),
                                                         preferred_element_type=F32)
            m_scr[h] = m_new[h]

    def body(ki, carry):
        step(ki, False)
        return carry

    lax.fori_loop(0, qi, body, 0)
    step(qi, True)
    for h in heads:
        o_ref[:, hs[h]] = jnp.transpose(acc_scr[h] / l_scr[h]).astype(o_ref.dtype)


def _fox_attention(proj, vt, c, batch, heads, q_col, blk, nh=4):
    n = proj.shape[0]
    t = n // batch
    nb = t // blk
    dh = LANES
    nh = min(nh, heads)
    w = nh * dh
    q0, k0 = q_col // w, (q_col + heads * dh) // w
    c_col = c.reshape(batch, heads, t, 1)
    c_row = c.reshape(batch, heads, nb, blk)
    return pl.pallas_call(
        functools.partial(_fox_kernel, blk=blk, nh=nh),
        grid=(batch, heads // nh, nb),
        in_specs=[pl.BlockSpec((blk, w), lambda b, h, i: (b * nb + i, q0 + h)),
                  pl.BlockSpec((t, w), lambda b, h, i: (b, k0 + h)),
                  pl.BlockSpec((nb, w, blk), lambda b, h, i: (b, h, 0)),
                  pl.BlockSpec((1, nh, nb, blk), lambda b, h, i: (b, h, 0, 0)),
                  pl.BlockSpec((1, nh, t, 1), lambda b, h, i: (b, h, 0, 0))],
        out_specs=pl.BlockSpec((blk, w), lambda b, h, i: (b * nb + i, h)),
        out_shape=jax.ShapeDtypeStruct((n, heads * dh), BF16),
        scratch_shapes=[pltpu.VMEM((nh, 1, blk), F32), pltpu.VMEM((nh, 1, blk), F32),
                        pltpu.VMEM((nh, dh, blk), F32)],
        compiler_params=_cparams(("parallel", "parallel", "arbitrary")),
        name="fox_attention",
    )(proj, proj, vt, c_row, c_col)


def _first_max(p, idx, valid):
    pm = jnp.where(valid, p, -2.0)
    m = jnp.max(pm, axis=0, keepdims=True)
    first = jnp.min(jnp.where(pm == m, idx, float(p.shape[0])), axis=0, keepdims=True)
    return m, first


def _route_tile(x, wr_ref, br_ref, carry):
    tm = x.shape[0]
    n_experts = br_ref.shape[0]
    per = n_experts // N_GROUPS
    x_hi = x.astype(BF16)
    x_lo = (x - x_hi.astype(F32)).astype(BF16)
    logits = _nt(wr_ref[0], x_hi) + _nt(wr_ref[0], x_lo) + _nt(wr_ref[1], x_hi) + br_ref[...]
    e = jnp.exp(logits - jnp.max(logits, axis=0, keepdims=True))
    probs = e / jnp.sum(e, axis=0, keepdims=True)
    idx = lax.broadcasted_iota(jnp.int32, (n_experts, tm), 0).astype(F32)
    best_score = jnp.full((1, tm), -1.0, F32)
    best_group = jnp.zeros((1, tm), F32)
    for grp in range(N_GROUPS):
        in_g = (idx >= grp * per) & (idx < (grp + 1) * per)
        m1, i1 = _first_max(probs, idx, in_g)
        m2, _ = _first_max(probs, idx, in_g & (idx != i1))
        score = m1 + m2
        take = score > best_score
        best_score = jnp.where(take, score, best_score)
        best_group = jnp.where(take, float(grp), best_group)
    in_sel = (idx >= best_group * per) & (idx < (best_group + 1) * per)
    p1, i1 = _first_max(probs, idx, in_sel)
    p2, i2 = _first_max(probs, idx, in_sel & (idx != i1))
    tot = p1 + p2
    row = lax.broadcasted_iota(jnp.int32, (tm, tm), 0)
    col = lax.broadcasted_iota(jnp.int32, (tm, tm), 1)
    earlier = jnp.where(row < col, 1.0, 0.0).astype(BF16)
    onehot = jnp.where((idx == i1) | (idx == i2), 1.0, 0.0)
    seen = jnp.dot(onehot.astype(BF16), earlier, preferred_element_type=F32) + carry[...]
    r1 = jnp.sum(jnp.where(idx == i1, seen, 0.0), axis=0, keepdims=True)
    r2 = jnp.sum(jnp.where(idx == i2, seen, 0.0), axis=0, keepdims=True)
    zero = jnp.zeros_like(r1)
    carry[...] = carry[...] + jnp.sum(onehot, axis=1, keepdims=True)
    return jnp.concatenate([i1, i2, p1 / tot, p2 / tot, r1, r2, zero, zero], axis=0)


def _row_copy(src_ref, src_row, dst_ref, dst_row, sem):
    return pltpu.make_async_copy(src_ref.at[pl.ds(src_row, 1)], dst_ref.at[pl.ds(dst_row, 1)], sem)


def _dispatch_kernel(zid_ref, pos_ref, x_hbm, xs_ref, zbuf, stage, row_sems, load_sems, zsem, *, tm, tok):
    i = pl.program_id(0)
    nt = pl.num_programs(0)
    n_slots = pos_ref.shape[0]
    n_stage = stage.shape[0]

    def zero_tiles(fn):
        for j in range(zid_ref.shape[0]):
            new_id = zid_ref[j] != zid_ref[max(j - 1, 0)] if j else True

            @pl.when(new_id)
            def _():
                fn(pltpu.make_async_copy(zbuf, xs_ref.at[pl.ds(zid_ref[j] * tm, tm)], zsem))

    def load(t, slot):
        return pltpu.make_async_copy(x_hbm.at[pl.ds(t * tok, tok)], stage.at[slot], load_sems.at[slot])

    def drain(slot):
        for _ in range(tok * n_slots):
            _row_copy(stage.at[slot], 0, xs_ref, 0, row_sems.at[slot]).wait()

    @pl.when(i == 0)
    def _():
        zbuf[...] = jnp.zeros_like(zbuf)
        zero_tiles(lambda cp: cp.start())
        load(0, 0).start()
        zero_tiles(lambda cp: cp.wait())

    @pl.when((i == 0) & (nt > 1))
    def _():
        load(1, 1).start()

    cur = i % n_stage
    prev = (i + n_stage - 1) % n_stage
    load(i, cur).wait()
    for r in range(tok):
        for s in range(n_slots):
            _row_copy(stage.at[cur], r, xs_ref, pos_ref[s, r], row_sems.at[cur]).start(priority=s % DMA_QUEUES)

    @pl.when(i > 0)
    def _():
        drain(prev)

    @pl.when(i + 2 < nt)
    def _():
        load(i + 2, prev).start()

    @pl.when(i == nt - 1)
    def _():
        drain(cur)


def _dispatch(x, pos, zero_ids, n_rows, tm, tok=512):
    n, d = x.shape
    grid_spec = pltpu.PrefetchScalarGridSpec(
        num_scalar_prefetch=1,
        grid=(n // tok,),
        in_specs=[pl.BlockSpec((pos.shape[0], tok), lambda i, z: (0, i), memory_space=pltpu.SMEM),
                  pl.BlockSpec(memory_space=pl.ANY)],
        out_specs=pl.BlockSpec(memory_space=pl.ANY),
        scratch_shapes=[pltpu.VMEM((tm, d), x.dtype), pltpu.VMEM((3, tok, d), x.dtype),
                        pltpu.SemaphoreType.DMA((3,)), pltpu.SemaphoreType.DMA((3,)), pltpu.SemaphoreType.DMA],
    )
    return pl.pallas_call(
        functools.partial(_dispatch_kernel, tm=tm, tok=tok),
        grid_spec=grid_spec,
        out_shape=jax.ShapeDtypeStruct((n_rows, d), x.dtype),
        compiler_params=_cparams(("arbitrary",)),
        name="moe_dispatch",
    )(zero_ids, pos, x)


def _experts_kernel(te_ref, nx_ref, na_ref, xs_ref, wgu_hbm, wd_hbm, ys_ref, wgu_f, wd_f, wgu_b, wd_b, sems, *, layer):
    j = pl.program_id(0)
    live = j < na_ref[0]
    fresh = (j == 0) | (te_ref[j] != te_ref[jnp.maximum(j - 1, 0)])

    def fetch(e):
        return (pltpu.make_async_copy(wgu_hbm.at[layer, e], wgu_f, sems.at[0]),
                pltpu.make_async_copy(wd_hbm.at[layer, e], wd_f, sems.at[1]))

    @pl.when(live & (j == 0))
    def _():
        for cp in fetch(te_ref[0]):
            cp.start()

    @pl.when(live & fresh)
    def _():
        for cp in fetch(te_ref[j]):
            cp.wait()
        wgu_b[...] = wgu_f[...].astype(BF16)
        wd_b[...] = wd_f[...].astype(BF16)

    @pl.when(live & fresh & (nx_ref[j] >= 0))
    def _():
        for cp in fetch(nx_ref[j]):
            cp.start()

    @pl.when(live)
    def _():
        de = wd_b.shape[0]
        gu = jnp.dot(xs_ref[...].astype(BF16), wgu_b[...], preferred_element_type=F32)
        gpart, upart = gu[:, :de], gu[:, de:]
        h = (gpart / (1.0 + jnp.exp(-gpart))) * upart
        ys_ref[...] = jnp.dot(h.astype(BF16), wd_b[...], preferred_element_type=F32)

    @pl.when(jnp.logical_not(live))
    def _():
        ys_ref[...] = jnp.zeros_like(ys_ref)


def _experts(xs, w_gu, w_down, layer, tile_expert, next_expert, n_active, tm):
    p, d = xs.shape
    de = w_down.shape[2]
    live = lambda j, na: jnp.maximum(jnp.minimum(j, na[0] - 1), 0)
    grid_spec = pltpu.PrefetchScalarGridSpec(
        num_scalar_prefetch=3,
        grid=(p // tm,),
        in_specs=[pl.BlockSpec((tm, d), lambda j, te, nx, na: (live(j, na), 0)),
                  pl.BlockSpec(memory_space=pl.ANY), pl.BlockSpec(memory_space=pl.ANY)],
        out_specs=pl.BlockSpec((tm, d), lambda j, te, nx, na: (j, 0)),
        scratch_shapes=[pltpu.VMEM((d, 2 * de), F32), pltpu.VMEM((de, d), F32), pltpu.VMEM((d, 2 * de), BF16),
                        pltpu.VMEM((de, d), BF16), pltpu.SemaphoreType.DMA((2,))],
    )
    return pl.pallas_call(
        functools.partial(_experts_kernel, layer=layer),
        grid_spec=grid_spec,
        out_shape=jax.ShapeDtypeStruct((p, d), F32),
        compiler_params=_cparams(("arbitrary",)),
        name="moe_experts",
    )(tile_expert, next_expert, n_active, xs, w_gu, w_down)


def _combine_kernel(pos_ref, pos_next_ref, x_ref, gw_ref, ys_ref, g_ref, b_ref, o_ref, buf, sems, *, alpha):
    i = pl.program_id(0)
    slot = i % 2
    tm = x_ref.shape[0]
    n_slots = pos_ref.shape[0]

    def rows(p_ref, sl, fn):
        for r in range(tm):
            for s in range(n_slots):
                fn(_row_copy(ys_ref, p_ref[s, r], buf.at[sl, s], r, sems.at[sl]), s)

    start = lambda cp, s: cp.start(priority=s % DMA_QUEUES)

    @pl.when(i == 0)
    def _():
        rows(pos_ref, 0, start)

    @pl.when(i + 1 < pl.num_programs(0))
    def _():
        rows(pos_next_ref, 1 - slot, start)

    rows(pos_ref, slot, lambda cp, s: cp.wait())
    gw = gw_ref[...]
    y = alpha * x_ref[...]
    for s in range(n_slots):
        y = y + gw[:, s:s + 1] * buf[slot, s]
    o_ref[...] = _layer_norm(y, g_ref[...], b_ref[...], LN_EPS)


def _combine_ln(x, ys, pos, gw, g, b, alpha, tm=512):
    n, d = x.shape
    n_slots = pos.shape[0]
    nt = n // tm
    return pl.pallas_call(
        functools.partial(_combine_kernel, alpha=alpha),
        grid=(nt,),
        in_specs=[pl.BlockSpec((n_slots, tm), lambda i: (0, i), memory_space=pltpu.SMEM),
                  pl.BlockSpec((n_slots, tm), lambda i: (0, jnp.minimum(i + 1, nt - 1)), memory_space=pltpu.SMEM),
                  pl.BlockSpec((tm, d), lambda i: (i, 0)), pl.BlockSpec((tm, n_slots), lambda i: (i, 0)),
                  pl.BlockSpec(memory_space=pl.ANY), pl.BlockSpec((1, d), lambda i: (0, 0)),
                  pl.BlockSpec((1, d), lambda i: (0, 0))],
        out_specs=pl.BlockSpec((tm, d), lambda i: (i, 0)),
        out_shape=jax.ShapeDtypeStruct((n, d), F32),
        scratch_shapes=[pltpu.VMEM((2, n_slots, tm, d), F32), pltpu.SemaphoreType.DMA((2,))],
        compiler_params=_cparams(("arbitrary",)),
        name="moe_combine",
    )(pos, pos, x, gw, ys, g.reshape(1, d), b.reshape(1, d))


def _moe_ln(x, meta, counts, w_gu, w_down, layer, g, b, alpha, tm=256):
    n, d = x.shape
    n_experts = w_down.shape[1]
    eid = meta[0:2].astype(jnp.int32)
    gw = meta[2:4].T
    rank = meta[4:6].astype(jnp.int32)
    counts = counts[:, 0].astype(jnp.int32)
    tiles = (counts + tm - 1) // tm
    tile_end = jnp.cumsum(tiles)
    row_off = (tile_end - tiles) * tm
    n_tiles = (eid.size + tm - 1) // tm + n_experts
    tile_expert = jnp.sum(jnp.arange(n_tiles)[:, None] >= tile_end[None, :], axis=1).astype(jnp.int32)
    tile_expert = jnp.minimum(tile_expert, n_experts - 1)
    n_active = tile_end[-1:].astype(jnp.int32)
    ids = jnp.arange(n_experts)
    later = (ids[None, :] > ids[:, None]) & (tiles[None, :] > 0)
    next_of = jnp.min(jnp.where(later, ids[None, :], n_experts), axis=1)
    next_of = jnp.where(next_of < n_experts, next_of, -1).astype(jnp.int32)
    lookup = lambda table, ids: jnp.sum(jnp.where(ids[..., None] == jnp.arange(n_experts), table, 0), axis=-1)
    next_expert = lookup(next_of, tile_expert)
    pos = lookup(row_off, eid) + rank
    last_tile = jnp.maximum(tile_end - 1, 0)
    idle_tile = jnp.minimum(tile_end[-1] + jnp.arange(n_experts), n_tiles - 1)
    zero_ids = jnp.concatenate([last_tile, idle_tile]).astype(jnp.int32)
    xs = _dispatch(x, pos, zero_ids, n_tiles * tm, tm)
    ys = _experts(xs, w_gu, w_down, layer, tile_expert, next_expert, n_active, tm)
    return _combine_ln(x, ys, pos, gw, g, b, alpha)


def _token_shift(x_ref, prev_ref, first):
    x = x_ref[...]
    prev_row = jnp.where(first, 0.0, prev_ref[7:8, :])
    row = lax.broadcasted_iota(jnp.int32, x.shape, 0)
    return x, jnp.where(row == 0, prev_row, pltpu.roll(x, 1, 0))


def _rkv_kernel(x_ref, prev_ref, mu_ref, w_ref, o_ref, *, tiles_per_seq):
    first = pl.program_id(0) % tiles_per_seq == 0
    x, xp = _token_shift(x_ref, prev_ref, first)
    xm = x + (xp - x) * mu_ref[0]
    o_ref[0] = jnp.dot(xm.astype(BF16), w_ref[0], preferred_element_type=F32).astype(o_ref.dtype)


def _rkv_proj(x, mu3, w_rkv, t, tm=1024):
    n, d = x.shape
    sub = tm // 8
    return pl.pallas_call(
        functools.partial(_rkv_kernel, tiles_per_seq=t // tm),
        grid=(n // tm, 3),
        in_specs=[pl.BlockSpec((tm, d), lambda i, j: (i, 0)),
                  pl.BlockSpec((8, d), lambda i, j: (jnp.maximum(i * sub - 1, 0), 0)),
                  pl.BlockSpec((1, 1, d), lambda i, j: (j, 0, 0)), pl.BlockSpec((1, d, d), lambda i, j: (j, 0, 0))],
        out_specs=pl.BlockSpec((1, tm, d), lambda i, j: (j, i, 0)),
        out_shape=jax.ShapeDtypeStruct((3, n, d), BF16),
        compiler_params=_cparams(("parallel", "arbitrary")),
        name="rkv_proj",
    )(x, x, mu3.reshape(3, 1, d), w_rkv)


def _lora_kernel(x_ref, prev_ref, mu_ref, w0_ref, w1_ref, w2_ref, a0_ref, a1_ref, a2_ref, g1_ref, g2_ref, wl_ref,
                 a_ref, g_ref, *, tiles_per_seq):
    first = pl.program_id(0) % tiles_per_seq == 0
    x, xp = _token_shift(x_ref, prev_ref, first)
    dx = xp - x

    def mm(a, w_ref):
        return jnp.dot(a.astype(BF16), w_ref[...], preferred_element_type=F32)

    zw = w0_ref[...] + mm(jnp.tanh(mm(x + dx * mu_ref[0:1, :], w1_ref)), w2_ref)
    wl_ref[...] = -HALF_DECAY * _sigmoid(zw)
    za = a0_ref[...] + mm(mm(x + dx * mu_ref[1:2, :], a1_ref), a2_ref)
    a_ref[...] = _sigmoid(za).astype(a_ref.dtype)
    zg = mm(x + dx * mu_ref[2:3, :], g1_ref)
    g_ref[...] = mm(_sigmoid(zg), g2_ref).astype(g_ref.dtype)


def _lora(x, mu3, w0, w1, w2, a0, a1, a2, g1, g2, t, tm=512):
    n, d = x.shape
    sub = tm // 8
    const = lambda shape: pl.BlockSpec(shape, lambda i: (0, 0))
    row = pl.BlockSpec((tm, d), lambda i: (i, 0))
    return pl.pallas_call(
        functools.partial(_lora_kernel, tiles_per_seq=t // tm),
        grid=(n // tm,),
        in_specs=[row, pl.BlockSpec((8, d), lambda i: (jnp.maximum(i * sub - 1, 0), 0)), const((3, d)),
                  const((1, d)), const(w1.shape), const(w2.shape), const((1, d)), const(a1.shape), const(a2.shape),
                  const(g1.shape), const(g2.shape)],
        out_specs=[row, row, row],
        out_shape=[jax.ShapeDtypeStruct((n, d), F32), jax.ShapeDtypeStruct((n, d), BF16),
                   jax.ShapeDtypeStruct((n, d), BF16)],
        compiler_params=_cparams(("parallel",)),
        name="lora",
    )(x, x, mu3, w0.reshape(1, d), w1, w2, a0.reshape(1, d), a1, a2, g1, g2)


def _scan_groups(r, k, v, lw, a_gate, g, kk_w, ka_w, rk_w, gg, gb, s0, *, hd, gn_eps):
    c = r[0].shape[0]
    nh = LANES // hd
    hc = nh * c
    each = lambda f, *xs: [f(*x) for x in zip(*xs)]
    lane = lax.broadcasted_iota(jnp.int32, (1, LANES), 1)
    head_masks = [(lane >= h * hd) & (lane < (h + 1) * hd) for h in range(nh)]
    lr = lax.broadcasted_iota(jnp.int32, (LANES, LANES), 0) // hd
    lc = lax.broadcasted_iota(jnp.int32, (LANES, LANES), 1) // hd
    same_head = jnp.where(lr == lc, 1.0, 0.0).astype(BF16)
    trow = lax.broadcasted_iota(jnp.int32, (c, c), 0)
    tcol = lax.broadcasted_iota(jnp.int32, (c, c), 1)
    lower = jnp.where(trow >= tcol, 1.0, 0.0).astype(BF16)
    prow = lax.broadcasted_iota(jnp.int32, (hc, hc), 0)
    pcol = lax.broadcasted_iota(jnp.int32, (hc, hc), 1)
    same_blk = (prow // c) == (pcol // c)
    strict = same_blk & (prow > pcol)
    incl = same_blk & (prow >= pcol)
    eye = jnp.where(prow == pcol, 1.0, 0.0).astype(F32)

    def split(x):
        hi = x.astype(BF16)
        return hi, (x - hi.astype(F32)).astype(BF16)

    def head_sum(xs):
        return [jnp.dot(x.astype(BF16), same_head, preferred_element_type=F32) for x in xs]

    def per_head(x):
        return jnp.concatenate([jnp.where(hm, x, 0.0) for hm in head_masks], axis=0).astype(BF16)

    kk = each(lambda k_, w_: k_ * w_, k, kk_w)
    kk_n = head_sum(each(lambda x: x * x, kk))
    kk = each(lambda x, n_: x / jnp.maximum(jnp.sqrt(n_), 1e-12), kk, kk_n)
    k2 = each(lambda k_, a_, w_: k_ * (1.0 + (a_ - 1.0) * w_), k, a_gate, ka_w)
    lw_parts = each(split, lw)
    cum = [jnp.dot(lower, hi, preferred_element_type=F32) + jnp.dot(lower, lo, preferred_element_type=F32)
           for hi, lo in lw_parts]
    gam = each(jnp.exp, cum)
    inv_gam = each(lambda x: jnp.exp(-x), cum)
    gam_end = each(lambda x: x[c - 1:c, :], gam)
    a_t = each(lambda kk_, cum_, lw_: -kk_ * jnp.exp(cum_ - lw_), kk, cum, lw)
    b_t = each(lambda kk_, a_, ig: kk_ * a_ * ig, kk, a_gate, inv_gam)
    k_t = each(lambda k2_, ig: k2_ * ig, k2, inv_gam)
    r_t = each(lambda r_, gm: r_ * gm, r, gam)

    s0b = each(lambda x: x.astype(BF16), s0)
    ar_s0 = each(lambda a_, r_, s_: _nt(jnp.concatenate([a_, r_], axis=0).astype(BF16), s_), a_t, r_t, s0b)
    ar2 = each(lambda a_, r_: jnp.concatenate([per_head(a_), per_head(r_)], axis=0), a_t, r_t)
    bk2 = each(lambda b_, k_: jnp.concatenate([per_head(b_), per_head(k_)], axis=0), b_t, k_t)
    v2 = each(per_head, v)
    gmat = each(_nt, ar2, bk2)
    a_ab = each(lambda x: jnp.where(strict, x[:hc, :hc], 0.0), gmat)
    a_ak = each(lambda x: jnp.where(strict, x[:hc, hc:], 0.0), gmat)
    a_r = each(lambda x: jnp.concatenate([jnp.where(incl, x[hc:, :hc], 0.0), jnp.where(incl, x[hc:, hc:], 0.0)],
                                         axis=1).astype(BF16), gmat)
    rhs = each(lambda as0, ak, v_: jnp.concatenate([jnp.where(hm, as0[:c], 0.0) for hm in head_masks], axis=0)
               + _bdot(ak, v_), ar_s0, a_ak, v2)
    inv = each(lambda x: eye + x, a_ab)
    m = each(lambda x: _bdot(x, x), a_ab)
    n_pow = 2
    while 2 * n_pow < c:
        mp = each(lambda m_, p_: _bdot(jnp.concatenate([m_, p_], axis=0), m_), m, inv)
        inv = each(lambda p_, mp_: p_ + mp_[hc:], inv, mp)
        m = each(lambda mp_: mp_[:hc], mp)
        n_pow *= 2
    rhs = each(lambda rhs_, m_: rhs_ + _bdot(m_, rhs_), rhs, m)
    u2 = each(_bdot, inv, rhs)
    uv = each(lambda u_, v_: jnp.concatenate([u_.astype(BF16), v_], axis=0), u2, v2)
    y2 = each(lambda ar_, uv_: jnp.dot(ar_, uv_, preferred_element_type=F32), a_r, uv)
    y = each(lambda rs, y2_: rs[c:] + sum(y2_[h * c:(h + 1) * c, :] for h in range(nh)), ar_s0, y2)
    bkg = each(lambda b_, k_, ge: jnp.concatenate([per_head(b_ * ge), per_head(k_ * ge)], axis=0), b_t, k_t, gam_end)
    s_new = each(lambda s_, ge, uv_, bkg_: s_ * ge + _tn(uv_, bkg_), s0, gam_end, uv, bkg)

    inv_hd = 1.0 / hd
    mean = head_sum(y)
    yc = each(lambda y_, m_: y_ - m_ * inv_hd, y, mean)
    var = head_sum(each(lambda x: x * x, yc))
    yn = each(lambda yc_, var_, gg_, gb_: yc_ * lax.rsqrt(var_ * inv_hd + gn_eps) * gg_ + gb_, yc, var, gg, gb)
    rk_sum = head_sum(each(lambda r_, k2_, w_: r_ * k2_ * w_, r, k2, rk_w))
    out = each(lambda yn_, rk_, v_, g_: (yn_ + rk_ * v_) * g_, yn, rk_sum, v, g)
    return out, s_new


def _scan_kernel(r_ref, k_ref, v_ref, wl_ref, a_ref, g_ref, kk_ref, ka_ref, rk_ref, gg_ref, gb_ref, o_ref, state, *,
                 hd, gn_eps):
    @pl.when(pl.program_id(2) == 0)
    def _():
        state[...] = jnp.zeros_like(state)

    ng = state.shape[0]
    sls = [slice(p * LANES, (p + 1) * LANES) for p in range(ng)]
    tok3 = lambda ref: [ref[0, :, sl].astype(F32) for sl in sls]
    tok2 = lambda ref: [ref[:, sl].astype(F32) for sl in sls]
    out, s_new = _scan_groups(tok3(r_ref), tok3(k_ref), tok3(v_ref), tok2(wl_ref), tok2(a_ref), tok2(g_ref),
                              tok2(kk_ref), tok2(ka_ref), tok2(rk_ref), tok2(gg_ref), tok2(gb_ref),
                              [state[p] for p in range(ng)], hd=hd, gn_eps=gn_eps)
    for p in range(ng):
        state[p] = s_new[p]
        o_ref[:, sls[p]] = out[p].astype(o_ref.dtype)


def _rwkv_scan(rkv, wl, a, g, k_k, k_a, r_k, gn_g, gn_b, batch, hd, gn_eps, chunk=64, groups=16):
    _, n, d = rkv.shape
    t = n // batch
    nc = t // chunk
    groups = min(groups, d // LANES)
    w = groups * LANES
    tok = lambda j: pl.BlockSpec((1, chunk, w), functools.partial(lambda b, p, c, j: (j, b * nc + c, p), j=j))
    tok2 = pl.BlockSpec((chunk, w), lambda b, p, c: (b * nc + c, p))
    par = pl.BlockSpec((1, w), lambda b, p, c: (0, p))
    return pl.pallas_call(
        functools.partial(_scan_kernel, hd=hd, gn_eps=gn_eps),
        grid=(batch, d // w, nc),
        in_specs=[tok(0), tok(1), tok(2), tok2, tok2, tok2, par, par, par, par, par],
        out_specs=tok2,
        out_shape=jax.ShapeDtypeStruct((n, d), BF16),
        scratch_shapes=[pltpu.VMEM((groups, LANES, LANES), F32)],
        compiler_params=_cparams(("parallel", "parallel", "arbitrary")),
        name="rwkv_scan",
    )(rkv, rkv, rkv, wl, a, g, k_k.reshape(1, d), k_a.reshape(1, d), r_k.reshape(1, d), gn_g.reshape(1, d),
      gn_b.reshape(1, d))


def _pad_lora(w_in, w_out):
    r = w_in.shape[1]
    rp = -(-r // LANES) * LANES
    return (jnp.pad(w_in, ((0, 0), (0, rp - r))).astype(BF16), jnp.pad(w_out, ((0, rp - r), (0, 0))).astype(BF16))


def kernel(x, ev_w_in, ev_b_a, ev_w_s, ev_b_s, ev_g_v, ev_b_v, ev_b_f, ev_w_out, rw_mu, rw_w_rkv, rw_w0, rw_w1, rw_w2, rw_a0, rw_a1, rw_a2, rw_g1, rw_g2, rw_k_k, rw_k_a, rw_r_k, rw_gn_g, rw_gn_b, rw_w_o, ln_g, ln_b, w_router, b_router, w_gu, w_down):
    batch, t, d = x.shape
    depth = ln_g.shape[0]
    alpha = (2 * depth) ** 0.25
    h = x.reshape(batch * t, d)
    for layer in range(depth):
        i = layer // 2
        if layer % 2 == 0:
            aw = ev_g_v.shape[1]
            heads = ev_b_f.shape[1]
            q_col = 2 * aw
            bw = heads * LANES
            k_col, v_col, f_col = q_col + bw, q_col + 2 * bw, q_col + 3 * bw
            w_in = ev_w_in[i]
            col = jnp.arange(v_col)
            q_scale = jnp.where((col >= q_col) & (col < k_col), LANES ** -0.5 * LOG2E, 1.0)
            proj = _matmul(h, (w_in[:, :v_col] * q_scale).astype(BF16), v_col, BF16)
            vt, f_logit = _proj_transposed(h, w_in[:, v_col:f_col].T.astype(BF16), w_in[:, f_col:].T, FOX_BLOCK)
            c = _fgate(f_logit, ev_b_f[i], batch)
            y_a = _sgu(proj, ev_b_a[i], ev_w_s[i], ev_b_s[i], ev_g_v[i], ev_b_v[i])
            y_b = _fox_attention(proj, vt, c, batch, heads, q_col, FOX_BLOCK)
            h, meta, counts = _proj_ln_route([y_a, y_b], ev_w_out[i].astype(BF16), h, ln_g[layer, 0],
                                             ln_b[layer, 0], alpha, w_router, b_router)
        else:
            hd = rw_r_k.shape[2]
            mu = rw_mu[i]
            rkv = _rkv_proj(h, mu[:3], rw_w_rkv[i].astype(BF16), t)
            w1, w2 = _pad_lora(rw_w1[i], rw_w2[i])
            a1, a2 = _pad_lora(rw_a1[i], rw_a2[i])
            g1, g2 = _pad_lora(rw_g1[i], rw_g2[i])
            wl, a, g = _lora(h, mu[3:], rw_w0[i], w1, w2, rw_a0[i], a1, a2, g1, g2, t)
            y = _rwkv_scan(rkv, wl, a, g, rw_k_k[i], rw_k_a[i], rw_r_k[i].reshape(-1), rw_gn_g[i], rw_gn_b[i],
                           batch, hd, hd * 1e-5)
            h, meta, counts = _proj_ln_route([y], rw_w_o[i].astype(BF16), h, ln_g[layer, 0], ln_b[layer, 0], alpha,
                                             w_router, b_router)
        h = _moe_ln(h, meta, counts, w_gu, w_down, layer, ln_g[layer, 1], ln_b[layer, 1], alpha)
    return h.reshape(batch, t, d)
```

```python
import functools

import jax
import jax.numpy as jnp
from jax import lax
from jax.experimental import pallas as pl
from jax.experimental.pallas import tpu as pltpu

F32 = jnp.float32
BF16 = jnp.bfloat16
HI = lax.Precision.HIGHEST

LN_EPS = 1e-5
N_GROUPS = 4
LANES = 128
FOX_BLOCK = 512
LOG2E = 1.4426950408889634
HALF_DECAY = 0.6065306597126334
DMA_QUEUES = 2
VMEM_LIMIT = 56 * 1024 * 1024


def _cparams(sem):
    return pltpu.CompilerParams(dimension_semantics=sem, vmem_limit_bytes=VMEM_LIMIT)


def _layer_norm(x, g, b, eps):
    mu = jnp.mean(x, -1, keepdims=True)
    xc = x - mu
    var = jnp.mean(xc * xc, -1, keepdims=True)
    return xc * lax.rsqrt(var + eps) * g + b


def _sigmoid(x):
    return 0.5 + 0.5 * jnp.tanh(0.5 * x)


def _nt(a, b, **kw):
    return lax.dot_general(a, b, (((1,), (1,)), ((), ())), preferred_element_type=F32, **kw)


def _tn(a, b, **kw):
    return lax.dot_general(a, b, (((0,), (0,)), ((), ())), preferred_element_type=F32, **kw)


def _bdot(a, b):
    return jnp.dot(a.astype(BF16), b.astype(BF16), preferred_element_type=F32)


def _mm_kernel(a_ref, w_ref, o_ref, a_bf16):
    @pl.when(pl.program_id(1) == 0)
    def _():
        a_bf16[...] = a_ref[...].astype(BF16)

    o_ref[...] = jnp.dot(a_bf16[...], w_ref[...], preferred_element_type=F32).astype(o_ref.dtype)


def _matmul(a, w, n_cols, out_dtype, tm=1024, tn=1024):
    m, k = a.shape
    tm = min(tm, m)
    tn = next(c for c in (tn, 256, LANES) if n_cols % c == 0)
    return pl.pallas_call(
        _mm_kernel,
        grid=(m // tm, n_cols // tn),
        in_specs=[pl.BlockSpec((tm, k), lambda i, j: (i, 0)), pl.BlockSpec((k, tn), lambda i, j: (0, j))],
        out_specs=pl.BlockSpec((tm, tn), lambda i, j: (i, j)),
        out_shape=jax.ShapeDtypeStruct((m, n_cols), out_dtype),
        scratch_shapes=[pltpu.VMEM((tm, k), BF16)],
        compiler_params=_cparams(("parallel", "arbitrary")),
        name="matmul",
    )(a, w)


def _proj_ln_kernel(*refs, n_in, alpha):
    a_refs, w_refs = refs[:n_in], refs[n_in:2 * n_in]
    res_ref, g_ref, b_ref, wr_ref, br_ref, o_ref, meta_ref, cnt_ref, carry = refs[2 * n_in:]

    @pl.when(pl.program_id(0) == 0)
    def _():
        carry[...] = jnp.zeros_like(carry)

    acc = alpha * res_ref[...]
    for a_ref, w_ref in zip(a_refs, w_refs):
        acc = acc + jnp.dot(a_ref[...], w_ref[...], preferred_element_type=F32)
    y = _layer_norm(acc, g_ref[...], b_ref[...], LN_EPS)
    o_ref[...] = y
    meta_ref[...] = _route_tile(y, wr_ref, br_ref, carry)
    cnt_ref[...] = jnp.broadcast_to(carry[...], cnt_ref.shape)


def _proj_ln_route(a_list, w, res, g, b, alpha, w_router, b_router, tm=512):
    m, d = res.shape
    n_in = len(a_list)
    kc = a_list[0].shape[1]
    n_experts = w_router.shape[1]
    wr = w_router.T
    wr_hi = wr.astype(BF16)
    wr = jnp.stack([wr_hi, (wr - wr_hi.astype(F32)).astype(BF16)])
    in_specs = [pl.BlockSpec((tm, kc), lambda i: (i, 0)) for _ in a_list]
    in_specs += [pl.BlockSpec((kc, d), functools.partial(lambda i, r: (r, 0), r=r)) for r in range(n_in)]
    in_specs += [pl.BlockSpec((tm, d), lambda i: (i, 0)), pl.BlockSpec((1, d), lambda i: (0, 0)),
                 pl.BlockSpec((1, d), lambda i: (0, 0)), pl.BlockSpec((2, n_experts, d), lambda i: (0, 0, 0)),
                 pl.BlockSpec((n_experts, 1), lambda i: (0, 0))]
    return pl.pallas_call(
        functools.partial(_proj_ln_kernel, n_in=n_in, alpha=alpha),
        grid=(m // tm,),
        in_specs=in_specs,
        out_specs=[pl.BlockSpec((tm, d), lambda i: (i, 0)), pl.BlockSpec((8, tm), lambda i: (0, i)),
                   pl.BlockSpec((n_experts, LANES), lambda i: (0, 0))],
        out_shape=[jax.ShapeDtypeStruct((m, d), F32), jax.ShapeDtypeStruct((8, m), F32),
                   jax.ShapeDtypeStruct((n_experts, LANES), F32)],
        scratch_shapes=[pltpu.VMEM((n_experts, 1), F32)],
        compiler_params=_cparams(("arbitrary",)),
        name="proj_ln_route",
    )(*a_list, *([w] * n_in), res, g.reshape(1, d), b.reshape(1, d), wr, b_router.reshape(n_experts, 1))


def _fgate_kernel(z_ref, bf_ref, c_ref, carry):
    @pl.when(pl.program_id(1) == 0)
    def _():
        carry[...] = jnp.zeros_like(carry)

    tm = z_ref.shape[2]
    z = z_ref[0] + bf_ref[...]
    log_f = jnp.minimum(z, 0.0) - jnp.log1p(jnp.exp(-jnp.abs(z)))
    row = lax.broadcasted_iota(jnp.int32, (tm, tm), 0)
    col = lax.broadcasted_iota(jnp.int32, (tm, tm), 1)
    upper = jnp.where(row <= col, 1.0, 0.0).astype(F32)
    c = jnp.dot(log_f, upper, preferred_element_type=F32, precision=HI) + carry[...]
    c_ref[0] = c
    carry[...] = carry[...] + jnp.sum(log_f, axis=-1, keepdims=True)


def _fgate(f_logit, b_f, batch):
    tiles, h, tm = f_logit.shape
    nt = tiles // batch
    return pl.pallas_call(
        _fgate_kernel,
        grid=(batch, nt),
        in_specs=[pl.BlockSpec((1, h, tm), lambda b, i: (b * nt + i, 0, 0)), pl.BlockSpec((h, 1), lambda b, i: (0, 0))],
        out_specs=pl.BlockSpec((1, h, tm), lambda b, i: (b, 0, i)),
        out_shape=jax.ShapeDtypeStruct((batch, h, nt * tm), F32),
        scratch_shapes=[pltpu.VMEM((h, 1), F32)],
        compiler_params=_cparams(("parallel", "arbitrary")),
        name="fgate",
    )(f_logit, b_f.reshape(h, 1))


def _gelu_tanh(x):
    return 0.5 * x * (1.0 + jnp.tanh(0.7978845608028654 * (x + 0.044715 * (x * x * x))))


def _sgu_kernel(z_ref, ba_ref, ws_ref, bs_ref, gv_ref, bv_ref, o_ref, *, chunk, groups):
    aw = o_ref.shape[1]
    gd = aw // groups
    z = _gelu_tanh(z_ref[...].astype(F32) + ba_ref[...])
    u = z[:, :aw]
    v = _layer_norm(z[:, aw:], gv_ref[...], bv_ref[...], LN_EPS).astype(BF16)
    row = lax.broadcasted_iota(jnp.int32, (chunk, chunk), 0)
    col = lax.broadcasted_iota(jnp.int32, (chunk, chunk), 1)
    causal = row >= col
    bs = bs_ref[...]
    for g in range(groups):
        w_g = jnp.where(causal, ws_ref[g], 0.0).astype(BF16)
        for c in range(z.shape[0] // chunk):
            rs = slice(c * chunk, (c + 1) * chunk)
            cs = slice(g * gd, (g + 1) * gd)
            s = jnp.dot(w_g, v[rs, cs], preferred_element_type=F32) + bs[:, g:g + 1]
            o_ref[rs, cs] = (u[rs, cs] * s).astype(o_ref.dtype)


def _sgu(proj, b_a, w_s, b_s, g_v, b_v, tm=512):
    n = proj.shape[0]
    groups, chunk, _ = w_s.shape
    aw = g_v.shape[0]
    return pl.pallas_call(
        functools.partial(_sgu_kernel, chunk=chunk, groups=groups),
        grid=(n // tm,),
        in_specs=[pl.BlockSpec((tm, 2 * aw), lambda i: (i, 0)), pl.BlockSpec((1, 2 * aw), lambda i: (0, 0)),
                  pl.BlockSpec((groups, chunk, chunk), lambda i: (0, 0, 0)),
                  pl.BlockSpec((chunk, groups), lambda i: (0, 0)), pl.BlockSpec((1, aw), lambda i: (0, 0)),
                  pl.BlockSpec((1, aw), lambda i: (0, 0))],
        out_specs=pl.BlockSpec((tm, aw), lambda i: (i, 0)),
        out_shape=jax.ShapeDtypeStruct((n, aw), BF16),
        compiler_params=_cparams(("parallel",)),
        name="sgu",
    )(proj, b_a.reshape(1, -1), w_s, b_s.T, g_v.reshape(1, aw), b_v.reshape(1, aw))


def _vt_kernel(x_ref, w_ref, wf_ref, o_ref, f_ref):
    x = x_ref[...]
    x_hi = x.astype(BF16)
    x_lo = (x - x_hi.astype(F32)).astype(BF16)
    o_ref[0] = _nt(w_ref[...], x_hi).astype(o_ref.dtype)
    f_ref[0] = _nt(wf_ref[0], x_hi) + _nt(wf_ref[0], x_lo) + _nt(wf_ref[1], x_hi)


def _proj_transposed(x, w_t, wf_t, blk):
    n, d = x.shape
    rows, rows_f = w_t.shape[0], wf_t.shape[0]
    wf_hi = wf_t.astype(BF16)
    wf = jnp.stack([wf_hi, (wf_t - wf_hi.astype(F32)).astype(BF16)])
    return pl.pallas_call(
        _vt_kernel,
        grid=(n // blk,),
        in_specs=[pl.BlockSpec((blk, d), lambda i: (i, 0)), pl.BlockSpec((rows, d), lambda i: (0, 0)),
                  pl.BlockSpec((2, rows_f, d), lambda i: (0, 0, 0))],
        out_specs=[pl.BlockSpec((1, rows, blk), lambda i: (i, 0, 0)),
                   pl.BlockSpec((1, rows_f, blk), lambda i: (i, 0, 0))],
        out_shape=[jax.ShapeDtypeStruct((n // blk, rows, blk), BF16),
                   jax.ShapeDtypeStruct((n // blk, rows_f, blk), F32)],
        compiler_params=_cparams(("parallel",)),
        name="proj_transposed",
    )(x, w_t, wf)


def _fox_kernel(q_ref, k_ref, vt_ref, cq_ref, ck_ref, o_ref, m_scr, l_scr, acc_scr, *, blk, nh):
    qi = pl.program_id(2)
    heads = range(nh)
    hs = [slice(h * LANES, (h + 1) * LANES) for h in heads]
    m_scr[...] = jnp.full_like(m_scr, -jnp.inf)
    l_scr[...] = jnp.zeros_like(l_scr)
    acc_scr[...] = jnp.zeros_like(acc_scr)
    q = [q_ref[:, hs[h]] for h in heads]
    cq = [cq_ref[0, h, pl.ds(qi, 1), :] * LOG2E for h in heads]

    def step(ki, masked):
        ks = pl.multiple_of(ki * blk, blk)
        t = [_nt(k_ref[pl.ds(ks, blk), hs[h]], q[h]) - ck_ref[0, h, pl.ds(ks, blk), :] * LOG2E for h in heads]
        if masked:
            row = lax.broadcasted_iota(jnp.int32, (blk, blk), 0)
            col = lax.broadcasted_iota(jnp.int32, (blk, blk), 1)
            t = [jnp.where(row <= col, x, -jnp.inf) for x in t]
        m_prev = [m_scr[h] for h in heads]
        m_new = [jnp.maximum(m_prev[h], cq[h] + jnp.max(t[h], axis=0, keepdims=True)) for h in heads]
        p = [jnp.exp2(t[h] - (m_new[h] - cq[h])) for h in heads]
        corr = [jnp.exp2(m_prev[h] - m_new[h]) for h in heads]
        for h in heads:
            l_scr[h] = corr[h] * l_scr[h] + jnp.sum(p[h], axis=0, keepdims=True)
            acc_scr[h] = corr[h] * acc_scr[h] + jnp.dot(vt_ref[ki, hs[h], :], p[h].astype(BF16),
                                                         preferred_element_type=F32)
            m_scr[h] = m_new[h]

    def body(ki, carry):
        step(ki, False)
        return carry

    lax.fori_loop(0, qi, body, 0)
    step(qi, True)
    for h in heads:
        o_ref[:, hs[h]] = jnp.transpose(acc_scr[h] / l_scr[h]).astype(o_ref.dtype)


def _fox_attention(proj, vt, c, batch, heads, q_col, blk, nh=4):
    n = proj.shape[0]
    t = n // batch
    nb = t // blk
    dh = LANES
    nh = min(nh, heads)
    w = nh * dh
    q0, k0 = q_col // w, (q_col + heads * dh) // w
    c_col = c.reshape(batch, heads, t, 1)
    c_row = c.reshape(batch, heads, nb, blk)
    return pl.pallas_call(
        functools.partial(_fox_kernel, blk=blk, nh=nh),
        grid=(batch, heads // nh, nb),
        in_specs=[pl.BlockSpec((blk, w), lambda b, h, i: (b * nb + i, q0 + h)),
                  pl.BlockSpec((t, w), lambda b, h, i: (b, k0 + h)),
                  pl.BlockSpec((nb, w, blk), lambda b, h, i: (b, h, 0)),
                  pl.BlockSpec((1, nh, nb, blk), lambda b, h, i: (b, h, 0, 0)),
                  pl.BlockSpec((1, nh, t, 1), lambda b, h, i: (b, h, 0, 0))],
        out_specs=pl.BlockSpec((blk, w), lambda b, h, i: (b * nb + i, h)),
        out_shape=jax.ShapeDtypeStruct((n, heads * dh), BF16),
        scratch_shapes=[pltpu.VMEM((nh, 1, blk), F32), pltpu.VMEM((nh, 1, blk), F32),
                        pltpu.VMEM((nh, dh, blk), F32)],
        compiler_params=_cparams(("parallel", "parallel", "arbitrary")),
        name="fox_attention",
    )(proj, proj, vt, c_row, c_col)


def _first_max(p, idx, valid):
    pm = jnp.where(valid, p, -2.0)
    m = jnp.max(pm, axis=0, keepdims=True)
    first = jnp.min(jnp.where(pm == m, idx, float(p.shape[0])), axis=0, keepdims=True)
    return m, first


def _route_tile(x, wr_ref, br_ref, carry):
    tm = x.shape[0]
    n_experts = br_ref.shape[0]
    per = n_experts // N_GROUPS
    x_hi = x.astype(BF16)
    x_lo = (x - x_hi.astype(F32)).astype(BF16)
    logits = _nt(wr_ref[0], x_hi) + _nt(wr_ref[0], x_lo) + _nt(wr_ref[1], x_hi) + br_ref[...]
    e = jnp.exp(logits - jnp.max(logits, axis=0, keepdims=True))
    probs = e / jnp.sum(e, axis=0, keepdims=True)
    idx = lax.broadcasted_iota(jnp.int32, (n_experts, tm), 0).astype(F32)
    best_score = jnp.full((1, tm), -1.0, F32)
    best_group = jnp.zeros((1, tm), F32)
    for grp in range(N_GROUPS):
        in_g = (idx >= grp * per) & (idx < (grp + 1) * per)
        m1, i1 = _first_max(probs, idx, in_g)
        m2, _ = _first_max(probs, idx, in_g & (idx != i1))
        score = m1 + m2
        take = score > best_score
        best_score = jnp.where(take, score, best_score)
        best_group = jnp.where(take, float(grp), best_group)
    in_sel = (idx >= best_group * per) & (idx < (best_group + 1) * per)
    p1, i1 = _first_max(probs, idx, in_sel)
    p2, i2 = _first_max(probs, idx, in_sel & (idx != i1))
    tot = p1 + p2
    row = lax.broadcasted_iota(jnp.int32, (tm, tm), 0)
    col = lax.broadcasted_iota(jnp.int32, (tm, tm), 1)
    earlier = jnp.where(row < col, 1.0, 0.0).astype(BF16)
    onehot = jnp.where((idx == i1) | (idx == i2), 1.0, 0.0)
    seen = jnp.dot(onehot.astype(BF16), earlier, preferred_element_type=F32) + carry[...]
    r1 = jnp.sum(jnp.where(idx == i1, seen, 0.0), axis=0, keepdims=True)
    r2 = jnp.sum(jnp.where(idx == i2, seen, 0.0), axis=0, keepdims=True)
    zero = jnp.zeros_like(r1)
    carry[...] = carry[...] + jnp.sum(onehot, axis=1, keepdims=True)
    return jnp.concatenate([i1, i2, p1 / tot, p2 / tot, r1, r2, zero, zero], axis=0)


def _row_copy(src_ref, src_row, dst_ref, dst_row, sem):
    return pltpu.make_async_copy(src_ref.at[pl.ds(src_row, 1)], dst_ref.at[pl.ds(dst_row, 1)], sem)


def _dispatch_kernel(zid_ref, pos_ref, x_hbm, xs_ref, zbuf, stage, row_sems, load_sems, zsem, *, tm, tok):
    i = pl.program_id(0)
    nt = pl.num_programs(0)
    n_slots = pos_ref.shape[0]
    n_stage = stage.shape[0]

    def zero_tiles(fn):
        for j in range(zid_ref.shape[0]):
            new_id = zid_ref[j] != zid_ref[max(j - 1, 0)] if j else True

            @pl.when(new_id)
            def _():
                fn(pltpu.make_async_copy(zbuf, xs_ref.at[pl.ds(zid_ref[j] * tm, tm)], zsem))

    def load(t, slot):
        return pltpu.make_async_copy(x_hbm.at[pl.ds(t * tok, tok)], stage.at[slot], load_sems.at[slot])

    def drain(slot):
        for _ in range(tok * n_slots):
            _row_copy(stage.at[slot], 0, xs_ref, 0, row_sems.at[slot]).wait()

    @pl.when(i == 0)
    def _():
        zbuf[...] = jnp.zeros_like(zbuf)
        zero_tiles(lambda cp: cp.start())
        load(0, 0).start()
        zero_tiles(lambda cp: cp.wait())

    @pl.when((i == 0) & (nt > 1))
    def _():
        load(1, 1).start()

    cur = i % n_stage
    prev = (i + n_stage - 1) % n_stage
    load(i, cur).wait()
    for r in range(tok):
        for s in range(n_slots):
            _row_copy(stage.at[cur], r, xs_ref, pos_ref[s, r], row_sems.at[cur]).start(priority=s % DMA_QUEUES)

    @pl.when(i > 0)
    def _():
        drain(prev)

    @pl.when(i + 2 < nt)
    def _():
        load(i + 2, prev).start()

    @pl.when(i == nt - 1)
    def _():
        drain(cur)


def _dispatch(x, pos, zero_ids, n_rows, tm, tok=512):
    n, d = x.shape
    grid_spec = pltpu.PrefetchScalarGridSpec(
        num_scalar_prefetch=1,
        grid=(n // tok,),
        in_specs=[pl.BlockSpec((pos.shape[0], tok), lambda i, z: (0, i), memory_space=pltpu.SMEM),
                  pl.BlockSpec(memory_space=pl.ANY)],
        out_specs=pl.BlockSpec(memory_space=pl.ANY),
        scratch_shapes=[pltpu.VMEM((tm, d), x.dtype), pltpu.VMEM((3, tok, d), x.dtype),
                        pltpu.SemaphoreType.DMA((3,)), pltpu.SemaphoreType.DMA((3,)), pltpu.SemaphoreType.DMA],
    )
    return pl.pallas_call(
        functools.partial(_dispatch_kernel, tm=tm, tok=tok),
        grid_spec=grid_spec,
        out_shape=jax.ShapeDtypeStruct((n_rows, d), x.dtype),
        compiler_params=_cparams(("arbitrary",)),
        name="moe_dispatch",
    )(zero_ids, pos, x)


def _experts_kernel(te_ref, nx_ref, na_ref, xs_ref, wgu_hbm, wd_hbm, ys_ref, wgu_f, wd_f, wgu_b, wd_b, sems, *, layer):
    j = pl.program_id(0)
    live = j < na_ref[0]
    fresh = (j == 0) | (te_ref[j] != te_ref[jnp.maximum(j - 1, 0)])

    def fetch(e):
        return (pltpu.make_async_copy(wgu_hbm.at[layer, e], wgu_f, sems.at[0]),
                pltpu.make_async_copy(wd_hbm.at[layer, e], wd_f, sems.at[1]))

    @pl.when(live & (j == 0))
    def _():
        for cp in fetch(te_ref[0]):
            cp.start()

    @pl.when(live & fresh)
    def _():
        for cp in fetch(te_ref[j]):
            cp.wait()
        wgu_b[...] = wgu_f[...].astype(BF16)
        wd_b[...] = wd_f[...].astype(BF16)

    @pl.when(live & fresh & (nx_ref[j] >= 0))
    def _():
        for cp in fetch(nx_ref[j]):
            cp.start()

    @pl.when(live)
    def _():
        de = wd_b.shape[0]
        gu = jnp.dot(xs_ref[...].astype(BF16), wgu_b[...], preferred_element_type=F32)
        gpart, upart = gu[:, :de], gu[:, de:]
        h = (gpart / (1.0 + jnp.exp(-gpart))) * upart
        ys_ref[...] = jnp.dot(h.astype(BF16), wd_b[...], preferred_element_type=F32)

    @pl.when(jnp.logical_not(live))
    def _():
        ys_ref[...] = jnp.zeros_like(ys_ref)


def _experts(xs, w_gu, w_down, layer, tile_expert, next_expert, n_active, tm):
    p, d = xs.shape
    de = w_down.shape[2]
    live = lambda j, na: jnp.maximum(jnp.minimum(j, na[0] - 1), 0)
    grid_spec = pltpu.PrefetchScalarGridSpec(
        num_scalar_prefetch=3,
        grid=(p // tm,),
        in_specs=[pl.BlockSpec((tm, d), lambda j, te, nx, na: (live(j, na), 0)),
                  pl.BlockSpec(memory_space=pl.ANY), pl.BlockSpec(memory_space=pl.ANY)],
        out_specs=pl.BlockSpec((tm, d), lambda j, te, nx, na: (j, 0)),
        scratch_shapes=[pltpu.VMEM((d, 2 * de), F32), pltpu.VMEM((de, d), F32), pltpu.VMEM((d, 2 * de), BF16),
                        pltpu.VMEM((de, d), BF16), pltpu.SemaphoreType.DMA((2,))],
    )
    return pl.pallas_call(
        functools.partial(_experts_kernel, layer=layer),
        grid_spec=grid_spec,
        out_shape=jax.ShapeDtypeStruct((p, d), F32),
        compiler_params=_cparams(("arbitrary",)),
        name="moe_experts",
    )(tile_expert, next_expert, n_active, xs, w_gu, w_down)


def _combine_kernel(pos_ref, pos_next_ref, x_ref, gw_ref, ys_ref, g_ref, b_ref, o_ref, buf, sems, *, alpha):
    i = pl.program_id(0)
    slot = i % 2
    tm = x_ref.shape[0]
    n_slots = pos_ref.shape[0]

    def rows(p_ref, sl, fn):
        for r in range(tm):
            for s in range(n_slots):
                fn(_row_copy(ys_ref, p_ref[s, r], buf.at[sl, s], r, sems.at[sl]), s)

    start = lambda cp, s: cp.start(priority=s % DMA_QUEUES)

    @pl.when(i == 0)
    def _():
        rows(pos_ref, 0, start)

    @pl.when(i + 1 < pl.num_programs(0))
    def _():
        rows(pos_next_ref, 1 - slot, start)

    rows(pos_ref, slot, lambda cp, s: cp.wait())
    gw = gw_ref[...]
    y = alpha * x_ref[...]
    for s in range(n_slots):
        y = y + gw[:, s:s + 1] * buf[slot, s]
    o_ref[...] = _layer_norm(y, g_ref[...], b_ref[...], LN_EPS)


def _combine_ln(x, ys, pos, gw, g, b, alpha, tm=256):
    n, d = x.shape
    n_slots = pos.shape[0]
    nt = n // tm
    return pl.pallas_call(
        functools.partial(_combine_kernel, alpha=alpha),
        grid=(nt,),
        in_specs=[pl.BlockSpec((n_slots, tm), lambda i: (0, i), memory_space=pltpu.SMEM),
                  pl.BlockSpec((n_slots, tm), lambda i: (0, jnp.minimum(i + 1, nt - 1)), memory_space=pltpu.SMEM),
                  pl.BlockSpec((tm, d), lambda i: (i, 0)), pl.BlockSpec((tm, n_slots), lambda i: (i, 0)),
                  pl.BlockSpec(memory_space=pl.ANY), pl.BlockSpec((1, d), lambda i: (0, 0)),
                  pl.BlockSpec((1, d), lambda i: (0, 0))],
        out_specs=pl.BlockSpec((tm, d), lambda i: (i, 0)),
        out_shape=jax.ShapeDtypeStruct((n, d), F32),
        scratch_shapes=[pltpu.VMEM((2, n_slots, tm, d), F32), pltpu.SemaphoreType.DMA((2,))],
        compiler_params=_cparams(("arbitrary",)),
        name="moe_combine",
    )(pos, pos, x, gw, ys, g.reshape(1, d), b.reshape(1, d))


def _moe_ln(x, meta, counts, w_gu, w_down, layer, g, b, alpha, tm=256):
    n, d = x.shape
    n_experts = w_down.shape[1]
    eid = meta[0:2].astype(jnp.int32)
    gw = meta[2:4].T
    rank = meta[4:6].astype(jnp.int32)
    counts = counts[:, 0].astype(jnp.int32)
    tiles = (counts + tm - 1) // tm
    tile_end = jnp.cumsum(tiles)
    row_off = (tile_end - tiles) * tm
    n_tiles = (eid.size + tm - 1) // tm + n_experts
    tile_expert = jnp.sum(jnp.arange(n_tiles)[:, None] >= tile_end[None, :], axis=1).astype(jnp.int32)
    tile_expert = jnp.minimum(tile_expert, n_experts - 1)
    n_active = tile_end[-1:].astype(jnp.int32)
    ids = jnp.arange(n_experts)
    later = (ids[None, :] > ids[:, None]) & (tiles[None, :] > 0)
    next_of = jnp.min(jnp.where(later, ids[None, :], n_experts), axis=1)
    next_of = jnp.where(next_of < n_experts, next_of, -1).astype(jnp.int32)
    lookup = lambda table, ids: jnp.sum(jnp.where(ids[..., None] == jnp.arange(n_experts), table, 0), axis=-1)
    next_expert = lookup(next_of, tile_expert)
    pos = lookup(row_off, eid) + rank
    last_tile = jnp.maximum(tile_end - 1, 0)
    idle_tile = jnp.minimum(tile_end[-1] + jnp.arange(n_experts), n_tiles - 1)
    zero_ids = jnp.concatenate([last_tile, idle_tile]).astype(jnp.int32)
    xs = _dispatch(x, pos, zero_ids, n_tiles * tm, tm)
    ys = _experts(xs, w_gu, w_down, layer, tile_expert, next_expert, n_active, tm)
    return _combine_ln(x, ys, pos, gw, g, b, alpha)


def _token_shift(x_ref, prev_ref, first):
    x = x_ref[...]
    prev_row = jnp.where(first, 0.0, prev_ref[7:8, :])
    row = lax.broadcasted_iota(jnp.int32, x.shape, 0)
    return x, jnp.where(row == 0, prev_row, pltpu.roll(x, 1, 0))


def _rkv_kernel(x_ref, prev_ref, mu_ref, w_ref, o_ref, *, tiles_per_seq):
    first = pl.program_id(0) % tiles_per_seq == 0
    x, xp = _token_shift(x_ref, prev_ref, first)
    xm = x + (xp - x) * mu_ref[0]
    o_ref[0] = jnp.dot(xm.astype(BF16), w_ref[0], preferred_element_type=F32).astype(o_ref.dtype)


def _rkv_proj(x, mu3, w_rkv, t, tm=1024):
    n, d = x.shape
    sub = tm // 8
    return pl.pallas_call(
        functools.partial(_rkv_kernel, tiles_per_seq=t // tm),
        grid=(n // tm, 3),
        in_specs=[pl.BlockSpec((tm, d), lambda i, j: (i, 0)),
                  pl.BlockSpec((8, d), lambda i, j: (jnp.maximum(i * sub - 1, 0), 0)),
                  pl.BlockSpec((1, 1, d), lambda i, j: (j, 0, 0)), pl.BlockSpec((1, d, d), lambda i, j: (j, 0, 0))],
        out_specs=pl.BlockSpec((1, tm, d), lambda i, j: (j, i, 0)),
        out_shape=jax.ShapeDtypeStruct((3, n, d), BF16),
        compiler_params=_cparams(("parallel", "arbitrary")),
        name="rkv_proj",
    )(x, x, mu3.reshape(3, 1, d), w_rkv)


def _lora_kernel(x_ref, prev_ref, mu_ref, w0_ref, w1_ref, w2_ref, a0_ref, a1_ref, a2_ref, g1_ref, g2_ref, wl_ref,
                 a_ref, g_ref, *, tiles_per_seq):
    first = pl.program_id(0) % tiles_per_seq == 0
    x, xp = _token_shift(x_ref, prev_ref, first)
    dx = xp - x

    def mm(a, w_ref):
        return jnp.dot(a.astype(BF16), w_ref[...], preferred_element_type=F32)

    zw = w0_ref[...] + mm(jnp.tanh(mm(x + dx * mu_ref[0:1, :], w1_ref)), w2_ref)
    wl_ref[...] = -HALF_DECAY * _sigmoid(zw)
    za = a0_ref[...] + mm(mm(x + dx * mu_ref[1:2, :], a1_ref), a2_ref)
    a_ref[...] = _sigmoid(za).astype(a_ref.dtype)
    zg = mm(x + dx * mu_ref[2:3, :], g1_ref)
    g_ref[...] = mm(_sigmoid(zg), g2_ref).astype(g_ref.dtype)


def _lora(x, mu3, w0, w1, w2, a0, a1, a2, g1, g2, t, tm=512):
    n, d = x.shape
    sub = tm // 8
    const = lambda shape: pl.BlockSpec(shape, lambda i: (0, 0))
    row = pl.BlockSpec((tm, d), lambda i: (i, 0))
    return pl.pallas_call(
        functools.partial(_lora_kernel, tiles_per_seq=t // tm),
        grid=(n // tm,),
        in_specs=[row, pl.BlockSpec((8, d), lambda i: (jnp.maximum(i * sub - 1, 0), 0)), const((3, d)),
                  const((1, d)), const(w1.shape), const(w2.shape), const((1, d)), const(a1.shape), const(a2.shape),
                  const(g1.shape), const(g2.shape)],
        out_specs=[row, row, row],
        out_shape=[jax.ShapeDtypeStruct((n, d), F32), jax.ShapeDtypeStruct((n, d), BF16),
                   jax.ShapeDtypeStruct((n, d), BF16)],
        compiler_params=_cparams(("parallel",)),
        name="lora",
    )(x, x, mu3, w0.reshape(1, d), w1, w2, a0.reshape(1, d), a1, a2, g1, g2)


def _scan_groups(r, k, v, lw, a_gate, g, kk_w, ka_w, rk_w, gg, gb, s0, *, hd, gn_eps):
    c = r[0].shape[0]
    nh = LANES // hd
    hc = nh * c
    each = lambda f, *xs: [f(*x) for x in zip(*xs)]
    lane = lax.broadcasted_iota(jnp.int32, (1, LANES), 1)
    head_masks = [(lane >= h * hd) & (lane < (h + 1) * hd) for h in range(nh)]
    lr = lax.broadcasted_iota(jnp.int32, (LANES, LANES), 0) // hd
    lc = lax.broadcasted_iota(jnp.int32, (LANES, LANES), 1) // hd
    same_head = jnp.where(lr == lc, 1.0, 0.0).astype(BF16)
    trow = lax.broadcasted_iota(jnp.int32, (c, c), 0)
    tcol = lax.broadcasted_iota(jnp.int32, (c, c), 1)
    lower = jnp.where(trow >= tcol, 1.0, 0.0).astype(BF16)
    prow = lax.broadcasted_iota(jnp.int32, (hc, hc), 0)
    pcol = lax.broadcasted_iota(jnp.int32, (hc, hc), 1)
    same_blk = (prow // c) == (pcol // c)
    strict = same_blk & (prow > pcol)
    incl = same_blk & (prow >= pcol)
    eye = jnp.where(prow == pcol, 1.0, 0.0).astype(F32)

    def split(x):
        hi = x.astype(BF16)
        return hi, (x - hi.astype(F32)).astype(BF16)

    def head_sum(xs):
        return [jnp.dot(x.astype(BF16), same_head, preferred_element_type=F32) for x in xs]

    def per_head(x):
        return jnp.concatenate([jnp.where(hm, x, 0.0) for hm in head_masks], axis=0).astype(BF16)

    kk = each(lambda k_, w_: k_ * w_, k, kk_w)
    kk_n = head_sum(each(lambda x: x * x, kk))
    kk = each(lambda x, n_: x / jnp.maximum(jnp.sqrt(n_), 1e-12), kk, kk_n)
    k2 = each(lambda k_, a_, w_: k_ * (1.0 + (a_ - 1.0) * w_), k, a_gate, ka_w)
    lw_parts = each(split, lw)
    cum = [jnp.dot(lower, hi, preferred_element_type=F32) + jnp.dot(lower, lo, preferred_element_type=F32)
           for hi, lo in lw_parts]
    gam = each(jnp.exp, cum)
    inv_gam = each(lambda x: jnp.exp(-x), cum)
    gam_end = each(lambda x: x[c - 1:c, :], gam)
    a_t = each(lambda kk_, cum_, lw_: -kk_ * jnp.exp(cum_ - lw_), kk, cum, lw)
    b_t = each(lambda kk_, a_, ig: kk_ * a_ * ig, kk, a_gate, inv_gam)
    k_t = each(lambda k2_, ig: k2_ * ig, k2, inv_gam)
    r_t = each(lambda r_, gm: r_ * gm, r, gam)

    s0b = each(lambda x: x.astype(BF16), s0)
    ar_s0 = each(lambda a_, r_, s_: _nt(jnp.concatenate([a_, r_], axis=0).astype(BF16), s_), a_t, r_t, s0b)
    ar2 = each(lambda a_, r_: jnp.concatenate([per_head(a_), per_head(r_)], axis=0), a_t, r_t)
    bk2 = each(lambda b_, k_: jnp.concatenate([per_head(b_), per_head(k_)], axis=0), b_t, k_t)
    v2 = each(per_head, v)
    gmat = each(_nt, ar2, bk2)
    a_ab = each(lambda x: jnp.where(strict, x[:hc, :hc], 0.0), gmat)
    a_ak = each(lambda x: jnp.where(strict, x[:hc, hc:], 0.0), gmat)
    a_r = each(lambda x: jnp.concatenate([jnp.where(incl, x[hc:, :hc], 0.0), jnp.where(incl, x[hc:, hc:], 0.0)],
                                         axis=1).astype(BF16), gmat)
    rhs = each(lambda as0, ak, v_: jnp.concatenate([jnp.where(hm, as0[:c], 0.0) for hm in head_masks], axis=0)
               + _bdot(ak, v_), ar_s0, a_ak, v2)
    inv = each(lambda x: eye + x, a_ab)
    m = each(lambda x: _bdot(x, x), a_ab)
    n_pow = 2
    while 2 * n_pow < c:
        mp = each(lambda m_, p_: _bdot(jnp.concatenate([m_, p_], axis=0), m_), m, inv)
        inv = each(lambda p_, mp_: p_ + mp_[hc:], inv, mp)
        m = each(lambda mp_: mp_[:hc], mp)
        n_pow *= 2
    rhs = each(lambda rhs_, m_: rhs_ + _bdot(m_, rhs_), rhs, m)
    u2 = each(_bdot, inv, rhs)
    uv = each(lambda u_, v_: jnp.concatenate([u_.astype(BF16), v_], axis=0), u2, v2)
    y2 = each(lambda ar_, uv_: jnp.dot(ar_, uv_, preferred_element_type=F32), a_r, uv)
    y = each(lambda rs, y2_: rs[c:] + sum(y2_[h * c:(h + 1) * c, :] for h in range(nh)), ar_s0, y2)
    bkg = each(lambda b_, k_, ge: jnp.concatenate([per_head(b_ * ge), per_head(k_ * ge)], axis=0), b_t, k_t, gam_end)
    s_new = each(lambda s_, ge, uv_, bkg_: s_ * ge + _tn(uv_, bkg_), s0, gam_end, uv, bkg)

    inv_hd = 1.0 / hd
    mean = head_sum(y)
    yc = each(lambda y_, m_: y_ - m_ * inv_hd, y, mean)
    var = head_sum(each(lambda x: x * x, yc))
    yn = each(lambda yc_, var_, gg_, gb_: yc_ * lax.rsqrt(var_ * inv_hd + gn_eps) * gg_ + gb_, yc, var, gg, gb)
    rk_sum = head_sum(each(lambda r_, k2_, w_: r_ * k2_ * w_, r, k2, rk_w))
    out = each(lambda yn_, rk_, v_, g_: (yn_ + rk_ * v_) * g_, yn, rk_sum, v, g)
    return out, s_new


def _scan_kernel(r_ref, k_ref, v_ref, wl_ref, a_ref, g_ref, kk_ref, ka_ref, rk_ref, gg_ref, gb_ref, o_ref, state, *,
                 hd, gn_eps):
    @pl.when(pl.program_id(2) == 0)
    def _():
        state[...] = jnp.zeros_like(state)

    ng = state.shape[0]
    sls = [slice(p * LANES, (p + 1) * LANES) for p in range(ng)]
    tok3 = lambda ref: [ref[0, :, sl].astype(F32) for sl in sls]
    tok2 = lambda ref: [ref[:, sl].astype(F32) for sl in sls]
    out, s_new = _scan_groups(tok3(r_ref), tok3(k_ref), tok3(v_ref), tok2(wl_ref), tok2(a_ref), tok2(g_ref),
                              tok2(kk_ref), tok2(ka_ref), tok2(rk_ref), tok2(gg_ref), tok2(gb_ref),
                              [state[p] for p in range(ng)], hd=hd, gn_eps=gn_eps)
    for p in range(ng):
        state[p] = s_new[p]
        o_ref[:, sls[p]] = out[p].astype(o_ref.dtype)


def _rwkv_scan(rkv, wl, a, g, k_k, k_a, r_k, gn_g, gn_b, batch, hd, gn_eps, chunk=64, groups=16):
    _, n, d = rkv.shape
    t = n // batch
    nc = t // chunk
    groups = min(groups, d // LANES)
    w = groups * LANES
    tok = lambda j: pl.BlockSpec((1, chunk, w), functools.partial(lambda b, p, c, j: (j, b * nc + c, p), j=j))
    tok2 = pl.BlockSpec((chunk, w), lambda b, p, c: (b * nc + c, p))
    par = pl.BlockSpec((1, w), lambda b, p, c: (0, p))
    return pl.pallas_call(
        functools.partial(_scan_kernel, hd=hd, gn_eps=gn_eps),
        grid=(batch, d // w, nc),
        in_specs=[tok(0), tok(1), tok(2), tok2, tok2, tok2, par, par, par, par, par],
        out_specs=tok2,
        out_shape=jax.ShapeDtypeStruct((n, d), BF16),
        scratch_shapes=[pltpu.VMEM((groups, LANES, LANES), F32)],
        compiler_params=_cparams(("parallel", "parallel", "arbitrary")),
        name="rwkv_scan",
    )(rkv, rkv, rkv, wl, a, g, k_k.reshape(1, d), k_a.reshape(1, d), r_k.reshape(1, d), gn_g.reshape(1, d),
      gn_b.reshape(1, d))


def _pad_lora(w_in, w_out):
    r = w_in.shape[1]
    rp = -(-r // LANES) * LANES
    return (jnp.pad(w_in, ((0, 0), (0, rp - r))).astype(BF16), jnp.pad(w_out, ((0, rp - r), (0, 0))).astype(BF16))


def kernel(x, ev_w_in, ev_b_a, ev_w_s, ev_b_s, ev_g_v, ev_b_v, ev_b_f, ev_w_out, rw_mu, rw_w_rkv, rw_w0, rw_w1, rw_w2, rw_a0, rw_a1, rw_a2, rw_g1, rw_g2, rw_k_k, rw_k_a, rw_r_k, rw_gn_g, rw_gn_b, rw_w_o, ln_g, ln_b, w_router, b_router, w_gu, w_down):
    batch, t, d = x.shape
    depth = ln_g.shape[0]
    alpha = (2 * depth) ** 0.25
    h = x.reshape(batch * t, d)
    for layer in range(depth):
        i = layer // 2
        if layer % 2 == 0:
            aw = ev_g_v.shape[1]
            heads = ev_b_f.shape[1]
            q_col = 2 * aw
            bw = heads * LANES
            k_col, v_col, f_col = q_col + bw, q_col + 2 * bw, q_col + 3 * bw
            w_in = ev_w_in[i]
            col = jnp.arange(v_col)
            q_scale = jnp.where((col >= q_col) & (col < k_col), LANES ** -0.5 * LOG2E, 1.0)
            proj = _matmul(h, (w_in[:, :v_col] * q_scale).astype(BF16), v_col, BF16)
            vt, f_logit = _proj_transposed(h, w_in[:, v_col:f_col].T.astype(BF16), w_in[:, f_col:].T, FOX_BLOCK)
            c = _fgate(f_logit, ev_b_f[i], batch)
            y_a = _sgu(proj, ev_b_a[i], ev_w_s[i], ev_b_s[i], ev_g_v[i], ev_b_v[i])
            y_b = _fox_attention(proj, vt, c, batch, heads, q_col, FOX_BLOCK)
            h, meta, counts = _proj_ln_route([y_a, y_b], ev_w_out[i].astype(BF16), h, ln_g[layer, 0],
                                             ln_b[layer, 0], alpha, w_router, b_router)
        else:
            hd = rw_r_k.shape[2]
            mu = rw_mu[i]
            rkv = _rkv_proj(h, mu[:3], rw_w_rkv[i].astype(BF16), t)
            w1, w2 = _pad_lora(rw_w1[i], rw_w2[i])
            a1, a2 = _pad_lora(rw_a1[i], rw_a2[i])
            g1, g2 = _pad_lora(rw_g1[i], rw_g2[i])
            wl, a, g = _lora(h, mu[3:], rw_w0[i], w1, w2, rw_a0[i], a1, a2, g1, g2, t)
            y = _rwkv_scan(rkv, wl, a, g, rw_k_k[i], rw_k_a[i], rw_r_k[i].reshape(-1), rw_gn_g[i], rw_gn_b[i],
                           batch, hd, hd * 1e-5)
            h, meta, counts = _proj_ln_route([y], rw_w_o[i].astype(BF16), h, ln_g[layer, 0], ln_b[layer, 0], alpha,
                                             w_router, b_router)
        h = _moe_ln(h, meta, counts, w_gu, w_down, layer, ln_g[layer, 1], ln_b[layer, 1], alpha)
    return h.reshape(batch, t, d)
```
